```python
import math
import jax, jax.numpy as jnp
from jax import lax
import numpy as np

D_MODEL = 1024
BATCH = 2
SEQ = 8192
DEPTH = 4
DEC_BATCH = 32
DEC_SEQ = 4
PAST_LEN = 8192
PAGE_SIZE = 128

RW_HEAD = 64
RW_HEADS = 8
RW_W = RW_HEADS * RW_HEAD
DECAY_LORA = 32
AAA_LORA = 32
GATE_LORA = 96
RW_PROJ = 3 * RW_W + DECAY_LORA + AAA_LORA + GATE_LORA
RW_SPLITS = (RW_W, 2 * RW_W, 3 * RW_W, 3 * RW_W + DECAY_LORA, 3 * RW_W + DECAY_LORA + AAA_LORA)
RW_GN_EPS = 64e-5
ATT_HEADS = 4
HEAD_QK = 64
HEAD_V = 2 * HEAD_QK
ATT_QK_W = ATT_HEADS * 2 * HEAD_QK
ATT_V_W = ATT_HEADS * HEAD_V
ROPE_DIM = HEAD_QK // 4
ROPE_THETA = 500000.0
Q_BLOCK = 128
SUBLN_EPS = 1e-5
NEG_INF = -1e30
IN_W = RW_PROJ + 2 * ATT_QK_W + ATT_V_W + 2 * D_MODEL
IN_SPLITS = (RW_PROJ, RW_PROJ + ATT_QK_W, RW_PROJ + 2 * ATT_QK_W,
             RW_PROJ + 2 * ATT_QK_W + ATT_V_W, RW_PROJ + 2 * ATT_QK_W + ATT_V_W + D_MODEL)
N_EXPERTS = 16
N_GROUPS = 4
EPG = N_EXPERTS // N_GROUPS
TOP_K = 2
D_EXPERT = D_MODEL // 2
NORM_EPS = 1e-6

kernel_name = 'rwkv7_diffattn_gated_moe_decoder_step'


def _rmsnorm(x, g, eps):
    xf = x.astype(jnp.float32)
    y = xf * lax.rsqrt(jnp.mean(xf * xf, axis=-1, keepdims=True) + eps)
    return (y * g.astype(jnp.float32)).astype(x.dtype)


def _partial_rope(x, pos):
    half = ROPE_DIM // 2
    inv = ROPE_THETA ** (-jnp.arange(0, ROPE_DIM, 2, dtype=jnp.float32) / ROPE_DIM)
    ang = pos.astype(jnp.float32)[:, None] * inv[None, :]
    cos = jnp.cos(ang)[None, :, None, None, :]
    sin = jnp.sin(ang)[None, :, None, None, :]
    xr = x[..., :ROPE_DIM].astype(jnp.float32)
    x1, x2 = xr[..., :half], xr[..., half:]
    rot = jnp.concatenate([x1 * cos - x2 * sin, x2 * cos + x1 * sin], axis=-1)
    return jnp.concatenate([rot.astype(x.dtype), x[..., ROPE_DIM:]], axis=-1)


def _wkv_scan(S0, r, w, kk, a, k, v):
    def step(S, inp):
        r_t, w_t, kk_t, a_t, k_t, v_t = inp
        sa = jnp.einsum('bhij,bhj->bhi', S, -kk_t)
        S = (S * w_t[:, :, None, :] + sa[..., None] * (kk_t * a_t)[:, :, None, :]
             + v_t[..., None] * k_t[:, :, None, :])
        return S, jnp.einsum('bhij,bhj->bhi', S, r_t)
    xs = tuple(jnp.swapaxes(t, 0, 1) for t in (r, w, kk, a, k, v))
    S, ys = lax.scan(step, S0, xs)
    return S, jnp.swapaxes(ys, 0, 1)


def _rwkv_branch(z, z_prev0, S0, p):
    f32 = jnp.float32
    B, T, _ = z.shape
    z_prev = jnp.concatenate([z_prev0[:, None, :].astype(z.dtype), z[:, :-1]], axis=1)
    zs = z + (z_prev - z) * p['rw_mu']
    r, k, v, wd, ad, gd = jnp.split(zs, RW_SPLITS, axis=-1)
    w_log = -jax.nn.softplus(-(p['rw_w0'] + jnp.tanh(wd) @ p['rw_w_up']).astype(f32)) - 0.5
    decay = jnp.exp(-jnp.exp(w_log))
    a = jax.nn.sigmoid((p['rw_a0'] + ad @ p['rw_a_up']).astype(f32))
    g = (jax.nn.sigmoid(gd) @ p['rw_g_up']).astype(f32)
    heads = lambda t: t.astype(f32).reshape(B, T, RW_HEADS, RW_HEAD)
    kk = heads(k * p['rw_k_k'])
    kk = kk / jnp.maximum(jnp.linalg.norm(kk, axis=-1, keepdims=True), 1e-12)
    k2 = k.astype(f32) * (1.0 + (a - 1.0) * p['rw_k_a'].astype(f32))
    rh, kh, vh, ah = heads(r), heads(k2), heads(v), heads(a)
    S, y = _wkv_scan(S0.astype(f32), rh, heads(decay), kk, ah, kh, vh)
    mu = jnp.mean(y, axis=-1, keepdims=True)
    var = jnp.mean(jnp.square(y - mu), axis=-1, keepdims=True)
    yn = ((y - mu) * lax.rsqrt(var + RW_GN_EPS)).reshape(B, T, RW_W)
    yn = yn * p['rw_ln_w'].astype(f32) + p['rw_ln_b'].astype(f32)
    bonus = jnp.sum(rh * kh * p['rw_r_k'].astype(f32), axis=-1, keepdims=True) * vh
    out = ((yn + bonus.reshape(B, T, RW_W)) * g).astype(z.dtype)
    return out @ p['w_rw_out'], z[:, -1], S.astype(z.dtype)


def _diff_attend(q, k, v, q_pos, k_pos, lam):
    s = jnp.einsum('bqhmd,bkhmd->bhmqk', q, k).astype(jnp.float32) * (HEAD_QK ** -0.5)
    mask = k_pos[None, :] <= q_pos[:, None]
    pr = jax.nn.softmax(jnp.where(mask, s, NEG_INF), axis=-1)
    att = pr[:, :, 0] - lam * pr[:, :, 1]
    return jnp.einsum('bhqk,bkhd->bqhd', att.astype(v.dtype), v)


def _prompt_attend(q, k, v, pos, lam):
    B, T = q.shape[0], q.shape[1]
    nb = T // Q_BLOCK
    qb = jnp.swapaxes(q.reshape(B, nb, Q_BLOCK, ATT_HEADS, 2, HEAD_QK), 0, 1)
    pb = pos.reshape(nb, Q_BLOCK)
    ob = lax.map(lambda blk: _diff_attend(blk[0], k, v, blk[1], pos, lam), (qb, pb))
    return jnp.swapaxes(ob, 0, 1).reshape(B, T, ATT_HEADS, HEAD_V)


def _diff_branch(q, k, v, pos, k_past, v_past, p, lam_init):
    f32 = jnp.float32
    B, T, _ = q.shape
    q = _partial_rope(q.reshape(B, T, ATT_HEADS, 2, HEAD_QK), pos)
    k = _partial_rope(k.reshape(B, T, ATT_HEADS, 2, HEAD_QK), pos)
    v = v.reshape(B, T, ATT_HEADS, HEAD_V)
    lam = (jnp.exp(jnp.sum(p['att_lq1'].astype(f32) * p['att_lk1'].astype(f32)))
           - jnp.exp(jnp.sum(p['att_lq2'].astype(f32) * p['att_lk2'].astype(f32))) + lam_init)
    if k_past is None:
        o = _prompt_attend(q, k, v, pos, lam)
    else:
        k_all = jnp.concatenate([k_past.astype(k.dtype), k], axis=1)
        v_all = jnp.concatenate([v_past.astype(v.dtype), v], axis=1)
        o = _diff_attend(q, k_all, v_all, pos, jnp.arange(k_all.shape[1]), lam)
    o = _rmsnorm(o, p['att_subln'], SUBLN_EPS) * (1.0 - lam_init)
    return o.reshape(B, T, ATT_V_W) @ p['w_att_out'], k, v


def _moe(h, w_router, e_bias, w1, w3, w2):
    B, T, D = h.shape
    hf = h.reshape(B * T, D)
    s = jax.nn.sigmoid((hf @ w_router).astype(jnp.float32))
    sel = (s + e_bias.astype(jnp.float32)).reshape(-1, N_GROUPS, EPG)
    grp = jnp.sum(lax.top_k(sel, 2)[0], axis=-1)
    g_idx = jnp.argmax(grp, axis=-1)
    sel_g = jnp.take_along_axis(sel, g_idx[:, None, None], axis=1)[:, 0]
    _, loc = lax.top_k(sel_g, TOP_K)
    e_idx = g_idx[:, None] * EPG + loc
    wts = jnp.take_along_axis(s, e_idx, axis=1)
    wts = wts / jnp.sum(wts, axis=-1, keepdims=True)
    gates = jnp.sum(jax.nn.one_hot(e_idx, N_EXPERTS, dtype=jnp.float32) * wts[..., None], axis=1)
    act = jax.nn.silu(jnp.einsum('nd,edf->nef', hf, w1)) * jnp.einsum('nd,edf->nef', hf, w3)
    out = jnp.einsum('nef,efd->nd', act * gates[..., None].astype(act.dtype), w2)
    return out.reshape(B, T, D)


def _layer(x, c, pos, z_prev0, S0, k_past, v_past, p, lam_init):
    mod = jax.nn.silu(c) @ p['w_ada'] + p['b_ada']
    sh1, sc1, gt1, sh2, sc2, gt2 = [m[:, None, :] for m in jnp.split(mod, 6, axis=-1)]
    h = _rmsnorm(x, p['norm1_g'], NORM_EPS) * (1.0 + sc1) + sh1
    z, q, k, v, g_rw, g_att = jnp.split(h @ p['w_in'], IN_SPLITS, axis=-1)
    y_rw, z_last, S = _rwkv_branch(z, z_prev0, S0, p)
    y_att, k_rows, v_rows = _diff_branch(q, k, v, pos, k_past, v_past, p, lam_init)
    merged = jax.nn.sigmoid(g_rw) * y_rw + jax.nn.sigmoid(g_att) * y_att
    x = x + gt1 * (merged @ p['w_o'])
    h2 = _rmsnorm(x, p['norm2_g'], NORM_EPS) * (1.0 + sc2) + sh2
    x = x + gt2 * _moe(h2, p['w_router'], p['e_bias'], p['moe_w1'], p['moe_w3'], p['moe_w2'])
    return x, z_last, S, k_rows, v_rows


def setup_inputs(seed: int = 0) -> dict:
    key = jax.random.key(seed)
    ks = iter(jax.random.split(key, 48))
    f32 = jnp.float32
    nrm = lambda shape, scale: scale * jax.random.normal(next(ks), shape, f32)
    uni = lambda shape, lo, hi: jax.random.uniform(next(ks), shape, f32, lo, hi)
    n_pages = PAST_LEN // PAGE_SIZE
    n_used = DEC_BATCH * n_pages
    n_pool = n_used + max(1, n_used // 4)
    perm = jax.random.permutation(next(ks), n_pool)
    page_table = perm[:n_used].reshape(DEC_BATCH, n_pages).astype(jnp.int32)
    D = D_MODEL
    return {
        'x_prompt': nrm((BATCH, SEQ, D), 1.0),
        'x_sample': nrm((DEC_BATCH, DEC_SEQ, D), 1.0),
        'c_prompt': nrm((BATCH, D), 1.0),
        'c_sample': nrm((DEC_BATCH, D), 1.0),
        'cache_k': nrm((n_pool, DEPTH, PAGE_SIZE, ATT_HEADS, 2, HEAD_QK), 1.0),
        'cache_v': nrm((n_pool, DEPTH, PAGE_SIZE, ATT_HEADS, HEAD_V), 1.0),
        'page_table': page_table,
        'state_shift': nrm((DEPTH, DEC_BATCH, RW_PROJ), 1.0),
        'state_wkv': nrm((DEPTH, DEC_BATCH, RW_HEADS, RW_HEAD, RW_HEAD), 0.3),
        'w_ada': nrm((DEPTH, D, 6 * D), 0.3 * D ** -0.5),
        'b_ada': nrm((DEPTH, 6 * D), 0.02),
        'norm1_g': 1.0 + nrm((DEPTH, D), 0.02),
        'norm2_g': 1.0 + nrm((DEPTH, D), 0.02),
        'w_in': nrm((DEPTH, D, IN_W), D ** -0.5),
        'rw_mu': uni((DEPTH, RW_PROJ), 0.0, 1.0),
        'rw_w0': uni((DEPTH, RW_W), -6.0, 1.0),
        'rw_w_up': nrm((DEPTH, DECAY_LORA, RW_W), 0.5 * DECAY_LORA ** -0.5),
        'rw_a0': nrm((DEPTH, RW_W), 0.1),
        'rw_a_up': nrm((DEPTH, AAA_LORA, RW_W), 0.5 * AAA_LORA ** -0.5),
        'rw_g_up': nrm((DEPTH, GATE_LORA, RW_W), GATE_LORA ** -0.5),
        'rw_k_k': 0.85 + nrm((DEPTH, RW_W), 0.02),
        'rw_k_a': 1.0 + nrm((DEPTH, RW_W), 0.02),
        'rw_r_k': nrm((DEPTH, RW_HEADS, RW_HEAD), 0.1),
        'rw_ln_w': 1.0 + nrm((DEPTH, RW_W), 0.02),
        'rw_ln_b': nrm((DEPTH, RW_W), 0.02),
        'w_rw_out': nrm((DEPTH, RW_W, D), RW_W ** -0.5),
        'att_lq1': nrm((DEPTH, HEAD_QK), 0.1),
        'att_lk1': nrm((DEPTH, HEAD_QK), 0.1),
        'att_lq2': nrm((DEPTH, HEAD_QK), 0.1),
        'att_lk2': nrm((DEPTH, HEAD_QK), 0.1),
        'att_subln': 1.0 + nrm((DEPTH, HEAD_V), 0.02),
        'w_att_out': nrm((DEPTH, ATT_V_W, D), ATT_V_W ** -0.5),
        'w_o': nrm((DEPTH, D, D), D ** -0.5),
        'w_router': nrm((D, N_EXPERTS), D ** -0.5),
        'e_bias': nrm((N_EXPERTS,), 0.01),
        'moe_w1': nrm((DEPTH, N_EXPERTS, D, D_EXPERT), D ** -0.5),
        'moe_w3': nrm((DEPTH, N_EXPERTS, D, D_EXPERT), D ** -0.5),
        'moe_w2': nrm((DEPTH, N_EXPERTS, D_EXPERT, D), D_EXPERT ** -0.5),
        'normf_g': 1.0 + nrm((D,), 0.02),
    }


def reference(x_prompt, x_sample, c_prompt, c_sample, cache_k, cache_v, page_table,
              state_shift, state_wkv, w_ada, b_ada, norm1_g, norm2_g, w_in,
              rw_mu, rw_w0, rw_w_up, rw_a0, rw_a_up, rw_g_up, rw_k_k, rw_k_a, rw_r_k,
              rw_ln_w, rw_ln_b, w_rw_out, att_lq1, att_lk1, att_lq2, att_lk2, att_subln,
              w_att_out, w_o, w_router, e_bias, moe_w1, moe_w3, moe_w2, normf_g):
    bp, tp = x_prompt.shape[0], x_prompt.shape[1]
    bs, ts = x_sample.shape[0], x_sample.shape[1]
    past_len = page_table.shape[1] * PAGE_SIZE
    pos_p = jnp.arange(tp)
    pos_s = past_len + jnp.arange(ts)
    zp0 = jnp.zeros((bp, RW_PROJ), x_prompt.dtype)
    sp0 = jnp.zeros((bp, RW_HEADS, RW_HEAD, RW_HEAD), jnp.float32)
    xp, xs = x_prompt, x_sample
    kp_l, vp_l, zp_l, sp_l, ks_l, vs_l, zs_l, ss_l = [], [], [], [], [], [], [], []
    for l in range(DEPTH):
        p = dict(w_ada=w_ada[l], b_ada=b_ada[l], norm1_g=norm1_g[l], norm2_g=norm2_g[l],
                 w_in=w_in[l], rw_mu=rw_mu[l], rw_w0=rw_w0[l], rw_w_up=rw_w_up[l],
                 rw_a0=rw_a0[l], rw_a_up=rw_a_up[l], rw_g_up=rw_g_up[l], rw_k_k=rw_k_k[l],
                 rw_k_a=rw_k_a[l], rw_r_k=rw_r_k[l], rw_ln_w=rw_ln_w[l], rw_ln_b=rw_ln_b[l],
                 w_rw_out=w_rw_out[l], att_lq1=att_lq1[l], att_lk1=att_lk1[l],
                 att_lq2=att_lq2[l], att_lk2=att_lk2[l], att_subln=att_subln[l],
                 w_att_out=w_att_out[l], w_o=w_o[l], w_router=w_router, e_bias=e_bias,
                 moe_w1=moe_w1[l], moe_w3=moe_w3[l], moe_w2=moe_w2[l])
        lam_init = 0.8 - 0.6 * math.exp(-0.3 * l)
        xp, z_last, s_fin, k_rows, v_rows = _layer(xp, c_prompt, pos_p, zp0, sp0, None, None, p, lam_init)
        kp_l.append(k_rows); vp_l.append(v_rows); zp_l.append(z_last); sp_l.append(s_fin)
        k_past = cache_k[page_table, l].reshape(bs, past_len, ATT_HEADS, 2, HEAD_QK)
        v_past = cache_v[page_table, l].reshape(bs, past_len, ATT_HEADS, HEAD_V)
        xs, z_last, s_fin, k_rows, v_rows = _layer(xs, c_sample, pos_s, state_shift[l], state_wkv[l],
                                                   k_past, v_past, p, lam_init)
        ks_l.append(k_rows); vs_l.append(v_rows); zs_l.append(z_last); ss_l.append(s_fin)
    y_prompt = _rmsnorm(xp, normf_g, NORM_EPS)
    y_sample = _rmsnorm(xs, normf_g, NORM_EPS)
    return (y_prompt, y_sample,
            jnp.stack(kp_l, axis=1), jnp.stack(vp_l, axis=1),
            jnp.stack(zp_l, axis=0), jnp.stack(sp_l, axis=0),
            jnp.stack(ks_l, axis=1), jnp.stack(vs_l, axis=1),
            jnp.stack(zs_l, axis=0), jnp.stack(ss_l, axis=0))
```

```python
import functools
import math

import jax
import jax.numpy as jnp
from jax import lax
from jax.experimental import pallas as pl
from jax.experimental.pallas import tpu as pltpu

F32 = jnp.float32
BF16 = jnp.bfloat16

D_MODEL = 1024
RW_HEAD = 64
RW_HEADS = 8
RW_W = RW_HEADS * RW_HEAD
DECAY_LORA = 32
AAA_LORA = 32
GATE_LORA = 96
RW_PROJ = 3 * RW_W + DECAY_LORA + AAA_LORA + GATE_LORA
RW_PROJ_PAD = 1792
LORA_PAD = RW_PROJ_PAD - 3 * RW_W
RW_GN_EPS = 64e-5
ATT_HEADS = 4
HEAD_QK = 64
HEAD_V = 128
ATT_W = 512
ROPE_DIM = HEAD_QK // 4
ROPE_THETA = 500000.0
SUBLN_EPS = 1e-5
NEG_INF = -1e30
N_EXPERTS = 16
N_GROUPS = 4
EPG = 4
D_EXPERT = 512
NORM_EPS = 1e-6
PAGE = 128

COL_GATE = 0
COL_Q = 2048
COL_V = 3072
COL_Z = 3584
IN_W_PAD = COL_Z + RW_PROJ_PAD

VMEM_LIMIT = 56 * 1024 * 1024

_NN = (((1,), (0,)), ((), ()))
_NT = (((1,), (1,)), ((), ()))
_TN = (((0,), (0,)), ((), ()))


def _dot(a, b, dims=_NN):
    return lax.dot_general(a.astype(BF16), b.astype(BF16), dims, preferred_element_type=F32)


def _dot_hi(a, b, dims=_NN):
    return lax.dot_general(a, b, dims, preferred_element_type=F32,
                           precision=lax.Precision.HIGHEST)


def _cparams(sem):
    return pltpu.CompilerParams(dimension_semantics=sem, vmem_limit_bytes=VMEM_LIMIT)


def _rms(x, eps):
    return x * lax.rsqrt(jnp.mean(x * x, axis=-1, keepdims=True) + eps)


def _ada_kernel(c_ref, w_ref, b_ref, o_ref):
    c = c_ref[...]
    sc = c * jax.nn.sigmoid(c)
    o_ref[...] = _dot(sc, w_ref[...]) + b_ref[...]


def _ada_mod(c_all, w_ada, b_ada):
    L, D, N6 = w_ada.shape
    M = c_all.shape[0]
    tn = 1536
    return pl.pallas_call(
        _ada_kernel,
        grid=(L, N6 // tn),
        in_specs=[pl.BlockSpec((M, D), lambda l, j: (0, 0)),
                  pl.BlockSpec((None, D, tn), lambda l, j: (l, 0, j)),
                  pl.BlockSpec((None, 1, tn), lambda l, j: (l, 0, j))],
        out_specs=pl.BlockSpec((None, M, tn), lambda l, j: (l, 0, j)),
        out_shape=jax.ShapeDtypeStruct((L, M, N6), F32),
        compiler_params=_cparams(("arbitrary", "arbitrary")),
    )(c_all, w_ada, b_ada.reshape(L, 1, N6))


def _mod_spec(m, tm, tiles_per_group):
    r = m.shape[1]
    return pl.BlockSpec((None, r, m.shape[2]), lambda i, *_: (i // tiles_per_group, 0, 0))


def _nm_matmul_kernel(x_ref, g_ref, sc_ref, sh_ref, w_ref, o_ref, h_ref):
    @pl.when(pl.program_id(1) == 0)
    def _():
        y = _rms(x_ref[...], NORM_EPS) * g_ref[...]
        h_ref[...] = (y * (1.0 + sc_ref[...]) + sh_ref[...]).astype(BF16)

    o_ref[...] = jnp.dot(h_ref[...], w_ref[...], preferred_element_type=F32)


def _nm_matmul(x, g, sc, sh, w, tm, tn):
    N, D = x.shape
    n_out = w.shape[1]
    tpg = (N // sc.shape[0]) // tm
    return pl.pallas_call(
        _nm_matmul_kernel,
        grid=(N // tm, n_out // tn),
        in_specs=[pl.BlockSpec((tm, D), lambda i, j: (i, 0)),
                  pl.BlockSpec((1, D), lambda i, j: (0, 0)),
                  _mod_spec(sc, tm, tpg), _mod_spec(sh, tm, tpg),
                  pl.BlockSpec((D, tn), lambda i, j: (0, j))],
        out_specs=pl.BlockSpec((tm, tn), lambda i, j: (i, j)),
        out_shape=jax.ShapeDtypeStruct((N, n_out), F32),
        scratch_shapes=[pltpu.VMEM((tm, D), BF16)],
        compiler_params=_cparams(("arbitrary", "arbitrary")),
    )(x, g, sc, sh, w)


def _head_sum(x):
    parts = []
    for h in range(RW_HEADS):
        s = jnp.sum(x[:, h * RW_HEAD:(h + 1) * RW_HEAD], axis=-1, keepdims=True)
        parts.append(jnp.broadcast_to(s, (x.shape[0], RW_HEAD)))
    return jnp.concatenate(parts, axis=-1)


def _rw_prep_kernel(z_ref, z0_ref, mu_ref, vec_ref, wup_ref,
                    r_ref, lw_ref, kk_ref, ka_ref, k2_ref, v_ref, g_ref, bon_ref,
                    carry_ref, *, tm, seq_len):
    i = pl.program_id(0)
    z = z_ref[...]
    rolled = pltpu.roll(z, 1, axis=0)
    row = lax.broadcasted_iota(jnp.int32, z.shape, 0)
    if seq_len >= tm:
        tiles_per_seq = seq_len // tm
        first = jnp.where(i % tiles_per_seq == 0, z0_ref[...], carry_ref[0:1, :])
        z_prev = jnp.where(row == 0, first, rolled)
        carry_ref[0:1, :] = z[tm - 1:tm, :]
    else:
        z_prev = jnp.where(row % seq_len == 0, z0_ref[...], rolled)
    zs = z + (z_prev - z) * mu_ref[...]
    r = zs[:, 0:RW_W]
    k = zs[:, RW_W:2 * RW_W]
    v = zs[:, 2 * RW_W:3 * RW_W]
    tail = zs[:, 3 * RW_W:]
    w0, a0, k_k, k_a, r_k = (vec_ref[j:j + 1, :] for j in range(5))
    dw = _dot(jnp.tanh(tail), wup_ref[0])
    da = _dot(tail, wup_ref[1])
    g = _dot(jax.nn.sigmoid(tail), wup_ref[2])
    t = -(w0 + dw)
    softplus = jnp.maximum(t, 0.0) + jnp.log1p(jnp.exp(-jnp.abs(t)))
    lw = -jnp.exp(-softplus - 0.5)
    a = jax.nn.sigmoid(a0 + da)
    kk = k * k_k
    kk = kk / jnp.maximum(jnp.sqrt(_head_sum(kk * kk)), 1e-12)
    k2 = k * (1.0 + (a - 1.0) * k_a)
    r_ref[...] = r
    lw_ref[...] = lw
    kk_ref[...] = kk
    ka_ref[...] = kk * a
    k2_ref[...] = k2
    v_ref[...] = v
    g_ref[...] = g
    bon_ref[...] = _head_sum(r * k2 * r_k) * v


def _rw_prep(proj, z0, mu, vecs, wup, tm, seq_len):
    N = proj.shape[0]
    kern = functools.partial(_rw_prep_kernel, tm=tm, seq_len=seq_len)
    if seq_len >= tm:
        tps = seq_len // tm
        z0_spec = pl.BlockSpec((None, 1, RW_PROJ_PAD), lambda i: (i // tps, 0, 0))
    else:
        z0_spec = pl.BlockSpec((tm, RW_PROJ_PAD), lambda i: (i, 0))
    o_spec = pl.BlockSpec((tm, RW_W), lambda i: (i, 0))
    return pl.pallas_call(
        kern,
        grid=(N // tm,),
        in_specs=[pl.BlockSpec((tm, RW_PROJ_PAD), lambda i: (i, COL_Z // RW_PROJ_PAD)),
                  z0_spec,
                  pl.BlockSpec((1, RW_PROJ_PAD), lambda i: (0, 0)),
                  pl.BlockSpec((8, RW_W), lambda i: (0, 0)),
                  pl.BlockSpec((3, LORA_PAD, RW_W), lambda i: (0, 0, 0))],
        out_specs=[o_spec] * 8,
        out_shape=[jax.ShapeDtypeStruct((N, RW_W), F32)] * 8,
        scratch_shapes=[pltpu.VMEM((8, RW_PROJ_PAD), F32)],
        compiler_params=_cparams(("arbitrary",)),
    )(proj, z0, mu, vecs, wup)


def _wkv_kernel(r_ref, lw_ref, kk_ref, ka_ref, k2_ref, v_ref, s0_ref, y_ref, sout_ref,
                s_ref, *, tb, C):
    tblk = pl.program_id(1)

    @pl.when(tblk == 0)
    def _():
        s_ref[...] = s0_ref[...]

    ri = lax.broadcasted_iota(jnp.int32, (C, C), 0)
    ci = lax.broadcasted_iota(jnp.int32, (C, C), 1)
    low_incl = ri >= ci
    low_strict = ri > ci
    tri = low_incl.astype(F32)
    eye = (ri == ci).astype(F32)
    n_sq = int(math.log2(C)) - 1

    def chunk(c, carry):
        rows = pl.ds(pl.multiple_of(c * C, C), C)
        r = r_ref[rows, :]
        lw = lw_ref[rows, :]
        kk = kk_ref[rows, :]
        ka = ka_ref[rows, :]
        k2 = k2_ref[rows, :]
        v = v_ref[rows, :]
        cs = _dot_hi(tri, lw)
        tot = cs[C - 1:C, :]
        p_in = jnp.exp(cs)
        p_ex = jnp.exp(cs - lw)
        p_inv = jnp.exp(-cs)
        p_rem = jnp.exp(tot - cs)
        a_t = -kk * p_ex
        r_t = r * p_in
        b_t = ka * p_inv
        k_t = k2 * p_inv
        b_r = ka * p_rem
        k_r = k2 * p_rem
        p_tot = jnp.exp(tot)
        ys = []
        for h in range(RW_HEADS):
            sl = slice(h * RW_HEAD, (h + 1) * RW_HEAD)
            s0 = s_ref[h]
            gram = _dot(jnp.concatenate([a_t[:, sl], r_t[:, sl]], axis=0),
                        jnp.concatenate([b_t[:, sl], k_t[:, sl]], axis=0), _NT)
            mb = jnp.where(low_strict, gram[:C, :C], 0.0)
            mk = jnp.where(low_strict, gram[:C, C:], 0.0)
            nb = jnp.where(low_incl, gram[C:, :C], 0.0)
            nk = jnp.where(low_incl, gram[C:, C:], 0.0)
            tinv = eye + mb
            pw = _dot(mb, mb)
            for _ in range(n_sq - 1):
                both = _dot(pw, jnp.concatenate([tinv, pw], axis=1))
                tinv = tinv + both[:, :C]
                pw = both[:, C:]
            tinv = tinv + _dot(pw, tinv)
            vh = v[:, sl]
            x = _dot(tinv, jnp.concatenate([a_t[:, sl], _dot(mk, vh)], axis=1))
            a_hat = x[:, :RW_HEAD]
            w1 = x[:, RW_HEAD:]
            u = _dot(a_hat, s0, _NT) + w1
            y = _dot(r_t[:, sl], s0, _NT) + _dot(nb, u) + _dot(nk, vh)
            s_new = s0 * p_tot[:, sl] + _dot(jnp.concatenate([u, vh], axis=0),
                                             jnp.concatenate([b_r[:, sl], k_r[:, sl]], axis=0), _TN)
            s_ref[h] = s_new
            ys.append(y)
        y_ref[rows, :] = jnp.concatenate(ys, axis=-1)
        return carry

    lax.fori_loop(0, tb // C, chunk, 0)

    @pl.when(tblk == pl.num_programs(1) - 1)
    def _():
        sout_ref[...] = s_ref[...]


def _wkv(r, lw, kk, ka, k2, v, s0, n_seq, tb, C):
    N = r.shape[0]
    nt = (N // n_seq) // tb
    kern = functools.partial(_wkv_kernel, tb=tb, C=C)
    in_spec = pl.BlockSpec((tb, RW_W), lambda b, t: (b * nt + t, 0))
    s_spec = pl.BlockSpec((None, RW_HEADS, RW_HEAD, RW_HEAD), lambda b, t: (b, 0, 0, 0))
    return pl.pallas_call(
        kern,
        grid=(n_seq, nt),
        in_specs=[in_spec] * 6 + [s_spec],
        out_specs=[in_spec, s_spec],
        out_shape=[jax.ShapeDtypeStruct((N, RW_W), F32),
                   jax.ShapeDtypeStruct((n_seq, RW_HEADS, RW_HEAD, RW_HEAD), F32)],
        scratch_shapes=[pltpu.VMEM((RW_HEADS, RW_HEAD, RW_HEAD), F32)],
        compiler_params=_cparams(("arbitrary", "arbitrary")),
    )(r, lw, kk, ka, k2, v, s0)


def _rope_kernel(qk_ref, cos_ref, s1_ref, s2_ref, q_ref, k_ref):
    cos, s1, s2 = cos_ref[...], s1_ref[...], s2_ref[...]
    half = ROPE_DIM // 2
    for dst, base, scale in ((q_ref, 0, HEAD_QK ** -0.5), (k_ref, ATT_W, 1.0)):
        for cblk in range(ATT_W // 128):
            x = qk_ref[:, base + cblk * 128: base + (cblk + 1) * 128]
            up = pltpu.roll(x, 128 - half, axis=1)
            dn = pltpu.roll(x, half, axis=1)
            y = x * cos + up * s1 + dn * s2
            dst[:, cblk * 128:(cblk + 1) * 128] = y * scale if scale != 1.0 else y


def _rope(proj, tabs, tm):
    N = proj.shape[0]
    ntab = tabs[0].shape[0] // tm
    t_spec = pl.BlockSpec((tm, 128), lambda i: (i % ntab, 0))
    o_spec = pl.BlockSpec((tm, ATT_W), lambda i: (i, 0))
    return pl.pallas_call(
        _rope_kernel,
        grid=(N // tm,),
        in_specs=[pl.BlockSpec((tm, 2 * ATT_W), lambda i: (i, COL_Q // (2 * ATT_W))),
                  t_spec, t_spec, t_spec],
        out_specs=[o_spec, o_spec],
        out_shape=[jax.ShapeDtypeStruct((N, ATT_W), F32)] * 2,
        compiler_params=_cparams(("arbitrary",)),
    )(proj, *tabs)


def _rope_tables(pos):
    half = ROPE_DIM // 2
    inv = ROPE_THETA ** (-jnp.arange(0, ROPE_DIM, 2, dtype=F32) / ROPE_DIM)
    ang = pos.astype(F32)[:, None] * inv[None, :]
    cos, sin = jnp.cos(ang), jnp.sin(ang)
    n = pos.shape[0]
    one = jnp.ones((n, HEAD_QK - ROPE_DIM), F32)
    zero = jnp.zeros((n, HEAD_QK - ROPE_DIM), F32)
    zh = jnp.zeros((n, half), F32)
    c64 = jnp.concatenate([cos, cos, one], axis=1)
    s1 = jnp.concatenate([-sin, zh, zero], axis=1)
    s2 = jnp.concatenate([zh, sin, zero], axis=1)
    return tuple(jnp.concatenate([t, t], axis=1) for t in (c64, s1, s2))


def _lambda(lp_ref, lam_init):
    lp = lp_ref[...]
    d1 = jnp.sum(lp[0:1, :] * lp[1:2, :], axis=-1, keepdims=True)
    d2 = jnp.sum(lp[2:3, :] * lp[3:4, :], axis=-1, keepdims=True)
    return jnp.exp(d1) - jnp.exp(d2) + lam_init


def _flash_kernel(q_ref, k_ref, v_ref, lp_ref, sub_ref, o_ref,
                  m1, l1, a1, m2, l2, a2, *, tq, lam_init):
    i = pl.program_id(2)
    j = pl.program_id(3)

    @pl.when(j == 0)
    def _():
        for m, l, a in ((m1, l1, a1), (m2, l2, a2)):
            m[...] = jnp.full(m.shape, NEG_INF, F32)
            l[...] = jnp.zeros(l.shape, F32)
            a[...] = jnp.zeros(a.shape, F32)

    @pl.when(j <= i)
    def _():
        q = q_ref[...]
        k = k_ref[...].astype(BF16)
        v = v_ref[...].astype(BF16)
        lane = lax.broadcasted_iota(jnp.int32, q.shape, 1)
        rr = lax.broadcasted_iota(jnp.int32, (tq, tq), 0)
        cc = lax.broadcasted_iota(jnp.int32, (tq, tq), 1)
        keep = jnp.logical_or(j < i, cc <= rr)
        for sub, (m, l, a) in enumerate(((m1, l1, a1), (m2, l2, a2))):
            qm = jnp.where((lane >= HEAD_QK) == (sub == 1), q, 0.0).astype(BF16)
            s = lax.dot_general(qm, k, _NT, preferred_element_type=F32)
            s = jnp.where(keep, s, NEG_INF)
            m_new = jnp.maximum(m[...], jnp.max(s, axis=-1, keepdims=True))
            alpha = jnp.exp(m[...] - m_new)
            p = jnp.exp(s - m_new)
            l[...] = alpha * l[...] + jnp.sum(p, axis=-1, keepdims=True)
            a[...] = alpha * a[...] + jnp.dot(p.astype(BF16), v, preferred_element_type=F32)
            m[...] = m_new

    @pl.when(j == i)
    def _():
        lam = _lambda(lp_ref, lam_init)
        o = a1[...] / l1[...] - lam * (a2[...] / l2[...])
        o_ref[...] = _rms(o, SUBLN_EPS) * sub_ref[...] * (1.0 - lam_init)


def _flash(qs, kr, proj, lam_p, subln, n_seq, tq, lam_init):
    N = qs.shape[0]
    nq = (N // n_seq) // tq
    kern = functools.partial(_flash_kernel, tq=tq, lam_init=lam_init)
    vcol = COL_V // HEAD_V
    return pl.pallas_call(
        kern,
        grid=(n_seq, ATT_HEADS, nq, nq),
        in_specs=[pl.BlockSpec((tq, 128), lambda b, h, i, j: (b * nq + i, h)),
                  pl.BlockSpec((tq, 128), lambda b, h, i, j: (b * nq + jnp.minimum(j, i), h)),
                  pl.BlockSpec((tq, HEAD_V), lambda b, h, i, j: (b * nq + jnp.minimum(j, i), vcol + h)),
                  pl.BlockSpec((8, HEAD_QK), lambda b, h, i, j: (0, 0)),
                  pl.BlockSpec((1, HEAD_V), lambda b, h, i, j: (0, 0))],
        out_specs=pl.BlockSpec((tq, HEAD_V), lambda b, h, i, j: (b * nq + i, h)),
        out_shape=jax.ShapeDtypeStruct((N, ATT_W), F32),
        scratch_shapes=[pltpu.VMEM((tq, 1), F32), pltpu.VMEM((tq, 1), F32), pltpu.VMEM((tq, HEAD_V), F32),
                        pltpu.VMEM((tq, 1), F32), pltpu.VMEM((tq, 1), F32), pltpu.VMEM((tq, HEAD_V), F32)],
        compiler_params=_cparams(("arbitrary",) * 4),
    )(qs, kr, proj, lam_p, subln)


def _decode_kernel(pt_ref, q_ref, kc_ref, vc_ref, kn_ref, vn_ref, lp_ref, sub_ref, o_ref,
                   qrow, m_s, l_s, acc, *, ts, n_pages, lam_init):
    p = pl.program_id(1)
    nrow = 2 * ATT_HEADS * ts
    row = lax.broadcasted_iota(jnp.int32, (nrow, ATT_W), 0)
    lane = lax.broadcasted_iota(jnp.int32, (nrow, ATT_W), 1)
    row_head = (row % (ATT_HEADS * ts)) // ts
    row_sub = row // (ATT_HEADS * ts)

    @pl.when(p == 0)
    def _():
        q = q_ref[...]
        qt = jnp.concatenate([q] * (2 * ATT_HEADS), axis=0)
        qrow[...] = jnp.where(lane // HEAD_QK == row_head * 2 + row_sub, qt, 0.0)
        m_s[...] = jnp.full(m_s.shape, NEG_INF, F32)
        l_s[...] = jnp.zeros(l_s.shape, F32)
        acc[...] = jnp.zeros(acc.shape, F32)

    own = p == n_pages
    k = jnp.where(own, kn_ref[...], kc_ref[...]).astype(BF16)
    v = jnp.where(own, vn_ref[...], vc_ref[...]).astype(BF16)
    s = lax.dot_general(qrow[...].astype(BF16), k, _NT, preferred_element_type=F32)
    r2 = lax.broadcasted_iota(jnp.int32, (nrow, PAGE), 0)
    c2 = lax.broadcasted_iota(jnp.int32, (nrow, PAGE), 1)
    keep = jnp.logical_or(jnp.logical_not(own), c2 <= r2 % ts)
    s = jnp.where(keep, s, NEG_INF)
    m_new = jnp.maximum(m_s[...], jnp.max(s, axis=-1, keepdims=True))
    alpha = jnp.exp(m_s[...] - m_new)
    pr = jnp.exp(s - m_new)
    l_s[...] = alpha * l_s[...] + jnp.sum(pr, axis=-1, keepdims=True)
    acc[...] = alpha * acc[...] + jnp.dot(pr.astype(BF16), v, preferred_element_type=F32)
    m_s[...] = m_new

    @pl.when(own)
    def _():
        lam = _lambda(lp_ref, lam_init)
        nh = ATT_HEADS * ts
        on = acc[...] / l_s[...]
        o_all = on[:nh] - lam * on[nh:]
        rh = lax.broadcasted_iota(jnp.int32, (nh, ATT_W), 0) // ts
        lh = lax.broadcasted_iota(jnp.int32, (nh, ATT_W), 1) // HEAD_V
        o_sel = jnp.where(rh == lh, o_all, 0.0)
        o = o_sel[0:ts]
        for h in range(1, ATT_HEADS):
            o = o + o_sel[h * ts:(h + 1) * ts]
        outs = []
        for h in range(ATT_HEADS):
            oh = o[:, h * HEAD_V:(h + 1) * HEAD_V]
            outs.append(_rms(oh, SUBLN_EPS) * sub_ref[...] * (1.0 - lam_init))
        o_ref[...] = jnp.concatenate(outs, axis=-1)


def _decode_attn(qs, kn, vn, cache_k4, cache_v4, page_table, lam_p, subln, layer, lam_init):
    bs, ts, _ = qs.shape
    n_pages = page_table.shape[1]
    nrow = 2 * ATT_HEADS * ts
    kern = functools.partial(_decode_kernel, ts=ts, n_pages=n_pages, lam_init=lam_init)

    def cache_map(b, p, pt):
        return (pt[b, jnp.minimum(p, n_pages - 1)], layer, 0, 0)

    grid_spec = pltpu.PrefetchScalarGridSpec(
        num_scalar_prefetch=1,
        grid=(bs, n_pages + 1),
        in_specs=[pl.BlockSpec((None, ts, ATT_W), lambda b, p, pt: (b, 0, 0)),
                  pl.BlockSpec((None, None, PAGE, ATT_W), cache_map),
                  pl.BlockSpec((None, None, PAGE, ATT_W), cache_map),
                  pl.BlockSpec((None, PAGE, ATT_W), lambda b, p, pt: (b, 0, 0)),
                  pl.BlockSpec((None, PAGE, ATT_W), lambda b, p, pt: (b, 0, 0)),
                  pl.BlockSpec((8, HEAD_QK), lambda b, p, pt: (0, 0)),
                  pl.BlockSpec((1, HEAD_V), lambda b, p, pt: (0, 0))],
        out_specs=pl.BlockSpec((None, ts, ATT_W), lambda b, p, pt: (b, 0, 0)),
        scratch_shapes=[pltpu.VMEM((nrow, ATT_W), F32), pltpu.VMEM((nrow, 1), F32),
                        pltpu.VMEM((nrow, 1), F32), pltpu.VMEM((nrow, ATT_W), F32)],
    )
    return pl.pallas_call(
        kern, grid_spec=grid_spec,
        out_shape=jax.ShapeDtypeStruct((bs, ts, ATT_W), F32),
        compiler_params=_cparams(("arbitrary", "arbitrary")),
    )(page_table, qs, cache_k4, cache_v4, kn, vn, lam_p, subln)


def _merge_kernel(x_ref, y_ref, bon_ref, g_ref, oatt_ref, gates_ref, gt_ref, ln_ref,
                  wrw_ref, watt_ref, wo_ref, o_ref):
    y = y_ref[...]
    n = y.shape[0]
    parts = []
    for h in range(RW_HEADS):
        yh = y[:, h * RW_HEAD:(h + 1) * RW_HEAD]
        mu = jnp.mean(yh, axis=-1, keepdims=True)
        d = yh - mu
        var = jnp.mean(d * d, axis=-1, keepdims=True)
        parts.append(d * lax.rsqrt(var + RW_GN_EPS))
    yn = jnp.concatenate(parts, axis=-1) * ln_ref[0:1, :] + ln_ref[1:2, :]
    out_rw = (yn + bon_ref[...]) * g_ref[...]
    y_rw = _dot(out_rw, wrw_ref[...])
    y_att = _dot(oatt_ref[...], watt_ref[...])
    gates = gates_ref[...]
    merged = (jax.nn.sigmoid(gates[:, :D_MODEL]) * y_rw
              + jax.nn.sigmoid(gates[:, D_MODEL:]) * y_att)
    o_ref[...] = x_ref[...] + gt_ref[...] * _dot(merged, wo_ref[...])


def _merge(x, y, bon, g, oatt, proj, gt, ln, wrw, watt, wo, tm):
    N = x.shape[0]
    tpg = (N // gt.shape[0]) // tm
    s512 = pl.BlockSpec((tm, RW_W), lambda i: (i, 0))
    full = lambda a: pl.BlockSpec(a.shape, lambda i: (0,) * a.ndim)
    return pl.pallas_call(
        _merge_kernel,
        grid=(N // tm,),
        in_specs=[pl.BlockSpec((tm, D_MODEL), lambda i: (i, 0)), s512, s512, s512, s512,
                  pl.BlockSpec((tm, 2 * D_MODEL), lambda i: (i, 0)),
                  _mod_spec(gt, tm, tpg), full(ln), full(wrw), full(watt), full(wo)],
        out_specs=pl.BlockSpec((tm, D_MODEL), lambda i: (i, 0)),
        out_shape=jax.ShapeDtypeStruct((N, D_MODEL), F32),
        compiler_params=_cparams(("arbitrary",)),
    )(x, y, bon, g, oatt, proj, gt, ln, wrw, watt, wo)


def _first_argmax(cols):
    best = cols[0]
    idx = jnp.zeros(best.shape, jnp.int32)
    for n, c in enumerate(cols[1:], start=1):
        take = c > best
        best = jnp.where(take, c, best)
        idx = jnp.where(take, n, idx)
    return best, idx


def _router_kernel(x_ref, g_ref, sc_ref, sh_ref, wr_ref, eb_ref, h_ref, gate_ref):
    y = _rms(x_ref[...], NORM_EPS) * g_ref[...]
    h = y * (1.0 + sc_ref[...]) + sh_ref[...]
    h_ref[...] = h.astype(BF16)
    logits = _dot_hi(h, wr_ref[...])
    s = jax.nn.sigmoid(logits)
    sel = s + eb_ref[...]
    sc_cols = [sel[:, e:e + 1] for e in range(N_EXPERTS)]
    s_cols = [s[:, e:e + 1] for e in range(N_EXPERTS)]
    grp = []
    for gi in range(N_GROUPS):
        cols = sc_cols[gi * EPG:(gi + 1) * EPG]
        m1, i1 = _first_argmax(cols)
        rest = [jnp.where(i1 == n, -jnp.inf, c) for n, c in enumerate(cols)]
        m2, _ = _first_argmax(rest)
        grp.append(m1 + m2)
    _, g_idx = _first_argmax(grp)
    pick = lambda cols_all, n: sum(jnp.where(g_idx == gi, cols_all[gi * EPG + n], 0.0)
                                   for gi in range(N_GROUPS))
    sel_g = [pick(sc_cols, n) for n in range(EPG)]
    s_g = [pick(s_cols, n) for n in range(EPG)]
    _, loc1 = _first_argmax(sel_g)
    _, loc2 = _first_argmax([jnp.where(loc1 == n, -jnp.inf, c) for n, c in enumerate(sel_g)])
    w_1 = sum(jnp.where(loc1 == n, s_g[n], 0.0) for n in range(EPG))
    w_2 = sum(jnp.where(loc2 == n, s_g[n], 0.0) for n in range(EPG))
    tot = w_1 + w_2
    w_1, w_2 = w_1 / tot, w_2 / tot
    e1 = g_idx * EPG + loc1
    e2 = g_idx * EPG + loc2
    lane = lax.broadcasted_iota(jnp.int32, logits.shape, 1)
    gate_ref[...] = jnp.where(lane == e1, w_1, 0.0) + jnp.where(lane == e2, w_2, 0.0)


def _router(x, g, sc, sh, wr, eb, tm):
    N, D = x.shape
    tpg = (N // sc.shape[0]) // tm
    return pl.pallas_call(
        _router_kernel,
        grid=(N // tm,),
        in_specs=[pl.BlockSpec((tm, D), lambda i: (i, 0)),
                  pl.BlockSpec((1, D), lambda i: (0, 0)),
                  _mod_spec(sc, tm, tpg), _mod_spec(sh, tm, tpg),
                  pl.BlockSpec((D, 128), lambda i: (0, 0)),
                  pl.BlockSpec((1, 128), lambda i: (0, 0))],
        out_specs=[pl.BlockSpec((tm, D), lambda i: (i, 0)),
                   pl.BlockSpec((tm, 128), lambda i: (i, 0))],
        out_shape=[jax.ShapeDtypeStruct((N, D), BF16), jax.ShapeDtypeStruct((N, 128), F32)],
        compiler_params=_cparams(("arbitrary",)),
    )(x, g, sc, sh, wr, eb)


def _moe_kernel(x_ref, h_ref, gate_ref, gt_ref, w1_ref, w3_ref, w2_ref, o_ref, acc_ref):
    e = pl.program_id(1)

    @pl.when(e == 0)
    def _():
        acc_ref[...] = jnp.zeros(acc_ref.shape, F32)

    h = h_ref[...]
    a1 = jnp.dot(h, w1_ref[...], preferred_element_type=F32)
    a3 = jnp.dot(h, w3_ref[...], preferred_element_type=F32)
    lane = lax.broadcasted_iota(jnp.int32, gate_ref.shape, 1)
    gcol = jnp.sum(jnp.where(lane == e, gate_ref[...], 0.0), axis=-1, keepdims=True)
    act = (a1 * jax.nn.sigmoid(a1)) * a3 * gcol
    acc_ref[...] += _dot(act, w2_ref[...])

    @pl.when(e == pl.num_programs(1) - 1)
    def _():
        o_ref[...] = x_ref[...] + gt_ref[...] * acc_ref[...]


def _moe(x, h, gate, gt, w1, w3, w2, tm):
    N, D = x.shape
    tpg = (N // gt.shape[0]) // tm
    return pl.pallas_call(
        _moe_kernel,
        grid=(N // tm, N_EXPERTS),
        in_specs=[pl.BlockSpec((tm, D), lambda i, e: (i, 0)),
                  pl.BlockSpec((tm, D), lambda i, e: (i, 0)),
                  pl.BlockSpec((tm, 128), lambda i, e: (i, 0)),
                  _mod_spec(gt, tm, tpg),
                  pl.BlockSpec((None, D, D_EXPERT), lambda i, e: (e, 0, 0)),
                  pl.BlockSpec((None, D, D_EXPERT), lambda i, e: (e, 0, 0)),
                  pl.BlockSpec((None, D_EXPERT, D), lambda i, e: (e, 0, 0))],
        out_specs=pl.BlockSpec((tm, D), lambda i, e: (i, 0)),
        out_shape=jax.ShapeDtypeStruct((N, D), F32),
        scratch_shapes=[pltpu.VMEM((tm, D), F32)],
        compiler_params=_cparams(("arbitrary", "arbitrary")),
    )(x, h, gate, gt, w1, w3, w2)


def _final_norm_kernel(x_ref, g_ref, o_ref):
    o_ref[...] = _rms(x_ref[...], NORM_EPS) * g_ref[...]


def _final_norm(x, g, tm):
    N, D = x.shape
    return pl.pallas_call(
        _final_norm_kernel,
        grid=(N // tm,),
        in_specs=[pl.BlockSpec((tm, D), lambda i: (i, 0)), pl.BlockSpec((1, D), lambda i: (0, 0))],
        out_specs=pl.BlockSpec((tm, D), lambda i: (i, 0)),
        out_shape=jax.ShapeDtypeStruct((N, D), F32),
        compiler_params=_cparams(("arbitrary",)),
    )(x, g)


def _group_layer(x, mods, lw, cfg, z0, s0, attend):
    sh1, sc1, gt1, sh2, sc2, gt2 = mods
    tm = cfg["tm"]
    proj = _nm_matmul(x, lw["norm1_g"], sc1, sh1, lw["w_in"], tm, cfg["tn_in"])
    r, lwd, kk, ka, k2, v, g, bon = _rw_prep(proj, z0, lw["mu"], lw["rw_vecs"], lw["wup"],
                                               cfg["tm_prep"], cfg["seq"])
    y, s_fin = _wkv(r, lwd, kk, ka, k2, v, s0, cfg["n_seq"], cfg["tb"], cfg["chunk"])
    qs, kr = _rope(proj, cfg["rope_tabs"], cfg["tm_prep"])
    oatt = attend(qs, kr, proj)
    x = _merge(x, y, bon, g, oatt, proj, gt1, lw["ln"], lw["w_rw_out"], lw["w_att_out"],
               lw["w_o"], tm)
    h2, gate = _router(x, lw["norm2_g"], sc2, sh2, lw["w_router"], lw["e_bias"], tm)
    x = _moe(x, h2, gate, gt2, lw["moe_w1"], lw["moe_w3"], lw["moe_w2"], cfg["tm_moe"])
    return x, proj, kr, s_fin


def kernel(x_prompt, x_sample, c_prompt, c_sample, cache_k, cache_v, page_table, state_shift, state_wkv, w_ada, b_ada, norm1_g, norm2_g, w_in, rw_mu, rw_w0, rw_w_up, rw_a0, rw_a_up, rw_g_up, rw_k_k, rw_k_a, rw_r_k, rw_ln_w, rw_ln_b, w_rw_out, att_lq1, att_lk1, att_lq2, att_lk2, att_subln, w_att_out, w_o, w_router, e_bias, moe_w1, moe_w3, moe_w2, normf_g):
    bp, tp, D = x_prompt.shape
    bs, ts, _ = x_sample.shape
    depth = w_in.shape[0]
    n_pages = page_table.shape[1]
    past_len = n_pages * PAGE
    n_p, n_s = bp * tp, bs * ts

    z_w, q_w, k_w, v_w, grw_w, gatt_w = jnp.split(
        w_in, [RW_PROJ, RW_PROJ + ATT_W, RW_PROJ + 2 * ATT_W, RW_PROJ + 3 * ATT_W,
               RW_PROJ + 3 * ATT_W + D], axis=-1)
    w_in_p = jnp.concatenate(
        [grw_w, gatt_w, q_w, k_w, v_w, z_w,
         jnp.zeros((depth, D, RW_PROJ_PAD - RW_PROJ), F32)], axis=-1).astype(BF16)
    mu_p = jnp.pad(rw_mu, ((0, 0), (0, RW_PROJ_PAD - RW_PROJ)))
    wup = jnp.zeros((depth, 3, LORA_PAD, RW_W), F32)
    wup = wup.at[:, 0, 0:DECAY_LORA].set(rw_w_up)
    wup = wup.at[:, 1, DECAY_LORA:DECAY_LORA + AAA_LORA].set(rw_a_up)
    wup = wup.at[:, 2, DECAY_LORA + AAA_LORA:DECAY_LORA + AAA_LORA + GATE_LORA].set(rw_g_up)
    wup = wup.astype(BF16)
    rw_vecs = jnp.stack([rw_w0, rw_a0, rw_k_k, rw_k_a, rw_r_k.reshape(depth, RW_W),
                         jnp.zeros_like(rw_w0), jnp.zeros_like(rw_w0), jnp.zeros_like(rw_w0)], axis=1)
    ln = jnp.stack([rw_ln_w, rw_ln_b], axis=1)
    lam_p = jnp.stack([att_lq1, att_lk1, att_lq2, att_lk2] + [jnp.zeros_like(att_lq1)] * 4, axis=1)
    w_router_p = jnp.pad(w_router, ((0, 0), (0, 128 - N_EXPERTS)))
    e_bias_p = jnp.pad(e_bias, (0, 128 - N_EXPERTS)).reshape(1, 128)
    w_rw_out_b, w_att_out_b, w_o_b = (w.astype(BF16) for w in (w_rw_out, w_att_out, w_o))
    moe_w1_b, moe_w3_b, moe_w2_b = (w.astype(BF16) for w in (moe_w1, moe_w3, moe_w2))

    n_c = bp + bs
    n_cp = -(-n_c // 8) * 8
    c_all = jnp.pad(jnp.concatenate([c_prompt, c_sample], axis=0), ((0, n_cp - n_c), (0, 0)))
    mod = _ada_mod(c_all, w_ada, b_ada)

    tm_p = min(512, tp)
    cfg_p = dict(tm=tm_p, tn_in=768, tm_prep=min(256, tp), seq=tp, n_seq=bp, tb=min(256, tp),
                 chunk=32, tm_moe=min(1024, tp), rope_tabs=_rope_tables(jnp.arange(tp)))
    cfg_s = dict(tm=n_s, tn_in=768, tm_prep=n_s, seq=ts, n_seq=bs, tb=8, chunk=8, tm_moe=n_s,
                 rope_tabs=_rope_tables(jnp.tile(past_len + jnp.arange(ts), bs)))

    cache_k4 = cache_k.reshape(cache_k.shape[0], depth, PAGE, ATT_W)
    cache_v4 = cache_v.reshape(cache_v.shape[0], depth, PAGE, ATT_W)
    zp0 = jnp.zeros((bp, 1, RW_PROJ_PAD), F32)
    sp0 = jnp.zeros((bp, RW_HEADS, RW_HEAD, RW_HEAD), F32)

    xp = x_prompt.reshape(n_p, D)
    xs = x_sample.reshape(n_s, D)
    outs = {k: [] for k in ("kp", "vp", "zp", "sp", "ks", "vs", "zs", "ss")}
    for l in range(depth):
        lam_init = 0.8 - 0.6 * math.exp(-0.3 * l)
        lw = dict(norm1_g=norm1_g[l].reshape(1, D), norm2_g=norm2_g[l].reshape(1, D),
                  w_in=w_in_p[l], mu=mu_p[l].reshape(1, RW_PROJ_PAD), rw_vecs=rw_vecs[l],
                  wup=wup[l], ln=ln[l], w_rw_out=w_rw_out_b[l], w_att_out=w_att_out_b[l],
                  w_o=w_o_b[l], w_router=w_router_p, e_bias=e_bias_p,
                  moe_w1=moe_w1_b[l], moe_w3=moe_w3_b[l], moe_w2=moe_w2_b[l])
        subln = att_subln[l].reshape(1, HEAD_V)
        mods_p = [m.reshape(bp, 1, D) for m in jnp.split(mod[l, :bp], 6, axis=-1)]
        mods_s = [jnp.repeat(m, ts, axis=0).reshape(1, n_s, D)
                  for m in jnp.split(mod[l, bp:n_c], 6, axis=-1)]

        attend_p = lambda qs, kr, proj: _flash(qs, kr, proj, lam_p[l], subln, bp,
                                               min(512, tp), lam_init)
        xp, proj_p, kr_p, s_fin = _group_layer(xp, mods_p, lw, cfg_p, zp0, sp0, attend_p)
        outs["kp"].append(kr_p.reshape(bp, tp, ATT_HEADS, 2, HEAD_QK))
        outs["vp"].append(proj_p[:, COL_V:COL_V + ATT_W].reshape(bp, tp, ATT_HEADS, HEAD_V))
        outs["zp"].append(proj_p.reshape(bp, tp, IN_W_PAD)[:, -1, COL_Z:COL_Z + RW_PROJ])
        outs["sp"].append(s_fin)

        def attend_s(qs, kr, proj):
            pad = lambda a: jnp.pad(a.reshape(bs, ts, ATT_W), ((0, 0), (0, PAGE - ts), (0, 0)))
            return _decode_attn(qs.reshape(bs, ts, ATT_W), pad(kr), pad(proj[:, COL_V:COL_V + ATT_W]),
                                cache_k4, cache_v4, page_table, lam_p[l], subln, l,
                                lam_init).reshape(n_s, ATT_W)

        z0_s = jnp.pad(jnp.repeat(state_shift[l], ts, axis=0), ((0, 0), (0, RW_PROJ_PAD - RW_PROJ)))
        xs, proj_s, kr_s, s_fin = _sample_layer(xs, mods_s, lw, cfg_s, z0_s, state_wkv[l], attend_s, bs, ts)
        outs["ks"].append(kr_s.reshape(bs, ts, ATT_HEADS, 2, HEAD_QK))
        outs["vs"].append(proj_s[:, COL_V:COL_V + ATT_W].reshape(bs, ts, ATT_HEADS, HEAD_V))
        outs["zs"].append(proj_s.reshape(bs, ts, IN_W_PAD)[:, -1, COL_Z:COL_Z + RW_PROJ])
        outs["ss"].append(s_fin)

    y_prompt = _final_norm(xp, normf_g.reshape(1, D), tm_p).reshape(bp, tp, D)
    y_sample = _final_norm(xs, normf_g.reshape(1, D), n_s).reshape(bs, ts, D)
    return (y_prompt, y_sample,
            jnp.stack(outs["kp"], axis=1), jnp.stack(outs["vp"], axis=1),
            jnp.stack(outs["zp"], axis=0), jnp.stack(outs["sp"], axis=0),
            jnp.stack(outs["ks"], axis=1), jnp.stack(outs["vs"], axis=1),
            jnp.stack(outs["zs"], axis=0), jnp.stack(outs["ss"], axis=0))


def _sample_layer(x, mods, lw, cfg, z0, s0, attend, bs, ts):
    sh1, sc1, gt1, sh2, sc2, gt2 = mods
    tm = cfg["tm"]
    proj = _nm_matmul(x, lw["norm1_g"], sc1, sh1, lw["w_in"], tm, cfg["tn_in"])
    prep = _rw_prep(proj, z0, lw["mu"], lw["rw_vecs"], lw["wup"], cfg["tm_prep"], cfg["seq"])
    r, lwd, kk, ka, k2, v, g, bon = prep
    pad8 = lambda a: jnp.pad(a.reshape(bs, ts, RW_W), ((0, 0), (0, 8 - ts), (0, 0))).reshape(bs * 8, RW_W)
    y8, s_fin = _wkv(*(pad8(a) for a in (r, lwd, kk, ka, k2, v)), s0, bs, 8, 8)
    y = y8.reshape(bs, 8, RW_W)[:, :ts].reshape(bs * ts, RW_W)
    qs, kr = _rope(proj, cfg["rope_tabs"], cfg["tm_prep"])
    oatt = attend(qs, kr, proj)
    x = _merge(x, y, bon, g, oatt, proj, gt1, lw["ln"], lw["w_rw_out"], lw["w_att_out"],
               lw["w_o"], tm)
    h2, gate = _router(x, lw["norm2_g"], sc2, sh2, lw["w_router"], lw["e_bias"], tm)
    x = _moe(x, h2, gate, gt2, lw["moe_w1"], lw["moe_w3"], lw["moe_w2"], cfg["tm_moe"])
    return x, proj, kr, s_fin
```

```python
import functools
import math

import jax
import jax.numpy as jnp
from jax import lax
from jax.experimental import pallas as pl
from jax.experimental.pallas import tpu as pltpu

F32 = jnp.float32
BF16 = jnp.bfloat16

D_MODEL = 1024
RW_HEAD = 64
RW_HEADS = 8
RW_W = RW_HEADS * RW_HEAD
DECAY_LORA = 32
AAA_LORA = 32
GATE_LORA = 96
RW_PROJ = 3 * RW_W + DECAY_LORA + AAA_LORA + GATE_LORA
RW_PROJ_PAD = 1792
LORA_PAD = RW_PROJ_PAD - 3 * RW_W
RW_GN_EPS = 64e-5
ATT_HEADS = 4
HEAD_QK = 64
HEAD_V = 128
ATT_W = 512
ROPE_DIM = HEAD_QK // 4
ROPE_THETA = 500000.0
SUBLN_EPS = 1e-5
NEG_INF = -1e30
N_EXPERTS = 16
N_GROUPS = 4
EPG = 4
D_EXPERT = 512
NORM_EPS = 1e-6
PAGE = 128

COL_GATE = 0
COL_Q = 2048
COL_V = 3072
COL_Z = 3584
IN_W_PAD = COL_Z + RW_PROJ_PAD

VMEM_LIMIT = 56 * 1024 * 1024

_NN = (((1,), (0,)), ((), ()))
_NT = (((1,), (1,)), ((), ()))
_TN = (((0,), (0,)), ((), ()))


def _dot(a, b, dims=_NN):
    return lax.dot_general(a.astype(BF16), b.astype(BF16), dims, preferred_element_type=F32)


def _dot_hi(a, b, dims=_NN):
    return lax.dot_general(a, b, dims, preferred_element_type=F32,
                           precision=lax.Precision.HIGHEST)


def _cparams(sem):
    return pltpu.CompilerParams(dimension_semantics=sem, vmem_limit_bytes=VMEM_LIMIT)


def _rms(x, eps):
    return x * lax.rsqrt(jnp.mean(x * x, axis=-1, keepdims=True) + eps)


def _ada_kernel(c_ref, w_ref, b_ref, o_ref):
    c = c_ref[...]
    sc = c * jax.nn.sigmoid(c)
    o_ref[...] = _dot(sc, w_ref[...]) + b_ref[...]


def _ada_mod(c_all, w_ada, b_ada):
    L, D, N6 = w_ada.shape
    M = c_all.shape[0]
    tn = 1536
    return pl.pallas_call(
        _ada_kernel,
        grid=(L, N6 // tn),
        in_specs=[pl.BlockSpec((M, D), lambda l, j: (0, 0)),
                  pl.BlockSpec((None, D, tn), lambda l, j: (l, 0, j)),
                  pl.BlockSpec((None, 1, tn), lambda l, j: (l, 0, j))],
        out_specs=pl.BlockSpec((None, M, tn), lambda l, j: (l, 0, j)),
        out_shape=jax.ShapeDtypeStruct((L, M, N6), F32),
        compiler_params=_cparams(("arbitrary", "arbitrary")),
    )(c_all, w_ada, b_ada.reshape(L, 1, N6))


def _mod_spec(m, tm, tiles_per_group):
    r = m.shape[1]
    return pl.BlockSpec((None, r, m.shape[2]), lambda i, *_: (i // tiles_per_group, 0, 0))


def _nm_matmul_kernel(x_ref, g_ref, sc_ref, sh_ref, w_ref, o_ref, h_ref):
    @pl.when(pl.program_id(1) == 0)
    def _():
        y = _rms(x_ref[...], NORM_EPS) * g_ref[...]
        h_ref[...] = (y * (1.0 + sc_ref[...]) + sh_ref[...]).astype(BF16)

    o_ref[...] = jnp.dot(h_ref[...], w_ref[...], preferred_element_type=F32)


def _nm_matmul(x, g, sc, sh, w, tm, tn):
    N, D = x.shape
    n_out = w.shape[1]
    tpg = (N // sc.shape[0]) // tm
    return pl.pallas_call(
        _nm_matmul_kernel,
        grid=(N // tm, n_out // tn),
        in_specs=[pl.BlockSpec((tm, D), lambda i, j: (i, 0)),
                  pl.BlockSpec((1, D), lambda i, j: (0, 0)),
                  _mod_spec(sc, tm, tpg), _mod_spec(sh, tm, tpg),
                  pl.BlockSpec((D, tn), lambda i, j: (0, j))],
        out_specs=pl.BlockSpec((tm, tn), lambda i, j: (i, j)),
        out_shape=jax.ShapeDtypeStruct((N, n_out), F32),
        scratch_shapes=[pltpu.VMEM((tm, D), BF16)],
        compiler_params=_cparams(("arbitrary", "arbitrary")),
    )(x, g, sc, sh, w)


def _head_sum(x):
    parts = []
    for h in range(RW_HEADS):
        s = jnp.sum(x[:, h * RW_HEAD:(h + 1) * RW_HEAD], axis=-1, keepdims=True)
        parts.append(jnp.broadcast_to(s, (x.shape[0], RW_HEAD)))
    return jnp.concatenate(parts, axis=-1)


def _rw_prep_kernel(z_ref, z0_ref, mu_ref, vec_ref, wup_ref,
                    r_ref, lw_ref, kk_ref, ka_ref, k2_ref, v_ref, g_ref, bon_ref,
                    carry_ref, *, tm, seq_len):
    i = pl.program_id(0)
    z = z_ref[...]
    rolled = pltpu.roll(z, 1, axis=0)
    row = lax.broadcasted_iota(jnp.int32, z.shape, 0)
    if seq_len >= tm:
        tiles_per_seq = seq_len // tm
        first = jnp.where(i % tiles_per_seq == 0, z0_ref[...], carry_ref[0:1, :])
        z_prev = jnp.where(row == 0, first, rolled)
        carry_ref[0:1, :] = z[tm - 1:tm, :]
    else:
        z_prev = jnp.where(row % seq_len == 0, z0_ref[...], rolled)
    zs = z + (z_prev - z) * mu_ref[...]
    r = zs[:, 0:RW_W]
    k = zs[:, RW_W:2 * RW_W]
    v = zs[:, 2 * RW_W:3 * RW_W]
    tail = zs[:, 3 * RW_W:]
    w0, a0, k_k, k_a, r_k = (vec_ref[j:j + 1, :] for j in range(5))
    dw = _dot(jnp.tanh(tail), wup_ref[0])
    da = _dot(tail, wup_ref[1])
    g = _dot(jax.nn.sigmoid(tail), wup_ref[2])
    t = -(w0 + dw)
    softplus = jnp.maximum(t, 0.0) + jnp.log1p(jnp.exp(-jnp.abs(t)))
    lw = -jnp.exp(-softplus - 0.5)
    a = jax.nn.sigmoid(a0 + da)
    kk = k * k_k
    kk = kk / jnp.maximum(jnp.sqrt(_head_sum(kk * kk)), 1e-12)
    k2 = k * (1.0 + (a - 1.0) * k_a)
    r_ref[...] = r
    lw_ref[...] = lw
    kk_ref[...] = kk
    ka_ref[...] = kk * a
    k2_ref[...] = k2
    v_ref[...] = v
    g_ref[...] = g
    bon_ref[...] = _head_sum(r * k2 * r_k) * v


def _rw_prep(proj, z0, mu, vecs, wup, tm, seq_len):
    N = proj.shape[0]
    kern = functools.partial(_rw_prep_kernel, tm=tm, seq_len=seq_len)
    if seq_len >= tm:
        tps = seq_len // tm
        z0_spec = pl.BlockSpec((None, 1, RW_PROJ_PAD), lambda i: (i // tps, 0, 0))
    else:
        z0_spec = pl.BlockSpec((tm, RW_PROJ_PAD), lambda i: (i, 0))
    o_spec = pl.BlockSpec((tm, RW_W), lambda i: (i, 0))
    return pl.pallas_call(
        kern,
        grid=(N // tm,),
        in_specs=[pl.BlockSpec((tm, RW_PROJ_PAD), lambda i: (i, COL_Z // RW_PROJ_PAD)),
                  z0_spec,
                  pl.BlockSpec((1, RW_PROJ_PAD), lambda i: (0, 0)),
                  pl.BlockSpec((8, RW_W), lambda i: (0, 0)),
                  pl.BlockSpec((3, LORA_PAD, RW_W), lambda i: (0, 0, 0))],
        out_specs=[o_spec] * 8,
        out_shape=[jax.ShapeDtypeStruct((N, RW_W), F32)] * 8,
        scratch_shapes=[pltpu.VMEM((8, RW_PROJ_PAD), F32)],
        compiler_params=_cparams(("arbitrary",)),
    )(proj, z0, mu, vecs, wup)


def _wkv_kernel(r_ref, lw_ref, kk_ref, ka_ref, k2_ref, v_ref, s0_ref, y_ref, sout_ref,
                s_ref, *, tb, C):
    tblk = pl.program_id(1)
    n_pair = RW_HEADS // 2
    zero64 = jnp.zeros((RW_HEAD, RW_HEAD), F32)

    @pl.when(tblk == 0)
    def _():
        for p in range(n_pair):
            s_ref[p] = jnp.concatenate(
                [jnp.concatenate([s0_ref[2 * p], zero64], axis=1),
                 jnp.concatenate([zero64, s0_ref[2 * p + 1]], axis=1)], axis=0)

    C2 = 2 * C
    ri = lax.broadcasted_iota(jnp.int32, (C, C), 0)
    ci = lax.broadcasted_iota(jnp.int32, (C, C), 1)
    tri = (ri >= ci).astype(F32)
    r2 = lax.broadcasted_iota(jnp.int32, (C2, C2), 0)
    c2 = lax.broadcasted_iota(jnp.int32, (C2, C2), 1)
    low_incl = r2 >= c2
    low_strict = r2 > c2
    eye = (r2 == c2).astype(F32)
    lo_lanes = lax.broadcasted_iota(jnp.int32, (C, 128), 1) < RW_HEAD
    n_sq = int(math.log2(C)) - 1
    pairs = range(n_pair)

    def stack(x, p):
        xp = x[:, p * 128:(p + 1) * 128]
        return jnp.concatenate([jnp.where(lo_lanes, xp, 0.0), jnp.where(lo_lanes, 0.0, xp)], axis=0)

    def chunk(c, carry):
        rows = pl.ds(pl.multiple_of(c * C, C), C)
        r = r_ref[rows, :]
        lw = lw_ref[rows, :]
        kk = kk_ref[rows, :]
        ka = ka_ref[rows, :]
        k2 = k2_ref[rows, :]
        v = v_ref[rows, :]
        cs = _dot_hi(tri, lw)
        tot = cs[C - 1:C, :]
        p_inv = jnp.exp(-cs)
        p_rem = jnp.exp(tot - cs)
        p_tot = jnp.exp(tot)
        a_s = [stack(-kk * jnp.exp(cs - lw), p).astype(BF16) for p in pairs]
        r_s = [stack(r * jnp.exp(cs), p).astype(BF16) for p in pairs]
        b_s = [stack(ka * p_inv, p).astype(BF16) for p in pairs]
        k_s = [stack(k2 * p_inv, p).astype(BF16) for p in pairs]
        bk_r = [jnp.concatenate([stack(ka * p_rem, p), stack(k2 * p_rem, p)], axis=0).astype(BF16)
                for p in pairs]
        v_s = [stack(v, p).astype(BF16) for p in pairs]
        ar = [jnp.concatenate([a_s[p], r_s[p]], axis=0) for p in pairs]
        gb = [_dot(ar[p], b_s[p], _NT) for p in pairs]
        gk = [_dot(ar[p], k_s[p], _NT) for p in pairs]
        mb = [jnp.where(low_strict, gb[p][:C2], 0.0) for p in pairs]
        nb = [jnp.where(low_incl, gb[p][C2:], 0.0) for p in pairs]
        mk = [jnp.where(low_strict, gk[p][:C2], 0.0) for p in pairs]
        nk = [jnp.where(low_incl, gk[p][C2:], 0.0) for p in pairs]
        tinv = [eye + mb[p] for p in pairs]
        pw = [_dot(mb[p], mb[p]) for p in pairs]
        for lvl in range(n_sq):
            if lvl < n_sq - 1:
                both = [_dot(jnp.concatenate([tinv[p], pw[p]], axis=0), pw[p]) for p in pairs]
                tinv = [tinv[p] + both[p][:C2] for p in pairs]
                pw = [both[p][C2:] for p in pairs]
            else:
                tinv = [tinv[p] + _dot(tinv[p], pw[p]) for p in pairs]
        mkv = [_dot(mk[p], v_s[p]) for p in pairs]
        x = [_dot(tinv[p], jnp.concatenate([a_s[p], mkv[p].astype(BF16)], axis=1)) for p in pairs]
        s_old = [s_ref[p] for p in pairs]
        s_bf = [s.astype(BF16) for s in s_old]
        u = [_dot(x[p][:, :128], s_bf[p], _NT) + x[p][:, 128:] for p in pairs]
        y = [_dot(r_s[p], s_bf[p], _NT) + _dot(nb[p], u[p]) + _dot(nk[p], v_s[p]) for p in pairs]
        for p in pairs:
            uv = jnp.concatenate([u[p].astype(BF16), v_s[p]], axis=0)
            s_ref[p] = s_old[p] * p_tot[:, p * 128:(p + 1) * 128] + _dot(uv, bk_r[p], _TN)
            y_ref[rows, p * 128:(p + 1) * 128] = y[p][:C] + y[p][C:]
        return carry

    lax.fori_loop(0, tb // C, chunk, 0)

    @pl.when(tblk == pl.num_programs(1) - 1)
    def _():
        for p in range(n_pair):
            s = s_ref[p]
            sout_ref[2 * p] = s[:RW_HEAD, :RW_HEAD]
            sout_ref[2 * p + 1] = s[RW_HEAD:, RW_HEAD:]


def _wkv(r, lw, kk, ka, k2, v, s0, n_seq, tb, C):
    N = r.shape[0]
    nt = (N // n_seq) // tb
    kern = functools.partial(_wkv_kernel, tb=tb, C=C)
    in_spec = pl.BlockSpec((tb, RW_W), lambda b, t: (b * nt + t, 0))
    s_spec = pl.BlockSpec((None, RW_HEADS, RW_HEAD, RW_HEAD), lambda b, t: (b, 0, 0, 0))
    return pl.pallas_call(
        kern,
        grid=(n_seq, nt),
        in_specs=[in_spec] * 6 + [s_spec],
        out_specs=[in_spec, s_spec],
        out_shape=[jax.ShapeDtypeStruct((N, RW_W), F32),
                   jax.ShapeDtypeStruct((n_seq, RW_HEADS, RW_HEAD, RW_HEAD), F32)],
        scratch_shapes=[pltpu.VMEM((RW_HEADS // 2, 128, 128), F32)],
        compiler_params=_cparams(("arbitrary", "arbitrary")),
    )(r, lw, kk, ka, k2, v, s0)


def _rope_kernel(qk_ref, cos_ref, s1_ref, s2_ref, q_ref, k_ref):
    cos, s1, s2 = cos_ref[...], s1_ref[...], s2_ref[...]
    half = ROPE_DIM // 2
    for dst, base, scale in ((q_ref, 0, HEAD_QK ** -0.5), (k_ref, ATT_W, 1.0)):
        for cblk in range(ATT_W // 128):
            x = qk_ref[:, base + cblk * 128: base + (cblk + 1) * 128]
            up = pltpu.roll(x, 128 - half, axis=1)
            dn = pltpu.roll(x, half, axis=1)
            y = x * cos + up * s1 + dn * s2
            dst[:, cblk * 128:(cblk + 1) * 128] = y * scale if scale != 1.0 else y


def _rope(proj, tabs, tm):
    N = proj.shape[0]
    ntab = tabs[0].shape[0] // tm
    t_spec = pl.BlockSpec((tm, 128), lambda i: (i % ntab, 0))
    o_spec = pl.BlockSpec((tm, ATT_W), lambda i: (i, 0))
    return pl.pallas_call(
        _rope_kernel,
        grid=(N // tm,),
        in_specs=[pl.BlockSpec((tm, 2 * ATT_W), lambda i: (i, COL_Q // (2 * ATT_W))),
                  t_spec, t_spec, t_spec],
        out_specs=[o_spec, o_spec],
        out_shape=[jax.ShapeDtypeStruct((N, ATT_W), F32)] * 2,
        compiler_params=_cparams(("arbitrary",)),
    )(proj, *tabs)


def _rope_tables(pos):
    half = ROPE_DIM // 2
    inv = ROPE_THETA ** (-jnp.arange(0, ROPE_DIM, 2, dtype=F32) / ROPE_DIM)
    ang = pos.astype(F32)[:, None] * inv[None, :]
    cos, sin = jnp.cos(ang), jnp.sin(ang)
    n = pos.shape[0]
    one = jnp.ones((n, HEAD_QK - ROPE_DIM), F32)
    zero = jnp.zeros((n, HEAD_QK - ROPE_DIM), F32)
    zh = jnp.zeros((n, half), F32)
    c64 = jnp.concatenate([cos, cos, one], axis=1)
    s1 = jnp.concatenate([-sin, zh, zero], axis=1)
    s2 = jnp.concatenate([zh, sin, zero], axis=1)
    return tuple(jnp.concatenate([t, t], axis=1) for t in (c64, s1, s2))


def _lambda(lp_ref, lam_init):
    lp = lp_ref[...]
    d1 = jnp.sum(lp[0:1, :] * lp[1:2, :], axis=-1, keepdims=True)
    d2 = jnp.sum(lp[2:3, :] * lp[3:4, :], axis=-1, keepdims=True)
    return jnp.exp(d1) - jnp.exp(d2) + lam_init


def _flash_kernel(qi_ref, kj_ref, q_ref, k_ref, v_ref, lp_ref, sub_ref, o_ref,
                  qs_ref, m_ref, l_ref, acc_ref, *, tq, lam_init):
    t = pl.program_id(2)
    i = qi_ref[t]
    j = kj_ref[t]

    @pl.when(j == 0)
    def _():
        q = q_ref[...]
        lo = lax.broadcasted_iota(jnp.int32, q.shape, 1) < HEAD_QK
        qs_ref[...] = jnp.concatenate([jnp.where(lo, q, 0.0), jnp.where(lo, 0.0, q)],
                                      axis=0).astype(BF16)
        m_ref[...] = jnp.full(m_ref.shape, NEG_INF, F32)
        l_ref[...] = jnp.zeros(l_ref.shape, F32)
        acc_ref[...] = jnp.zeros(acc_ref.shape, F32)

    def step(diagonal):
        s = lax.dot_general(qs_ref[...], k_ref[...].astype(BF16), _NT, preferred_element_type=F32)
        if diagonal:
            rr = lax.broadcasted_iota(jnp.int32, s.shape, 0)
            cc = lax.broadcasted_iota(jnp.int32, s.shape, 1)
            s = jnp.where(cc <= jnp.where(rr >= tq, rr - tq, rr), s, NEG_INF)
        m_old = m_ref[...]
        m_new = jnp.maximum(m_old, jnp.max(s, axis=-1, keepdims=True))
        alpha = jnp.exp(m_old - m_new)
        p = jnp.exp(s - m_new)
        l_ref[...] = alpha * l_ref[...] + jnp.sum(p, axis=-1, keepdims=True)
        acc_ref[...] = alpha * acc_ref[...] + jnp.dot(p.astype(BF16), v_ref[...].astype(BF16),
                                                      preferred_element_type=F32)
        m_ref[...] = m_new

    @pl.when(j < i)
    def _():
        step(False)

    @pl.when(j == i)
    def _():
        step(True)
        lam = _lambda(lp_ref, lam_init)
        on = acc_ref[...] / l_ref[...]
        o = on[:tq] - lam * on[tq:]
        o_ref[...] = _rms(o, SUBLN_EPS) * sub_ref[...] * (1.0 - lam_init)


def _flash(qs, kr, proj, lam_p, subln, n_seq, tq, lam_init):
    N = qs.shape[0]
    nq = (N // n_seq) // tq
    kern = functools.partial(_flash_kernel, tq=tq, lam_init=lam_init)
    vcol = COL_V // HEAD_V
    pairs = [(i, j) for i in range(nq) for j in range(i + 1)]
    qi = jnp.asarray([p[0] for p in pairs], jnp.int32)
    kj = jnp.asarray([p[1] for p in pairs], jnp.int32)
    grid_spec = pltpu.PrefetchScalarGridSpec(
        num_scalar_prefetch=2,
        grid=(n_seq, ATT_HEADS, len(pairs)),
        in_specs=[pl.BlockSpec((tq, 128), lambda b, h, t, qi, kj: (b * nq + qi[t], h)),
                  pl.BlockSpec((tq, 128), lambda b, h, t, qi, kj: (b * nq + kj[t], h)),
                  pl.BlockSpec((tq, HEAD_V), lambda b, h, t, qi, kj: (b * nq + kj[t], vcol + h)),
                  pl.BlockSpec((8, HEAD_QK), lambda b, h, t, qi, kj: (0, 0)),
                  pl.BlockSpec((1, HEAD_V), lambda b, h, t, qi, kj: (0, 0))],
        out_specs=pl.BlockSpec((tq, HEAD_V), lambda b, h, t, qi, kj: (b * nq + qi[t], h)),
        scratch_shapes=[pltpu.VMEM((2 * tq, 128), BF16), pltpu.VMEM((2 * tq, 1), F32),
                        pltpu.VMEM((2 * tq, 1), F32), pltpu.VMEM((2 * tq, HEAD_V), F32)],
    )
    return pl.pallas_call(
        kern, grid_spec=grid_spec,
        out_shape=jax.ShapeDtypeStruct((N, ATT_W), F32),
        compiler_params=_cparams(("arbitrary",) * 3),
    )(qi, kj, qs, kr, proj, lam_p, subln)


def _decode_kernel(pt_ref, q_ref, kc_ref, vc_ref, kn_ref, vn_ref, lp_ref, sub_ref, o_ref,
                   qrow, m_s, l_s, acc, *, ts, n_pages, lam_init):
    p = pl.program_id(1)
    nrow = 2 * ATT_HEADS * ts
    row = lax.broadcasted_iota(jnp.int32, (nrow, ATT_W), 0)
    lane = lax.broadcasted_iota(jnp.int32, (nrow, ATT_W), 1)
    row_head = (row % (ATT_HEADS * ts)) // ts
    row_sub = row // (ATT_HEADS * ts)

    @pl.when(p == 0)
    def _():
        q = q_ref[...]
        qt = jnp.concatenate([q] * (2 * ATT_HEADS), axis=0)
        qrow[...] = jnp.where(lane // HEAD_QK == row_head * 2 + row_sub, qt, 0.0)
        m_s[...] = jnp.full(m_s.shape, NEG_INF, F32)
        l_s[...] = jnp.zeros(l_s.shape, F32)
        acc[...] = jnp.zeros(acc.shape, F32)

    own = p == n_pages
    k = jnp.where(own, kn_ref[...], kc_ref[...]).astype(BF16)
    v = jnp.where(own, vn_ref[...], vc_ref[...]).astype(BF16)
    s = lax.dot_general(qrow[...].astype(BF16), k, _NT, preferred_element_type=F32)
    r2 = lax.broadcasted_iota(jnp.int32, (nrow, PAGE), 0)
    c2 = lax.broadcasted_iota(jnp.int32, (nrow, PAGE), 1)
    keep = jnp.logical_or(jnp.logical_not(own), c2 <= r2 % ts)
    s = jnp.where(keep, s, NEG_INF)
    m_new = jnp.maximum(m_s[...], jnp.max(s, axis=-1, keepdims=True))
    alpha = jnp.exp(m_s[...] - m_new)
    pr = jnp.exp(s - m_new)
    l_s[...] = alpha * l_s[...] + jnp.sum(pr, axis=-1, keepdims=True)
    acc[...] = alpha * acc[...] + jnp.dot(pr.astype(BF16), v, preferred_element_type=F32)
    m_s[...] = m_new

    @pl.when(own)
    def _():
        lam = _lambda(lp_ref, lam_init)
        nh = ATT_HEADS * ts
        on = acc[...] / l_s[...]
        o_all = on[:nh] - lam * on[nh:]
        rh = lax.broadcasted_iota(jnp.int32, (nh, ATT_W), 0) // ts
        lh = lax.broadcasted_iota(jnp.int32, (nh, ATT_W), 1) // HEAD_V
        o_sel = jnp.where(rh == lh, o_all, 0.0)
        o = o_sel[0:ts]
        for h in range(1, ATT_HEADS):
            o = o + o_sel[h * ts:(h + 1) * ts]
        outs = []
        for h in range(ATT_HEADS):
            oh = o[:, h * HEAD_V:(h + 1) * HEAD_V]
            outs.append(_rms(oh, SUBLN_EPS) * sub_ref[...] * (1.0 - lam_init))
        o_ref[...] = jnp.concatenate(outs, axis=-1)


def _decode_attn(qs, kn, vn, cache_k4, cache_v4, page_table, lam_p, subln, layer, lam_init):
    bs, ts, _ = qs.shape
    n_pages = page_table.shape[1]
    nrow = 2 * ATT_HEADS * ts
    kern = functools.partial(_decode_kernel, ts=ts, n_pages=n_pages, lam_init=lam_init)

    def cache_map(b, p, pt):
        return (pt[b, jnp.minimum(p, n_pages - 1)], layer, 0, 0)

    grid_spec = pltpu.PrefetchScalarGridSpec(
        num_scalar_prefetch=1,
        grid=(bs, n_pages + 1),
        in_specs=[pl.BlockSpec((None, ts, ATT_W), lambda b, p, pt: (b, 0, 0)),
                  pl.BlockSpec((None, None, PAGE, ATT_W), cache_map),
                  pl.BlockSpec((None, None, PAGE, ATT_W), cache_map),
                  pl.BlockSpec((None, PAGE, ATT_W), lambda b, p, pt: (b, 0, 0)),
                  pl.BlockSpec((None, PAGE, ATT_W), lambda b, p, pt: (b, 0, 0)),
                  pl.BlockSpec((8, HEAD_QK), lambda b, p, pt: (0, 0)),
                  pl.BlockSpec((1, HEAD_V), lambda b, p, pt: (0, 0))],
        out_specs=pl.BlockSpec((None, ts, ATT_W), lambda b, p, pt: (b, 0, 0)),
        scratch_shapes=[pltpu.VMEM((nrow, ATT_W), F32), pltpu.VMEM((nrow, 1), F32),
                        pltpu.VMEM((nrow, 1), F32), pltpu.VMEM((nrow, ATT_W), F32)],
    )
    return pl.pallas_call(
        kern, grid_spec=grid_spec,
        out_shape=jax.ShapeDtypeStruct((bs, ts, ATT_W), F32),
        compiler_params=_cparams(("arbitrary", "arbitrary")),
    )(page_table, qs, cache_k4, cache_v4, kn, vn, lam_p, subln)


def _merge_kernel(x_ref, y_ref, bon_ref, g_ref, oatt_ref, gates_ref, gt_ref, ln_ref,
                  wrw_ref, watt_ref, wo_ref, o_ref):
    y = y_ref[...]
    n = y.shape[0]
    parts = []
    for h in range(RW_HEADS):
        yh = y[:, h * RW_HEAD:(h + 1) * RW_HEAD]
        mu = jnp.mean(yh, axis=-1, keepdims=True)
        d = yh - mu
        var = jnp.mean(d * d, axis=-1, keepdims=True)
        parts.append(d * lax.rsqrt(var + RW_GN_EPS))
    yn = jnp.concatenate(parts, axis=-1) * ln_ref[0:1, :] + ln_ref[1:2, :]
    out_rw = (yn + bon_ref[...]) * g_ref[...]
    y_rw = _dot(out_rw, wrw_ref[...])
    y_att = _dot(oatt_ref[...], watt_ref[...])
    gates = gates_ref[...]
    merged = (jax.nn.sigmoid(gates[:, :D_MODEL]) * y_rw
              + jax.nn.sigmoid(gates[:, D_MODEL:]) * y_att)
    o_ref[...] = x_ref[...] + gt_ref[...] * _dot(merged, wo_ref[...])


def _merge(x, y, bon, g, oatt, proj, gt, ln, wrw, watt, wo, tm):
    N = x.shape[0]
    tpg = (N // gt.shape[0]) // tm
    s512 = pl.BlockSpec((tm, RW_W), lambda i: (i, 0))
    full = lambda a: pl.BlockSpec(a.shape, lambda i: (0,) * a.ndim)
    return pl.pallas_call(
        _merge_kernel,
        grid=(N // tm,),
        in_specs=[pl.BlockSpec((tm, D_MODEL), lambda i: (i, 0)), s512, s512, s512, s512,
                  pl.BlockSpec((tm, 2 * D_MODEL), lambda i: (i, 0)),
                  _mod_spec(gt, tm, tpg), full(ln), full(wrw), full(watt), full(wo)],
        out_specs=pl.BlockSpec((tm, D_MODEL), lambda i: (i, 0)),
        out_shape=jax.ShapeDtypeStruct((N, D_MODEL), F32),
        compiler_params=_cparams(("arbitrary",)),
    )(x, y, bon, g, oatt, proj, gt, ln, wrw, watt, wo)


def _first_argmax(cols):
    best = cols[0]
    idx = jnp.zeros(best.shape, jnp.int32)
    for n, c in enumerate(cols[1:], start=1):
        take = c > best
        best = jnp.where(take, c, best)
        idx = jnp.where(take, n, idx)
    return best, idx


def _router_kernel(x_ref, g_ref, sc_ref, sh_ref, wr_ref, eb_ref, h_ref, gate_ref):
    y = _rms(x_ref[...], NORM_EPS) * g_ref[...]
    h = y * (1.0 + sc_ref[...]) + sh_ref[...]
    h_ref[...] = h.astype(BF16)
    logits = _dot_hi(h, wr_ref[...])
    s = jax.nn.sigmoid(logits)
    sel = s + eb_ref[...]
    sc_cols = [sel[:, e:e + 1] for e in range(N_EXPERTS)]
    s_cols = [s[:, e:e + 1] for e in range(N_EXPERTS)]
    grp = []
    for gi in range(N_GROUPS):
        cols = sc_cols[gi * EPG:(gi + 1) * EPG]
        m1, i1 = _first_argmax(cols)
        rest = [jnp.where(i1 == n, -jnp.inf, c) for n, c in enumerate(cols)]
        m2, _ = _first_argmax(rest)
        grp.append(m1 + m2)
    _, g_idx = _first_argmax(grp)
    pick = lambda cols_all, n: sum(jnp.where(g_idx == gi, cols_all[gi * EPG + n], 0.0)
                                   for gi in range(N_GROUPS))
    sel_g = [pick(sc_cols, n) for n in range(EPG)]
    s_g = [pick(s_cols, n) for n in range(EPG)]
    _, loc1 = _first_argmax(sel_g)
    _, loc2 = _first_argmax([jnp.where(loc1 == n, -jnp.inf, c) for n, c in enumerate(sel_g)])
    w_1 = sum(jnp.where(loc1 == n, s_g[n], 0.0) for n in range(EPG))
    w_2 = sum(jnp.where(loc2 == n, s_g[n], 0.0) for n in range(EPG))
    tot = w_1 + w_2
    w_1, w_2 = w_1 / tot, w_2 / tot
    e1 = g_idx * EPG + loc1
    e2 = g_idx * EPG + loc2
    lane = lax.broadcasted_iota(jnp.int32, logits.shape, 1)
    gate_ref[...] = jnp.where(lane == e1, w_1, 0.0) + jnp.where(lane == e2, w_2, 0.0)


def _router(x, g, sc, sh, wr, eb, tm):
    N, D = x.shape
    tpg = (N // sc.shape[0]) // tm
    return pl.pallas_call(
        _router_kernel,
        grid=(N // tm,),
        in_specs=[pl.BlockSpec((tm, D), lambda i: (i, 0)),
                  pl.BlockSpec((1, D), lambda i: (0, 0)),
                  _mod_spec(sc, tm, tpg), _mod_spec(sh, tm, tpg),
                  pl.BlockSpec((D, 128), lambda i: (0, 0)),
                  pl.BlockSpec((1, 128), lambda i: (0, 0))],
        out_specs=[pl.BlockSpec((tm, D), lambda i: (i, 0)),
                   pl.BlockSpec((tm, 128), lambda i: (i, 0))],
        out_shape=[jax.ShapeDtypeStruct((N, D), BF16), jax.ShapeDtypeStruct((N, 128), F32)],
        compiler_params=_cparams(("arbitrary",)),
    )(x, g, sc, sh, wr, eb)


def _moe_kernel(x_ref, h_ref, gate_ref, gt_ref, w1_ref, w3_ref, w2_ref, o_ref, acc_ref):
    e = pl.program_id(1)

    @pl.when(e == 0)
    def _():
        acc_ref[...] = jnp.zeros(acc_ref.shape, F32)

    h = h_ref[...]
    a1 = jnp.dot(h, w1_ref[...], preferred_element_type=F32)
    a3 = jnp.dot(h, w3_ref[...], preferred_element_type=F32)
    lane = lax.broadcasted_iota(jnp.int32, gate_ref.shape, 1)
    gcol = jnp.sum(jnp.where(lane == e, gate_ref[...], 0.0), axis=-1, keepdims=True)
    act = (a1 * jax.nn.sigmoid(a1)) * a3 * gcol
    acc_ref[...] += _dot(act, w2_ref[...])

    @pl.when(e == pl.num_programs(1) - 1)
    def _():
        o_ref[...] = x_ref[...] + gt_ref[...] * acc_ref[...]


def _moe(x, h, gate, gt, w1, w3, w2, tm):
    N, D = x.shape
    tpg = (N // gt.shape[0]) // tm
    return pl.pallas_call(
        _moe_kernel,
        grid=(N // tm, N_EXPERTS),
        in_specs=[pl.BlockSpec((tm, D), lambda i, e: (i, 0)),
                  pl.BlockSpec((tm, D), lambda i, e: (i, 0)),
                  pl.BlockSpec((tm, 128), lambda i, e: (i, 0)),
                  _mod_spec(gt, tm, tpg),
                  pl.BlockSpec((None, D, D_EXPERT), lambda i, e: (e, 0, 0)),
                  pl.BlockSpec((None, D, D_EXPERT), lambda i, e: (e, 0, 0)),
                  pl.BlockSpec((None, D_EXPERT, D), lambda i, e: (e, 0, 0))],
        out_specs=pl.BlockSpec((tm, D), lambda i, e: (i, 0)),
        out_shape=jax.ShapeDtypeStruct((N, D), F32),
        scratch_shapes=[pltpu.VMEM((tm, D), F32)],
        compiler_params=_cparams(("arbitrary", "arbitrary")),
    )(x, h, gate, gt, w1, w3, w2)


def _final_norm_kernel(x_ref, g_ref, o_ref):
    o_ref[...] = _rms(x_ref[...], NORM_EPS) * g_ref[...]


def _final_norm(x, g, tm):
    N, D = x.shape
    return pl.pallas_call(
        _final_norm_kernel,
        grid=(N // tm,),
        in_specs=[pl.BlockSpec((tm, D), lambda i: (i, 0)), pl.BlockSpec((1, D), lambda i: (0, 0))],
        out_specs=pl.BlockSpec((tm, D), lambda i: (i, 0)),
        out_shape=jax.ShapeDtypeStruct((N, D), F32),
        compiler_params=_cparams(("arbitrary",)),
    )(x, g)


def _group_layer(x, mods, lw, cfg, z0, s0, attend):
    sh1, sc1, gt1, sh2, sc2, gt2 = mods
    tm = cfg["tm"]
    proj = _nm_matmul(x, lw["norm1_g"], sc1, sh1, lw["w_in"], tm, cfg["tn_in"])
    r, lwd, kk, ka, k2, v, g, bon = _rw_prep(proj, z0, lw["mu"], lw["rw_vecs"], lw["wup"],
                                               cfg["tm_prep"], cfg["seq"])
    y, s_fin = _wkv(r, lwd, kk, ka, k2, v, s0, cfg["n_seq"], cfg["tb"], cfg["chunk"])
    qs, kr = _rope(proj, cfg["rope_tabs"], cfg["tm_prep"])
    oatt = attend(qs, kr, proj)
    x = _merge(x, y, bon, g, oatt, proj, gt1, lw["ln"], lw["w_rw_out"], lw["w_att_out"],
               lw["w_o"], tm)
    h2, gate = _router(x, lw["norm2_g"], sc2, sh2, lw["w_router"], lw["e_bias"], tm)
    x = _moe(x, h2, gate, gt2, lw["moe_w1"], lw["moe_w3"], lw["moe_w2"], cfg["tm_moe"])
    return x, proj, kr, s_fin


def kernel(x_prompt, x_sample, c_prompt, c_sample, cache_k, cache_v, page_table, state_shift, state_wkv, w_ada, b_ada, norm1_g, norm2_g, w_in, rw_mu, rw_w0, rw_w_up, rw_a0, rw_a_up, rw_g_up, rw_k_k, rw_k_a, rw_r_k, rw_ln_w, rw_ln_b, w_rw_out, att_lq1, att_lk1, att_lq2, att_lk2, att_subln, w_att_out, w_o, w_router, e_bias, moe_w1, moe_w3, moe_w2, normf_g):
    bp, tp, D = x_prompt.shape
    bs, ts, _ = x_sample.shape
    depth = w_in.shape[0]
    n_pages = page_table.shape[1]
    past_len = n_pages * PAGE
    n_p, n_s = bp * tp, bs * ts

    z_w, q_w, k_w, v_w, grw_w, gatt_w = jnp.split(
        w_in, [RW_PROJ, RW_PROJ + ATT_W, RW_PROJ + 2 * ATT_W, RW_PROJ + 3 * ATT_W,
               RW_PROJ + 3 * ATT_W + D], axis=-1)
    w_in_p = jnp.concatenate(
        [grw_w, gatt_w, q_w, k_w, v_w, z_w,
         jnp.zeros((depth, D, RW_PROJ_PAD - RW_PROJ), F32)], axis=-1).astype(BF16)
    mu_p = jnp.pad(rw_mu, ((0, 0), (0, RW_PROJ_PAD - RW_PROJ)))
    wup = jnp.zeros((depth, 3, LORA_PAD, RW_W), F32)
    wup = wup.at[:, 0, 0:DECAY_LORA].set(rw_w_up)
    wup = wup.at[:, 1, DECAY_LORA:DECAY_LORA + AAA_LORA].set(rw_a_up)
    wup = wup.at[:, 2, DECAY_LORA + AAA_LORA:DECAY_LORA + AAA_LORA + GATE_LORA].set(rw_g_up)
    wup = wup.astype(BF16)
    rw_vecs = jnp.stack([rw_w0, rw_a0, rw_k_k, rw_k_a, rw_r_k.reshape(depth, RW_W),
                         jnp.zeros_like(rw_w0), jnp.zeros_like(rw_w0), jnp.zeros_like(rw_w0)], axis=1)
    ln = jnp.stack([rw_ln_w, rw_ln_b], axis=1)
    lam_p = jnp.stack([att_lq1, att_lk1, att_lq2, att_lk2] + [jnp.zeros_like(att_lq1)] * 4, axis=1)
    w_router_p = jnp.pad(w_router, ((0, 0), (0, 128 - N_EXPERTS)))
    e_bias_p = jnp.pad(e_bias, (0, 128 - N_EXPERTS)).reshape(1, 128)
    w_rw_out_b, w_att_out_b, w_o_b = (w.astype(BF16) for w in (w_rw_out, w_att_out, w_o))
    moe_w1_b, moe_w3_b, moe_w2_b = (w.astype(BF16) for w in (moe_w1, moe_w3, moe_w2))

    n_c = bp + bs
    n_cp = -(-n_c // 8) * 8
    c_all = jnp.pad(jnp.concatenate([c_prompt, c_sample], axis=0), ((0, n_cp - n_c), (0, 0)))
    mod = _ada_mod(c_all, w_ada, b_ada)

    tm_p = min(512, tp)
    cfg_p = dict(tm=tm_p, tn_in=768, tm_prep=min(256, tp), seq=tp, n_seq=bp, tb=min(256, tp),
                 chunk=32, tm_moe=min(1024, tp), rope_tabs=_rope_tables(jnp.arange(tp)))
    cfg_s = dict(tm=n_s, tn_in=768, tm_prep=n_s, seq=ts, n_seq=bs, tb=8, chunk=8, tm_moe=n_s,
                 rope_tabs=_rope_tables(jnp.tile(past_len + jnp.arange(ts), bs)))

    cache_k4 = cache_k.reshape(cache_k.shape[0], depth, PAGE, ATT_W)
    cache_v4 = cache_v.reshape(cache_v.shape[0], depth, PAGE, ATT_W)
    zp0 = jnp.zeros((bp, 1, RW_PROJ_PAD), F32)
    sp0 = jnp.zeros((bp, RW_HEADS, RW_HEAD, RW_HEAD), F32)

    xp = x_prompt.reshape(n_p, D)
    xs = x_sample.reshape(n_s, D)
    outs = {k: [] for k in ("kp", "vp", "zp", "sp", "ks", "vs", "zs", "ss")}
    for l in range(depth):
        lam_init = 0.8 - 0.6 * math.exp(-0.3 * l)
        lw = dict(norm1_g=norm1_g[l].reshape(1, D), norm2_g=norm2_g[l].reshape(1, D),
                  w_in=w_in_p[l], mu=mu_p[l].reshape(1, RW_PROJ_PAD), rw_vecs=rw_vecs[l],
                  wup=wup[l], ln=ln[l], w_rw_out=w_rw_out_b[l], w_att_out=w_att_out_b[l],
                  w_o=w_o_b[l], w_router=w_router_p, e_bias=e_bias_p,
                  moe_w1=moe_w1_b[l], moe_w3=moe_w3_b[l], moe_w2=moe_w2_b[l])
        subln = att_subln[l].reshape(1, HEAD_V)
        mods_p = [m.reshape(bp, 1, D) for m in jnp.split(mod[l, :bp], 6, axis=-1)]
        mods_s = [jnp.repeat(m, ts, axis=0).reshape(1, n_s, D)
                  for m in jnp.split(mod[l, bp:n_c], 6, axis=-1)]

        attend_p = lambda qs, kr, proj: _flash(qs, kr, proj, lam_p[l], subln, bp,
                                               min(512, tp), lam_init)
        xp, proj_p, kr_p, s_fin = _group_layer(xp, mods_p, lw, cfg_p, zp0, sp0, attend_p)
        outs["kp"].append(kr_p.reshape(bp, tp, ATT_HEADS, 2, HEAD_QK))
        outs["vp"].append(proj_p[:, COL_V:COL_V + ATT_W].reshape(bp, tp, ATT_HEADS, HEAD_V))
        outs["zp"].append(proj_p.reshape(bp, tp, IN_W_PAD)[:, -1, COL_Z:COL_Z + RW_PROJ])
        outs["sp"].append(s_fin)

        def attend_s(qs, kr, proj):
            pad = lambda a: jnp.pad(a.reshape(bs, ts, ATT_W), ((0, 0), (0, PAGE - ts), (0, 0)))
            return _decode_attn(qs.reshape(bs, ts, ATT_W), pad(kr), pad(proj[:, COL_V:COL_V + ATT_W]),
                                cache_k4, cache_v4, page_table, lam_p[l], subln, l,
                                lam_init).reshape(n_s, ATT_W)

        z0_s = jnp.pad(jnp.repeat(state_shift[l], ts, axis=0), ((0, 0), (0, RW_PROJ_PAD - RW_PROJ)))
        xs, proj_s, kr_s, s_fin = _sample_layer(xs, mods_s, lw, cfg_s, z0_s, state_wkv[l], attend_s, bs, ts)
        outs["ks"].append(kr_s.reshape(bs, ts, ATT_HEADS, 2, HEAD_QK))
        outs["vs"].append(proj_s[:, COL_V:COL_V + ATT_W].reshape(bs, ts, ATT_HEADS, HEAD_V))
        outs["zs"].append(proj_s.reshape(bs, ts, IN_W_PAD)[:, -1, COL_Z:COL_Z + RW_PROJ])
        outs["ss"].append(s_fin)

    y_prompt = _final_norm(xp, normf_g.reshape(1, D), tm_p).reshape(bp, tp, D)
    y_sample = _final_norm(xs, normf_g.reshape(1, D), n_s).reshape(bs, ts, D)
    return (y_prompt, y_sample,
            jnp.stack(outs["kp"], axis=1), jnp.stack(outs["vp"], axis=1),
            jnp.stack(outs["zp"], axis=0), jnp.stack(outs["sp"], axis=0),
            jnp.stack(outs["ks"], axis=1), jnp.stack(outs["vs"], axis=1),
            jnp.stack(outs["zs"], axis=0), jnp.stack(outs["ss"], axis=0))


def _sample_layer(x, mods, lw, cfg, z0, s0, attend, bs, ts):
    sh1, sc1, gt1, sh2, sc2, gt2 = mods
    tm = cfg["tm"]
    proj = _nm_matmul(x, lw["norm1_g"], sc1, sh1, lw["w_in"], tm, cfg["tn_in"])
    prep = _rw_prep(proj, z0, lw["mu"], lw["rw_vecs"], lw["wup"], cfg["tm_prep"], cfg["seq"])
    r, lwd, kk, ka, k2, v, g, bon = prep
    pad8 = lambda a: jnp.pad(a.reshape(bs, ts, RW_W), ((0, 0), (0, 8 - ts), (0, 0))).reshape(bs * 8, RW_W)
    y8, s_fin = _wkv(*(pad8(a) for a in (r, lwd, kk, ka, k2, v)), s0, bs, 8, 8)
    y = y8.reshape(bs, 8, RW_W)[:, :ts].reshape(bs * ts, RW_W)
    qs, kr = _rope(proj, cfg["rope_tabs"], cfg["tm_prep"])
    oatt = attend(qs, kr, proj)
    x = _merge(x, y, bon, g, oatt, proj, gt1, lw["ln"], lw["w_rw_out"], lw["w_att_out"],
               lw["w_o"], tm)
    h2, gate = _router(x, lw["norm2_g"], sc2, sh2, lw["w_router"], lw["e_bias"], tm)
    x = _moe(x, h2, gate, gt2, lw["moe_w1"], lw["moe_w3"], lw["moe_w2"], cfg["tm_moe"])
    return x, proj, kr, s_fin
```

```python
import functools
import math

import jax
import jax.numpy as jnp
from jax import lax
from jax.experimental import pallas as pl
from jax.experimental.pallas import tpu as pltpu

F32 = jnp.float32
BF16 = jnp.bfloat16

D_MODEL = 1024
RW_HEAD = 64
RW_HEADS = 8
RW_W = RW_HEADS * RW_HEAD
DECAY_LORA = 32
AAA_LORA = 32
GATE_LORA = 96
RW_PROJ = 3 * RW_W + DECAY_LORA + AAA_LORA + GATE_LORA
RW_PROJ_PAD = 1792
LORA_PAD = RW_PROJ_PAD - 3 * RW_W
RW_GN_EPS = 64e-5
ATT_HEADS = 4
HEAD_QK = 64
HEAD_V = 128
ATT_W = 512
ROPE_DIM = HEAD_QK // 4
ROPE_THETA = 500000.0
SUBLN_EPS = 1e-5
NEG_INF = -1e30
N_EXPERTS = 16
N_GROUPS = 4
EPG = 4
D_EXPERT = 512
NORM_EPS = 1e-6
PAGE = 128

COL_GATE = 0
COL_Q = 2048
COL_V = 3072
COL_Z = 3584
IN_W_PAD = COL_Z + RW_PROJ_PAD

VMEM_LIMIT = 56 * 1024 * 1024

_NN = (((1,), (0,)), ((), ()))
_NT = (((1,), (1,)), ((), ()))
_TN = (((0,), (0,)), ((), ()))


def _dot(a, b, dims=_NN):
    return lax.dot_general(a.astype(BF16), b.astype(BF16), dims, preferred_element_type=F32)


def _dot_hi(a, b, dims=_NN):
    return lax.dot_general(a, b, dims, preferred_element_type=F32,
                           precision=lax.Precision.HIGHEST)


def _cparams(sem):
    return pltpu.CompilerParams(dimension_semantics=sem, vmem_limit_bytes=VMEM_LIMIT)


def _rms(x, eps):
    return x * lax.rsqrt(jnp.mean(x * x, axis=-1, keepdims=True) + eps)


def _ada_kernel(c_ref, w_ref, b_ref, o_ref):
    c = c_ref[...]
    sc = c * jax.nn.sigmoid(c)
    o_ref[...] = _dot(sc, w_ref[...]) + b_ref[...]


def _ada_mod(c_all, w_ada, b_ada):
    L, D, N6 = w_ada.shape
    M = c_all.shape[0]
    tn = 1536
    return pl.pallas_call(
        _ada_kernel,
        grid=(L, N6 // tn),
        in_specs=[pl.BlockSpec((M, D), lambda l, j: (0, 0)),
                  pl.BlockSpec((None, D, tn), lambda l, j: (l, 0, j)),
                  pl.BlockSpec((None, 1, tn), lambda l, j: (l, 0, j))],
        out_specs=pl.BlockSpec((None, M, tn), lambda l, j: (l, 0, j)),
        out_shape=jax.ShapeDtypeStruct((L, M, N6), F32),
        compiler_params=_cparams(("arbitrary", "arbitrary")),
    )(c_all, w_ada, b_ada.reshape(L, 1, N6))


def _mod_spec(m, tm, tiles_per_group):
    r = m.shape[1]
    return pl.BlockSpec((None, r, m.shape[2]), lambda i, *_: (i // tiles_per_group, 0, 0))


def _nm_matmul_kernel(x_ref, g_ref, sc_ref, sh_ref, w_ref, o_ref, h_ref):
    @pl.when(pl.program_id(1) == 0)
    def _():
        y = _rms(x_ref[...], NORM_EPS) * g_ref[...]
        h_ref[...] = (y * (1.0 + sc_ref[...]) + sh_ref[...]).astype(BF16)

    o_ref[...] = jnp.dot(h_ref[...], w_ref[...], preferred_element_type=F32)


def _nm_matmul(x, g, sc, sh, w, tm, tn):
    N, D = x.shape
    n_out = w.shape[1]
    tpg = (N // sc.shape[0]) // tm
    return pl.pallas_call(
        _nm_matmul_kernel,
        grid=(N // tm, n_out // tn),
        in_specs=[pl.BlockSpec((tm, D), lambda i, j: (i, 0)),
                  pl.BlockSpec((1, D), lambda i, j: (0, 0)),
                  _mod_spec(sc, tm, tpg), _mod_spec(sh, tm, tpg),
                  pl.BlockSpec((D, tn), lambda i, j: (0, j))],
        out_specs=pl.BlockSpec((tm, tn), lambda i, j: (i, j)),
        out_shape=jax.ShapeDtypeStruct((N, n_out), F32),
        scratch_shapes=[pltpu.VMEM((tm, D), BF16)],
        compiler_params=_cparams(("arbitrary", "arbitrary")),
    )(x, g, sc, sh, w)


def _head_sum(x):
    parts = []
    for h in range(RW_HEADS):
        s = jnp.sum(x[:, h * RW_HEAD:(h + 1) * RW_HEAD], axis=-1, keepdims=True)
        parts.append(jnp.broadcast_to(s, (x.shape[0], RW_HEAD)))
    return jnp.concatenate(parts, axis=-1)


def _rw_prep_kernel(z_ref, z0_ref, mu_ref, vec_ref, wup_ref,
                    r_ref, lw_ref, kk_ref, ka_ref, k2_ref, v_ref, g_ref, bon_ref,
                    carry_ref, *, tm, seq_len):
    i = pl.program_id(0)
    z = z_ref[...]
    rolled = pltpu.roll(z, 1, axis=0)
    row = lax.broadcasted_iota(jnp.int32, z.shape, 0)
    if seq_len >= tm:
        tiles_per_seq = seq_len // tm
        first = jnp.where(i % tiles_per_seq == 0, z0_ref[...], carry_ref[0:1, :])
        z_prev = jnp.where(row == 0, first, rolled)
        carry_ref[0:1, :] = z[tm - 1:tm, :]
    else:
        z_prev = jnp.where(row % seq_len == 0, z0_ref[...], rolled)
    zs = z + (z_prev - z) * mu_ref[...]
    r = zs[:, 0:RW_W]
    k = zs[:, RW_W:2 * RW_W]
    v = zs[:, 2 * RW_W:3 * RW_W]
    tail = zs[:, 3 * RW_W:]
    w0, a0, k_k, k_a, r_k = (vec_ref[j:j + 1, :] for j in range(5))
    dw = _dot(jnp.tanh(tail), wup_ref[0])
    da = _dot(tail, wup_ref[1])
    g = _dot(jax.nn.sigmoid(tail), wup_ref[2])
    t = -(w0 + dw)
    softplus = jnp.maximum(t, 0.0) + jnp.log1p(jnp.exp(-jnp.abs(t)))
    lw = -jnp.exp(-softplus - 0.5)
    a = jax.nn.sigmoid(a0 + da)
    kk = k * k_k
    kk = kk / jnp.maximum(jnp.sqrt(_head_sum(kk * kk)), 1e-12)
    k2 = k * (1.0 + (a - 1.0) * k_a)
    r_ref[...] = r
    lw_ref[...] = lw
    kk_ref[...] = kk
    ka_ref[...] = kk * a
    k2_ref[...] = k2
    v_ref[...] = v
    g_ref[...] = g
    bon_ref[...] = _head_sum(r * k2 * r_k) * v


def _rw_prep(proj, z0, mu, vecs, wup, tm, seq_len):
    N = proj.shape[0]
    kern = functools.partial(_rw_prep_kernel, tm=tm, seq_len=seq_len)
    if seq_len >= tm:
        tps = seq_len // tm
        z0_spec = pl.BlockSpec((None, 1, RW_PROJ_PAD), lambda i: (i // tps, 0, 0))
    else:
        z0_spec = pl.BlockSpec((tm, RW_PROJ_PAD), lambda i: (i, 0))
    o_spec = pl.BlockSpec((tm, RW_W), lambda i: (i, 0))
    return pl.pallas_call(
        kern,
        grid=(N // tm,),
        in_specs=[pl.BlockSpec((tm, RW_PROJ_PAD), lambda i: (i, COL_Z // RW_PROJ_PAD)),
                  z0_spec,
                  pl.BlockSpec((1, RW_PROJ_PAD), lambda i: (0, 0)),
                  pl.BlockSpec((8, RW_W), lambda i: (0, 0)),
                  pl.BlockSpec((3, LORA_PAD, RW_W), lambda i: (0, 0, 0))],
        out_specs=[o_spec] * 8,
        out_shape=[jax.ShapeDtypeStruct((N, RW_W), F32)] * 8,
        scratch_shapes=[pltpu.VMEM((8, RW_PROJ_PAD), F32)],
        compiler_params=_cparams(("arbitrary",)),
    )(proj, z0, mu, vecs, wup)


def _wkv_kernel(r_ref, lw_ref, kk_ref, ka_ref, k2_ref, v_ref, s0_ref, y_ref, sout_ref,
                s_ref, *, tb, C):
    tblk = pl.program_id(1)
    n_pair = RW_HEADS // 2
    zero64 = jnp.zeros((RW_HEAD, RW_HEAD), F32)

    @pl.when(tblk == 0)
    def _():
        for p in range(n_pair):
            s_ref[p] = jnp.concatenate(
                [jnp.concatenate([s0_ref[2 * p], zero64], axis=1),
                 jnp.concatenate([zero64, s0_ref[2 * p + 1]], axis=1)], axis=0)

    C2 = 2 * C
    ri = lax.broadcasted_iota(jnp.int32, (C, C), 0)
    ci = lax.broadcasted_iota(jnp.int32, (C, C), 1)
    tri = (ri >= ci).astype(F32)
    r2 = lax.broadcasted_iota(jnp.int32, (C2, C2), 0)
    c2 = lax.broadcasted_iota(jnp.int32, (C2, C2), 1)
    low_incl = r2 >= c2
    low_strict = r2 > c2
    eye = (r2 == c2).astype(F32)
    lo_lanes = lax.broadcasted_iota(jnp.int32, (C, 128), 1) < RW_HEAD
    n_sq = int(math.log2(C)) - 1
    pairs = range(n_pair)

    def stack(x, p):
        xp = x[:, p * 128:(p + 1) * 128]
        return jnp.concatenate([jnp.where(lo_lanes, xp, 0.0), jnp.where(lo_lanes, 0.0, xp)], axis=0)

    def chunk(c, carry):
        rows = pl.ds(pl.multiple_of(c * C, C), C)
        r = r_ref[rows, :]
        lw = lw_ref[rows, :]
        kk = kk_ref[rows, :]
        ka = ka_ref[rows, :]
        k2 = k2_ref[rows, :]
        v = v_ref[rows, :]
        cs = _dot_hi(tri, lw)
        tot = cs[C - 1:C, :]
        p_inv = jnp.exp(-cs)
        p_rem = jnp.exp(tot - cs)
        p_tot = jnp.exp(tot)
        a_s = [stack(-kk * jnp.exp(cs - lw), p).astype(BF16) for p in pairs]
        r_s = [stack(r * jnp.exp(cs), p).astype(BF16) for p in pairs]
        b_s = [stack(ka * p_inv, p).astype(BF16) for p in pairs]
        k_s = [stack(k2 * p_inv, p).astype(BF16) for p in pairs]
        bk_r = [jnp.concatenate([stack(ka * p_rem, p), stack(k2 * p_rem, p)], axis=0).astype(BF16)
                for p in pairs]
        v_s = [stack(v, p).astype(BF16) for p in pairs]
        ar = [jnp.concatenate([a_s[p], r_s[p]], axis=0) for p in pairs]
        gb = [_dot(ar[p], b_s[p], _NT) for p in pairs]
        gk = [_dot(ar[p], k_s[p], _NT) for p in pairs]
        mb = [jnp.where(low_strict, gb[p][:C2], 0.0) for p in pairs]
        nb = [jnp.where(low_incl, gb[p][C2:], 0.0) for p in pairs]
        mk = [jnp.where(low_strict, gk[p][:C2], 0.0) for p in pairs]
        nk = [jnp.where(low_incl, gk[p][C2:], 0.0) for p in pairs]
        tinv = [eye + mb[p] for p in pairs]
        pw = [_dot(mb[p], mb[p]) for p in pairs]
        for lvl in range(n_sq):
            if lvl < n_sq - 1:
                both = [_dot(jnp.concatenate([tinv[p], pw[p]], axis=0), pw[p]) for p in pairs]
                tinv = [tinv[p] + both[p][:C2] for p in pairs]
                pw = [both[p][C2:] for p in pairs]
            else:
                tinv = [tinv[p] + _dot(tinv[p], pw[p]) for p in pairs]
        mkv = [_dot(mk[p], v_s[p]) for p in pairs]
        x = [_dot(tinv[p], jnp.concatenate([a_s[p], mkv[p].astype(BF16)], axis=1)) for p in pairs]
        s_old = [s_ref[p] for p in pairs]
        s_bf = [s.astype(BF16) for s in s_old]
        u = [_dot(x[p][:, :128], s_bf[p], _NT) + x[p][:, 128:] for p in pairs]
        y = [_dot(r_s[p], s_bf[p], _NT) + _dot(nb[p], u[p]) + _dot(nk[p], v_s[p]) for p in pairs]
        for p in pairs:
            uv = jnp.concatenate([u[p].astype(BF16), v_s[p]], axis=0)
            s_ref[p] = s_old[p] * p_tot[:, p * 128:(p + 1) * 128] + _dot(uv, bk_r[p], _TN)
            y_ref[rows, p * 128:(p + 1) * 128] = y[p][:C] + y[p][C:]
        return carry

    lax.fori_loop(0, tb // C, chunk, 0)

    @pl.when(tblk == pl.num_programs(1) - 1)
    def _():
        for p in range(n_pair):
            s = s_ref[p]
            sout_ref[2 * p] = s[:RW_HEAD, :RW_HEAD]
            sout_ref[2 * p + 1] = s[RW_HEAD:, RW_HEAD:]


def _wkv(r, lw, kk, ka, k2, v, s0, n_seq, tb, C):
    N = r.shape[0]
    nt = (N // n_seq) // tb
    kern = functools.partial(_wkv_kernel, tb=tb, C=C)
    in_spec = pl.BlockSpec((tb, RW_W), lambda b, t: (b * nt + t, 0))
    s_spec = pl.BlockSpec((None, RW_HEADS, RW_HEAD, RW_HEAD), lambda b, t: (b, 0, 0, 0))
    return pl.pallas_call(
        kern,
        grid=(n_seq, nt),
        in_specs=[in_spec] * 6 + [s_spec],
        out_specs=[in_spec, s_spec],
        out_shape=[jax.ShapeDtypeStruct((N, RW_W), F32),
                   jax.ShapeDtypeStruct((n_seq, RW_HEADS, RW_HEAD, RW_HEAD), F32)],
        scratch_shapes=[pltpu.VMEM((RW_HEADS // 2, 128, 128), F32)],
        compiler_params=_cparams(("arbitrary", "arbitrary")),
    )(r, lw, kk, ka, k2, v, s0)


def _rope_kernel(qk_ref, cos_ref, s1_ref, s2_ref, q_ref, k_ref):
    cos, s1, s2 = cos_ref[...], s1_ref[...], s2_ref[...]
    half = ROPE_DIM // 2
    for dst, base, scale in ((q_ref, 0, HEAD_QK ** -0.5), (k_ref, ATT_W, 1.0)):
        for cblk in range(ATT_W // 128):
            x = qk_ref[:, base + cblk * 128: base + (cblk + 1) * 128]
            up = pltpu.roll(x, 128 - half, axis=1)
            dn = pltpu.roll(x, half, axis=1)
            y = x * cos + up * s1 + dn * s2
            dst[:, cblk * 128:(cblk + 1) * 128] = y * scale if scale != 1.0 else y


def _rope(proj, tabs, tm):
    N = proj.shape[0]
    ntab = tabs[0].shape[0] // tm
    t_spec = pl.BlockSpec((tm, 128), lambda i: (i % ntab, 0))
    o_spec = pl.BlockSpec((tm, ATT_W), lambda i: (i, 0))
    return pl.pallas_call(
        _rope_kernel,
        grid=(N // tm,),
        in_specs=[pl.BlockSpec((tm, 2 * ATT_W), lambda i: (i, COL_Q // (2 * ATT_W))),
                  t_spec, t_spec, t_spec],
        out_specs=[o_spec, o_spec],
        out_shape=[jax.ShapeDtypeStruct((N, ATT_W), F32)] * 2,
        compiler_params=_cparams(("arbitrary",)),
    )(proj, *tabs)


def _rope_tables(pos):
    half = ROPE_DIM // 2
    inv = ROPE_THETA ** (-jnp.arange(0, ROPE_DIM, 2, dtype=F32) / ROPE_DIM)
    ang = pos.astype(F32)[:, None] * inv[None, :]
    cos, sin = jnp.cos(ang), jnp.sin(ang)
    n = pos.shape[0]
    one = jnp.ones((n, HEAD_QK - ROPE_DIM), F32)
    zero = jnp.zeros((n, HEAD_QK - ROPE_DIM), F32)
    zh = jnp.zeros((n, half), F32)
    c64 = jnp.concatenate([cos, cos, one], axis=1)
    s1 = jnp.concatenate([-sin, zh, zero], axis=1)
    s2 = jnp.concatenate([zh, sin, zero], axis=1)
    return tuple(jnp.concatenate([t, t], axis=1) for t in (c64, s1, s2))


def _lambda(lp_ref, lam_init):
    lp = lp_ref[...]
    d1 = jnp.sum(lp[0:1, :] * lp[1:2, :], axis=-1, keepdims=True)
    d2 = jnp.sum(lp[2:3, :] * lp[3:4, :], axis=-1, keepdims=True)
    return jnp.exp(d1) - jnp.exp(d2) + lam_init


def _flash_kernel(qi_ref, kj_ref, q_ref, k_ref, v_ref, lp_ref, sub_ref, o_ref,
                  qs_ref, m_ref, l_ref, acc_ref, *, tq, lam_init):
    t = pl.program_id(2)
    i = qi_ref[t]
    j = kj_ref[t]

    @pl.when(j == 0)
    def _():
        q = q_ref[...]
        lo = lax.broadcasted_iota(jnp.int32, q.shape, 1) < HEAD_QK
        qs_ref[...] = jnp.concatenate([jnp.where(lo, q, 0.0), jnp.where(lo, 0.0, q)],
                                      axis=0).astype(BF16)
        m_ref[...] = jnp.full(m_ref.shape, NEG_INF, F32)
        l_ref[...] = jnp.zeros(l_ref.shape, F32)
        acc_ref[...] = jnp.zeros(acc_ref.shape, F32)

    def step(diagonal):
        s = lax.dot_general(k_ref[...].astype(BF16), qs_ref[...], _NT,
                            preferred_element_type=F32)
        if diagonal:
            key = lax.broadcasted_iota(jnp.int32, s.shape, 0)
            qry = lax.broadcasted_iota(jnp.int32, s.shape, 1)
            s = jnp.where(key <= jnp.where(qry >= tq, qry - tq, qry), s, NEG_INF)
        m_old = m_ref[...]
        m_new = jnp.maximum(m_old, jnp.max(s, axis=0, keepdims=True))
        alpha = jnp.exp(m_old - m_new)
        p = jnp.exp(s - m_new)
        l_ref[...] = alpha * l_ref[...] + jnp.sum(p, axis=0, keepdims=True)
        pv = lax.dot_general(v_ref[...].astype(BF16), p.astype(BF16), _TN,
                             preferred_element_type=F32)
        acc_ref[...] = alpha * acc_ref[...] + pv
        m_ref[...] = m_new

    @pl.when(j < i)
    def _():
        step(False)

    @pl.when(j == i)
    def _():
        step(True)
        lam = _lambda(lp_ref, lam_init)
        on = acc_ref[...] / l_ref[...]
        o_t = on[:, :tq] - lam * on[:, tq:]
        ms = jnp.mean(o_t * o_t, axis=0, keepdims=True)
        o_t = o_t * lax.rsqrt(ms + SUBLN_EPS) * sub_ref[...] * (1.0 - lam_init)
        o_ref[...] = o_t.T


def _flash(qs, kr, proj, lam_p, subln, n_seq, tq, lam_init):
    N = qs.shape[0]
    nq = (N // n_seq) // tq
    kern = functools.partial(_flash_kernel, tq=tq, lam_init=lam_init)
    vcol = COL_V // HEAD_V
    pairs = [(i, j) for i in range(nq) for j in range(i + 1)]
    qi = jnp.asarray([p[0] for p in pairs], jnp.int32)
    kj = jnp.asarray([p[1] for p in pairs], jnp.int32)
    grid_spec = pltpu.PrefetchScalarGridSpec(
        num_scalar_prefetch=2,
        grid=(n_seq, ATT_HEADS, len(pairs)),
        in_specs=[pl.BlockSpec((tq, 128), lambda b, h, t, qi, kj: (b * nq + qi[t], h)),
                  pl.BlockSpec((tq, 128), lambda b, h, t, qi, kj: (b * nq + kj[t], h)),
                  pl.BlockSpec((tq, HEAD_V), lambda b, h, t, qi, kj: (b * nq + kj[t], vcol + h)),
                  pl.BlockSpec((8, HEAD_QK), lambda b, h, t, qi, kj: (0, 0)),
                  pl.BlockSpec((HEAD_V, 1), lambda b, h, t, qi, kj: (0, 0))],
        out_specs=pl.BlockSpec((tq, HEAD_V), lambda b, h, t, qi, kj: (b * nq + qi[t], h)),
        scratch_shapes=[pltpu.VMEM((2 * tq, 128), BF16), pltpu.VMEM((1, 2 * tq), F32),
                        pltpu.VMEM((1, 2 * tq), F32), pltpu.VMEM((HEAD_V, 2 * tq), F32)],
    )
    return pl.pallas_call(
        kern, grid_spec=grid_spec,
        out_shape=jax.ShapeDtypeStruct((N, ATT_W), F32),
        compiler_params=_cparams(("arbitrary",) * 3),
    )(qi, kj, qs, kr, proj, lam_p, subln.reshape(HEAD_V, 1))


def _decode_kernel(pt_ref, q_ref, ka_ref, kb_ref, va_ref, vb_ref, kn_ref, vn_ref, lp_ref, sub_ref,
                   o_ref, qrow, m_s, l_s, acc, *, ts, n_steps, lam_init):
    p = pl.program_id(1)
    nrow = 2 * ATT_HEADS * ts
    rows_per_head = 2 * ts

    @pl.when(p == 0)
    def _():
        row = lax.broadcasted_iota(jnp.int32, (nrow, ATT_W), 0)
        lane = lax.broadcasted_iota(jnp.int32, (nrow, ATT_W), 1)
        qrow[...] = jnp.where(lane // HEAD_QK == row // ts, q_ref[...], 0.0).astype(BF16)
        m_s[...] = jnp.full(m_s.shape, NEG_INF, F32)
        l_s[...] = jnp.zeros(l_s.shape, F32)
        acc[...] = jnp.zeros(acc.shape, F32)

    def attend(k_refs, v_refs, own):
        kt = jnp.concatenate([r[...] for r in k_refs], axis=1)
        s = jnp.dot(qrow[...], kt.astype(BF16), preferred_element_type=F32)
        if own:
            r2 = lax.broadcasted_iota(jnp.int32, s.shape, 0)
            c2 = lax.broadcasted_iota(jnp.int32, s.shape, 1)
            s = jnp.where(c2 <= r2 % ts, s, NEG_INF)
        m_new = jnp.maximum(m_s[...], jnp.max(s, axis=-1, keepdims=True))
        alpha = jnp.exp(m_s[...] - m_new)
        pr = jnp.exp(s - m_new)
        l_s[...] = alpha * l_s[...] + jnp.sum(pr, axis=-1, keepdims=True)
        pr = pr.astype(BF16)
        pv = []
        for h in range(ATT_HEADS):
            vh = jnp.concatenate([r[pl.ds(h, PAGE, stride=ATT_HEADS), :] for r in v_refs], axis=0)
            pv.append(jnp.dot(pr[h * rows_per_head:(h + 1) * rows_per_head], vh.astype(BF16),
                              preferred_element_type=F32))
        acc[...] = alpha * acc[...] + jnp.concatenate(pv, axis=0)
        m_s[...] = m_new

    @pl.when(p < n_steps - 1)
    def _():
        attend((ka_ref, kb_ref), (va_ref, vb_ref), False)

    @pl.when(p == n_steps - 1)
    def _():
        attend((kn_ref,), (vn_ref,), True)
        lam = _lambda(lp_ref, lam_init)
        on = acc[...] / l_s[...]
        outs = []
        for h in range(ATT_HEADS):
            blk = on[h * rows_per_head:(h + 1) * rows_per_head]
            oh = blk[:ts] - lam * blk[ts:]
            outs.append(_rms(oh, SUBLN_EPS) * sub_ref[...] * (1.0 - lam_init))
        o_ref[...] = jnp.concatenate(outs, axis=-1)


def _decode_attn(q_rows, kn_t, vn, cache_kt, cache_v, page_table, lam_p, subln, layer, ts, lam_init):
    bs, nrow, _ = q_rows.shape
    n_pages = page_table.shape[1]
    n_steps = n_pages // 2 + 1
    kern = functools.partial(_decode_kernel, ts=ts, n_steps=n_steps, lam_init=lam_init)

    def page(which):
        return lambda b, p, pt: (pt[b, jnp.minimum(2 * p + which, n_pages - 1)], layer, 0, 0)

    k_spec = lambda w: pl.BlockSpec((None, None, ATT_W, PAGE), page(w))
    v_spec = lambda w: pl.BlockSpec((None, None, PAGE * ATT_HEADS, HEAD_V), page(w))
    grid_spec = pltpu.PrefetchScalarGridSpec(
        num_scalar_prefetch=1,
        grid=(bs, n_steps),
        in_specs=[pl.BlockSpec((None, nrow, ATT_W), lambda b, p, pt: (b, 0, 0)),
                  k_spec(0), k_spec(1), v_spec(0), v_spec(1),
                  pl.BlockSpec((None, ATT_W, PAGE), lambda b, p, pt: (b, 0, 0)),
                  pl.BlockSpec((None, PAGE * ATT_HEADS, HEAD_V), lambda b, p, pt: (b, 0, 0)),
                  pl.BlockSpec((8, HEAD_QK), lambda b, p, pt: (0, 0)),
                  pl.BlockSpec((1, HEAD_V), lambda b, p, pt: (0, 0))],
        out_specs=pl.BlockSpec((None, ts, ATT_W), lambda b, p, pt: (b, 0, 0)),
        scratch_shapes=[pltpu.VMEM((nrow, ATT_W), BF16), pltpu.VMEM((nrow, 1), F32),
                        pltpu.VMEM((nrow, 1), F32), pltpu.VMEM((nrow, HEAD_V), F32)],
    )
    return pl.pallas_call(
        kern, grid_spec=grid_spec,
        out_shape=jax.ShapeDtypeStruct((bs, ts, ATT_W), F32),
        compiler_params=_cparams(("arbitrary", "arbitrary")),
    )(page_table, q_rows, cache_kt, cache_kt, cache_v, cache_v, kn_t, vn, lam_p, subln)


def _merge_kernel(x_ref, y_ref, bon_ref, g_ref, oatt_ref, gates_ref, gt_ref, ln_ref,
                  wrw_ref, watt_ref, wo_ref, o_ref):
    y = y_ref[...]
    n = y.shape[0]
    parts = []
    for h in range(RW_HEADS):
        yh = y[:, h * RW_HEAD:(h + 1) * RW_HEAD]
        mu = jnp.mean(yh, axis=-1, keepdims=True)
        d = yh - mu
        var = jnp.mean(d * d, axis=-1, keepdims=True)
        parts.append(d * lax.rsqrt(var + RW_GN_EPS))
    yn = jnp.concatenate(parts, axis=-1) * ln_ref[0:1, :] + ln_ref[1:2, :]
    out_rw = (yn + bon_ref[...]) * g_ref[...]
    y_rw = _dot(out_rw, wrw_ref[...])
    y_att = _dot(oatt_ref[...], watt_ref[...])
    gates = gates_ref[...]
    merged = (jax.nn.sigmoid(gates[:, :D_MODEL]) * y_rw
              + jax.nn.sigmoid(gates[:, D_MODEL:]) * y_att)
    o_ref[...] = x_ref[...] + gt_ref[...] * _dot(merged, wo_ref[...])


def _merge(x, y, bon, g, oatt, proj, gt, ln, wrw, watt, wo, tm):
    N = x.shape[0]
    tpg = (N // gt.shape[0]) // tm
    s512 = pl.BlockSpec((tm, RW_W), lambda i: (i, 0))
    full = lambda a: pl.BlockSpec(a.shape, lambda i: (0,) * a.ndim)
    return pl.pallas_call(
        _merge_kernel,
        grid=(N // tm,),
        in_specs=[pl.BlockSpec((tm, D_MODEL), lambda i: (i, 0)), s512, s512, s512, s512,
                  pl.BlockSpec((tm, 2 * D_MODEL), lambda i: (i, 0)),
                  _mod_spec(gt, tm, tpg), full(ln), full(wrw), full(watt), full(wo)],
        out_specs=pl.BlockSpec((tm, D_MODEL), lambda i: (i, 0)),
        out_shape=jax.ShapeDtypeStruct((N, D_MODEL), F32),
        compiler_params=_cparams(("arbitrary",)),
    )(x, y, bon, g, oatt, proj, gt, ln, wrw, watt, wo)


def _first_argmax(cols):
    best = cols[0]
    idx = jnp.zeros(best.shape, jnp.int32)
    for n, c in enumerate(cols[1:], start=1):
        take = c > best
        best = jnp.where(take, c, best)
        idx = jnp.where(take, n, idx)
    return best, idx


def _router_kernel(x_ref, g_ref, sc_ref, sh_ref, wr_ref, eb_ref, h_ref, gate_ref):
    y = _rms(x_ref[...], NORM_EPS) * g_ref[...]
    h = y * (1.0 + sc_ref[...]) + sh_ref[...]
    h_ref[...] = h.astype(BF16)
    logits = _dot_hi(h, wr_ref[...])
    s = jax.nn.sigmoid(logits)
    sel = s + eb_ref[...]
    sc_cols = [sel[:, e:e + 1] for e in range(N_EXPERTS)]
    s_cols = [s[:, e:e + 1] for e in range(N_EXPERTS)]
    grp = []
    for gi in range(N_GROUPS):
        cols = sc_cols[gi * EPG:(gi + 1) * EPG]
        m1, i1 = _first_argmax(cols)
        rest = [jnp.where(i1 == n, -jnp.inf, c) for n, c in enumerate(cols)]
        m2, _ = _first_argmax(rest)
        grp.append(m1 + m2)
    _, g_idx = _first_argmax(grp)
    pick = lambda cols_all, n: sum(jnp.where(g_idx == gi, cols_all[gi * EPG + n], 0.0)
                                   for gi in range(N_GROUPS))
    sel_g = [pick(sc_cols, n) for n in range(EPG)]
    s_g = [pick(s_cols, n) for n in range(EPG)]
    _, loc1 = _first_argmax(sel_g)
    _, loc2 = _first_argmax([jnp.where(loc1 == n, -jnp.inf, c) for n, c in enumerate(sel_g)])
    w_1 = sum(jnp.where(loc1 == n, s_g[n], 0.0) for n in range(EPG))
    w_2 = sum(jnp.where(loc2 == n, s_g[n], 0.0) for n in range(EPG))
    tot = w_1 + w_2
    w_1, w_2 = w_1 / tot, w_2 / tot
    e1 = g_idx * EPG + loc1
    e2 = g_idx * EPG + loc2
    lane = lax.broadcasted_iota(jnp.int32, logits.shape, 1)
    gate_ref[...] = jnp.where(lane == e1, w_1, 0.0) + jnp.where(lane == e2, w_2, 0.0)


def _router(x, g, sc, sh, wr, eb, tm):
    N, D = x.shape
    tpg = (N // sc.shape[0]) // tm
    return pl.pallas_call(
        _router_kernel,
        grid=(N // tm,),
        in_specs=[pl.BlockSpec((tm, D), lambda i: (i, 0)),
                  pl.BlockSpec((1, D), lambda i: (0, 0)),
                  _mod_spec(sc, tm, tpg), _mod_spec(sh, tm, tpg),
                  pl.BlockSpec((D, 128), lambda i: (0, 0)),
                  pl.BlockSpec((1, 128), lambda i: (0, 0))],
        out_specs=[pl.BlockSpec((tm, D), lambda i: (i, 0)),
                   pl.BlockSpec((tm, 128), lambda i: (i, 0))],
        out_shape=[jax.ShapeDtypeStruct((N, D), BF16), jax.ShapeDtypeStruct((N, 128), F32)],
        compiler_params=_cparams(("arbitrary",)),
    )(x, g, sc, sh, wr, eb)


def _moe_kernel(x_ref, h_ref, gate_ref, gt_ref, w1_ref, w3_ref, w2_ref, o_ref, acc_ref):
    e = pl.program_id(1)

    @pl.when(e == 0)
    def _():
        acc_ref[...] = jnp.zeros(acc_ref.shape, F32)

    h = h_ref[...]
    a1 = jnp.dot(h, w1_ref[...], preferred_element_type=F32)
    a3 = jnp.dot(h, w3_ref[...], preferred_element_type=F32)
    lane = lax.broadcasted_iota(jnp.int32, gate_ref.shape, 1)
    gcol = jnp.sum(jnp.where(lane == e, gate_ref[...], 0.0), axis=-1, keepdims=True)
    act = (a1 * jax.nn.sigmoid(a1)) * a3 * gcol
    acc_ref[...] += _dot(act, w2_ref[...])

    @pl.when(e == pl.num_programs(1) - 1)
    def _():
        o_ref[...] = x_ref[...] + gt_ref[...] * acc_ref[...]


def _moe(x, h, gate, gt, w1, w3, w2, tm):
    N, D = x.shape
    tpg = (N // gt.shape[0]) // tm
    return pl.pallas_call(
        _moe_kernel,
        grid=(N // tm, N_EXPERTS),
        in_specs=[pl.BlockSpec((tm, D), lambda i, e: (i, 0)),
                  pl.BlockSpec((tm, D), lambda i, e: (i, 0)),
                  pl.BlockSpec((tm, 128), lambda i, e: (i, 0)),
                  _mod_spec(gt, tm, tpg),
                  pl.BlockSpec((None, D, D_EXPERT), lambda i, e: (e, 0, 0)),
                  pl.BlockSpec((None, D, D_EXPERT), lambda i, e: (e, 0, 0)),
                  pl.BlockSpec((None, D_EXPERT, D), lambda i, e: (e, 0, 0))],
        out_specs=pl.BlockSpec((tm, D), lambda i, e: (i, 0)),
        out_shape=jax.ShapeDtypeStruct((N, D), F32),
        scratch_shapes=[pltpu.VMEM((tm, D), F32)],
        compiler_params=_cparams(("arbitrary", "arbitrary")),
    )(x, h, gate, gt, w1, w3, w2)


def _final_norm_kernel(x_ref, g_ref, o_ref):
    o_ref[...] = _rms(x_ref[...], NORM_EPS) * g_ref[...]


def _final_norm(x, g, tm):
    N, D = x.shape
    return pl.pallas_call(
        _final_norm_kernel,
        grid=(N // tm,),
        in_specs=[pl.BlockSpec((tm, D), lambda i: (i, 0)), pl.BlockSpec((1, D), lambda i: (0, 0))],
        out_specs=pl.BlockSpec((tm, D), lambda i: (i, 0)),
        out_shape=jax.ShapeDtypeStruct((N, D), F32),
        compiler_params=_cparams(("arbitrary",)),
    )(x, g)


def _group_layer(x, mods, lw, cfg, z0, s0, attend):
    sh1, sc1, gt1, sh2, sc2, gt2 = mods
    tm = cfg["tm"]
    proj = _nm_matmul(x, lw["norm1_g"], sc1, sh1, lw["w_in"], tm, cfg["tn_in"])
    r, lwd, kk, ka, k2, v, g, bon = _rw_prep(proj, z0, lw["mu"], lw["rw_vecs"], lw["wup"],
                                               cfg["tm_prep"], cfg["seq"])
    y, s_fin = _wkv(r, lwd, kk, ka, k2, v, s0, cfg["n_seq"], cfg["tb"], cfg["chunk"])
    qs, kr = _rope(proj, cfg["rope_tabs"], cfg["tm_prep"])
    oatt = attend(qs, kr, proj)
    x = _merge(x, y, bon, g, oatt, proj, gt1, lw["ln"], lw["w_rw_out"], lw["w_att_out"],
               lw["w_o"], tm)
    h2, gate = _router(x, lw["norm2_g"], sc2, sh2, lw["w_router"], lw["e_bias"], tm)
    x = _moe(x, h2, gate, gt2, lw["moe_w1"], lw["moe_w3"], lw["moe_w2"], cfg["tm_moe"])
    return x, proj, kr, s_fin


def kernel(x_prompt, x_sample, c_prompt, c_sample, cache_k, cache_v, page_table, state_shift, state_wkv, w_ada, b_ada, norm1_g, norm2_g, w_in, rw_mu, rw_w0, rw_w_up, rw_a0, rw_a_up, rw_g_up, rw_k_k, rw_k_a, rw_r_k, rw_ln_w, rw_ln_b, w_rw_out, att_lq1, att_lk1, att_lq2, att_lk2, att_subln, w_att_out, w_o, w_router, e_bias, moe_w1, moe_w3, moe_w2, normf_g):
    bp, tp, D = x_prompt.shape
    bs, ts, _ = x_sample.shape
    depth = w_in.shape[0]
    n_pages = page_table.shape[1]
    past_len = n_pages * PAGE
    n_p, n_s = bp * tp, bs * ts

    z_w, q_w, k_w, v_w, grw_w, gatt_w = jnp.split(
        w_in, [RW_PROJ, RW_PROJ + ATT_W, RW_PROJ + 2 * ATT_W, RW_PROJ + 3 * ATT_W,
               RW_PROJ + 3 * ATT_W + D], axis=-1)
    w_in_p = jnp.concatenate(
        [grw_w, gatt_w, q_w, k_w, v_w, z_w,
         jnp.zeros((depth, D, RW_PROJ_PAD - RW_PROJ), F32)], axis=-1).astype(BF16)
    mu_p = jnp.pad(rw_mu, ((0, 0), (0, RW_PROJ_PAD - RW_PROJ)))
    wup = jnp.zeros((depth, 3, LORA_PAD, RW_W), F32)
    wup = wup.at[:, 0, 0:DECAY_LORA].set(rw_w_up)
    wup = wup.at[:, 1, DECAY_LORA:DECAY_LORA + AAA_LORA].set(rw_a_up)
    wup = wup.at[:, 2, DECAY_LORA + AAA_LORA:DECAY_LORA + AAA_LORA + GATE_LORA].set(rw_g_up)
    wup = wup.astype(BF16)
    rw_vecs = jnp.stack([rw_w0, rw_a0, rw_k_k, rw_k_a, rw_r_k.reshape(depth, RW_W),
                         jnp.zeros_like(rw_w0), jnp.zeros_like(rw_w0), jnp.zeros_like(rw_w0)], axis=1)
    ln = jnp.stack([rw_ln_w, rw_ln_b], axis=1)
    lam_p = jnp.stack([att_lq1, att_lk1, att_lq2, att_lk2] + [jnp.zeros_like(att_lq1)] * 4, axis=1)
    w_router_p = jnp.pad(w_router, ((0, 0), (0, 128 - N_EXPERTS)))
    e_bias_p = jnp.pad(e_bias, (0, 128 - N_EXPERTS)).reshape(1, 128)
    w_rw_out_b, w_att_out_b, w_o_b = (w.astype(BF16) for w in (w_rw_out, w_att_out, w_o))
    moe_w1_b, moe_w3_b, moe_w2_b = (w.astype(BF16) for w in (moe_w1, moe_w3, moe_w2))

    n_c = bp + bs
    n_cp = -(-n_c // 8) * 8
    c_all = jnp.pad(jnp.concatenate([c_prompt, c_sample], axis=0), ((0, n_cp - n_c), (0, 0)))
    mod = _ada_mod(c_all, w_ada, b_ada)

    tm_p = min(512, tp)
    cfg_p = dict(tm=tm_p, tn_in=768, tm_prep=min(256, tp), seq=tp, n_seq=bp, tb=min(256, tp),
                 chunk=32, tm_moe=min(1024, tp), rope_tabs=_rope_tables(jnp.arange(tp)))
    cfg_s = dict(tm=n_s, tn_in=768, tm_prep=n_s, seq=ts, n_seq=bs, tb=8, chunk=8, tm_moe=n_s,
                 rope_tabs=_rope_tables(jnp.tile(past_len + jnp.arange(ts), bs)))

    cache_kt = jnp.transpose(cache_k, (0, 1, 3, 4, 5, 2)).reshape(cache_k.shape[0], depth, ATT_W, PAGE)
    cache_v2 = cache_v.reshape(cache_v.shape[0], depth, PAGE * ATT_HEADS, HEAD_V)
    zp0 = jnp.zeros((bp, 1, RW_PROJ_PAD), F32)
    sp0 = jnp.zeros((bp, RW_HEADS, RW_HEAD, RW_HEAD), F32)

    xp = x_prompt.reshape(n_p, D)
    xs = x_sample.reshape(n_s, D)
    outs = {k: [] for k in ("kp", "vp", "zp", "sp", "ks", "vs", "zs", "ss")}
    for l in range(depth):
        lam_init = 0.8 - 0.6 * math.exp(-0.3 * l)
        lw = dict(norm1_g=norm1_g[l].reshape(1, D), norm2_g=norm2_g[l].reshape(1, D),
                  w_in=w_in_p[l], mu=mu_p[l].reshape(1, RW_PROJ_PAD), rw_vecs=rw_vecs[l],
                  wup=wup[l], ln=ln[l], w_rw_out=w_rw_out_b[l], w_att_out=w_att_out_b[l],
                  w_o=w_o_b[l], w_router=w_router_p, e_bias=e_bias_p,
                  moe_w1=moe_w1_b[l], moe_w3=moe_w3_b[l], moe_w2=moe_w2_b[l])
        subln = att_subln[l].reshape(1, HEAD_V)
        mods_p = [m.reshape(bp, 1, D) for m in jnp.split(mod[l, :bp], 6, axis=-1)]
        mods_s = [jnp.repeat(m, ts, axis=0).reshape(1, n_s, D)
                  for m in jnp.split(mod[l, bp:n_c], 6, axis=-1)]

        attend_p = lambda qs, kr, proj: _flash(qs, kr, proj, lam_p[l], subln, bp,
                                               min(512, tp), lam_init)
        xp, proj_p, kr_p, s_fin = _group_layer(xp, mods_p, lw, cfg_p, zp0, sp0, attend_p)
        outs["kp"].append(kr_p.reshape(bp, tp, ATT_HEADS, 2, HEAD_QK))
        outs["vp"].append(proj_p[:, COL_V:COL_V + ATT_W].reshape(bp, tp, ATT_HEADS, HEAD_V))
        outs["zp"].append(proj_p.reshape(bp, tp, IN_W_PAD)[:, -1, COL_Z:COL_Z + RW_PROJ])
        outs["sp"].append(s_fin)

        def attend_s(qs, kr, proj):
            q_rows = jnp.tile(qs.reshape(bs, ts, ATT_W), (1, 2 * ATT_HEADS, 1))
            kn_t = jnp.pad(jnp.swapaxes(kr.reshape(bs, ts, ATT_W), 1, 2),
                           ((0, 0), (0, 0), (0, PAGE - ts)))
            vn = jnp.pad(proj[:, COL_V:COL_V + ATT_W].reshape(bs, ts * ATT_HEADS, HEAD_V),
                         ((0, 0), (0, (PAGE - ts) * ATT_HEADS), (0, 0)))
            return _decode_attn(q_rows, kn_t, vn, cache_kt, cache_v2, page_table, lam_p[l], subln, l,
                                ts, lam_init).reshape(n_s, ATT_W)

        z0_s = jnp.pad(jnp.repeat(state_shift[l], ts, axis=0), ((0, 0), (0, RW_PROJ_PAD - RW_PROJ)))
        xs, proj_s, kr_s, s_fin = _sample_layer(xs, mods_s, lw, cfg_s, z0_s, state_wkv[l], attend_s, bs, ts)
        outs["ks"].append(kr_s.reshape(bs, ts, ATT_HEADS, 2, HEAD_QK))
        outs["vs"].append(proj_s[:, COL_V:COL_V + ATT_W].reshape(bs, ts, ATT_HEADS, HEAD_V))
        outs["zs"].append(proj_s.reshape(bs, ts, IN_W_PAD)[:, -1, COL_Z:COL_Z + RW_PROJ])
        outs["ss"].append(s_fin)

    y_prompt = _final_norm(xp, normf_g.reshape(1, D), tm_p).reshape(bp, tp, D)
    y_sample = _final_norm(xs, normf_g.reshape(1, D), n_s).reshape(bs, ts, D)
    return (y_prompt, y_sample,
            jnp.stack(outs["kp"], axis=1), jnp.stack(outs["vp"], axis=1),
            jnp.stack(outs["zp"], axis=0), jnp.stack(outs["sp"], axis=0),
            jnp.stack(outs["ks"], axis=1), jnp.stack(outs["vs"], axis=1),
            jnp.stack(outs["zs"], axis=0), jnp.stack(outs["ss"], axis=0))


def _sample_layer(x, mods, lw, cfg, z0, s0, attend, bs, ts):
    sh1, sc1, gt1, sh2, sc2, gt2 = mods
    tm = cfg["tm"]
    proj = _nm_matmul(x, lw["norm1_g"], sc1, sh1, lw["w_in"], tm, cfg["tn_in"])
    prep = _rw_prep(proj, z0, lw["mu"], lw["rw_vecs"], lw["wup"], cfg["tm_prep"], cfg["seq"])
    r, lwd, kk, ka, k2, v, g, bon = prep
    pad8 = lambda a: jnp.pad(a.reshape(bs, ts, RW_W), ((0, 0), (0, 8 - ts), (0, 0))).reshape(bs * 8, RW_W)
    y8, s_fin = _wkv(*(pad8(a) for a in (r, lwd, kk, ka, k2, v)), s0, bs, 8, 8)
    y = y8.reshape(bs, 8, RW_W)[:, :ts].reshape(bs * ts, RW_W)
    qs, kr = _rope(proj, cfg["rope_tabs"], cfg["tm_prep"])
    oatt = attend(qs, kr, proj)
    x = _merge(x, y, bon, g, oatt, proj, gt1, lw["ln"], lw["w_rw_out"], lw["w_att_out"],
               lw["w_o"], tm)
    h2, gate = _router(x, lw["norm2_g"], sc2, sh2, lw["w_router"], lw["e_bias"], tm)
    x = _moe(x, h2, gate, gt2, lw["moe_w1"], lw["moe_w3"], lw["moe_w2"], cfg["tm_moe"])
    return x, proj, kr, s_fin
```

```python
import functools
import math

import jax
import jax.numpy as jnp
from jax import lax
from jax.experimental import pallas as pl
from jax.experimental.pallas import tpu as pltpu

F32 = jnp.float32
BF16 = jnp.bfloat16

D_MODEL = 1024
RW_HEAD = 64
RW_HEADS = 8
RW_W = RW_HEADS * RW_HEAD
DECAY_LORA = 32
AAA_LORA = 32
GATE_LORA = 96
RW_PROJ = 3 * RW_W + DECAY_LORA + AAA_LORA + GATE_LORA
RW_PROJ_PAD = 1792
LORA_PAD = RW_PROJ_PAD - 3 * RW_W
RW_GN_EPS = 64e-5
ATT_HEADS = 4
HEAD_QK = 64
HEAD_V = 128
ATT_W = 512
ROPE_DIM = HEAD_QK // 4
ROPE_THETA = 500000.0
SUBLN_EPS = 1e-5
NEG_INF = -1e30
N_EXPERTS = 16
N_GROUPS = 4
EPG = 4
D_EXPERT = 512
NORM_EPS = 1e-6
PAGE = 128
DECODE_PAGES_PER_STEP = 4
FLASH_HEADS_PER_STEP = 4
WKV_CHUNK = 64

COL_GATE = 0
COL_Q = 2048
COL_V = 3072
COL_Z = 3584
IN_W_PAD = COL_Z + RW_PROJ_PAD

VMEM_LIMIT = 56 * 1024 * 1024

_NN = (((1,), (0,)), ((), ()))
_NT = (((1,), (1,)), ((), ()))
_TN = (((0,), (0,)), ((), ()))


def _dot(a, b, dims=_NN):
    return lax.dot_general(a.astype(BF16), b.astype(BF16), dims, preferred_element_type=F32)


def _dot_hi(a, b, dims=_NN):
    return lax.dot_general(a, b, dims, preferred_element_type=F32,
                           precision=lax.Precision.HIGHEST)


def _cparams(sem):
    return pltpu.CompilerParams(dimension_semantics=sem, vmem_limit_bytes=VMEM_LIMIT)


def _rms(x, eps):
    return x * lax.rsqrt(jnp.mean(x * x, axis=-1, keepdims=True) + eps)


def _ada_kernel(c_ref, w_ref, b_ref, o_ref):
    c = c_ref[...]
    sc = c * jax.nn.sigmoid(c)
    o_ref[...] = _dot(sc, w_ref[...]) + b_ref[...]


def _ada_mod(c_all, w_ada, b_ada):
    L, D, N6 = w_ada.shape
    M = c_all.shape[0]
    tn = 1536
    return pl.pallas_call(
        _ada_kernel,
        grid=(L, N6 // tn),
        in_specs=[pl.BlockSpec((M, D), lambda l, j: (0, 0)),
                  pl.BlockSpec((None, D, tn), lambda l, j: (l, 0, j)),
                  pl.BlockSpec((None, 1, tn), lambda l, j: (l, 0, j))],
        out_specs=pl.BlockSpec((None, M, tn), lambda l, j: (l, 0, j)),
        out_shape=jax.ShapeDtypeStruct((L, M, N6), F32),
        compiler_params=_cparams(("arbitrary", "arbitrary")),
    )(c_all, w_ada, b_ada.reshape(L, 1, N6))


def _mod_spec(m, tm, tiles_per_group):
    r = m.shape[1]
    return pl.BlockSpec((None, r, m.shape[2]), lambda i, *_: (i // tiles_per_group, 0, 0))


def _nm_matmul_kernel(x_ref, g_ref, sc_ref, sh_ref, w_ref, o_ref, h_ref):
    @pl.when(pl.program_id(1) == 0)
    def _():
        y = _rms(x_ref[...], NORM_EPS) * g_ref[...]
        h_ref[...] = (y * (1.0 + sc_ref[...]) + sh_ref[...]).astype(BF16)

    o_ref[...] = jnp.dot(h_ref[...], w_ref[...], preferred_element_type=F32)


def _nm_matmul(x, g, sc, sh, w, tm, tn):
    N, D = x.shape
    n_out = w.shape[1]
    tpg = (N // sc.shape[0]) // tm
    return pl.pallas_call(
        _nm_matmul_kernel,
        grid=(N // tm, n_out // tn),
        in_specs=[pl.BlockSpec((tm, D), lambda i, j: (i, 0)),
                  pl.BlockSpec((1, D), lambda i, j: (0, 0)),
                  _mod_spec(sc, tm, tpg), _mod_spec(sh, tm, tpg),
                  pl.BlockSpec((D, tn), lambda i, j: (0, j))],
        out_specs=pl.BlockSpec((tm, tn), lambda i, j: (i, j)),
        out_shape=jax.ShapeDtypeStruct((N, n_out), F32),
        scratch_shapes=[pltpu.VMEM((tm, D), BF16)],
        compiler_params=_cparams(("arbitrary", "arbitrary")),
    )(x, g, sc, sh, w)


def _head_sum(x):
    parts = []
    for h in range(RW_HEADS):
        s = jnp.sum(x[:, h * RW_HEAD:(h + 1) * RW_HEAD], axis=-1, keepdims=True)
        parts.append(jnp.broadcast_to(s, (x.shape[0], RW_HEAD)))
    return jnp.concatenate(parts, axis=-1)


def _rw_prep_kernel(z_ref, z0_ref, mu_ref, vec_ref, wup_ref,
                    r_ref, lw_ref, kk_ref, ka_ref, k2_ref, v_ref, g_ref, bon_ref,
                    carry_ref, *, tm, seq_len):
    i = pl.program_id(0)
    z = z_ref[...]
    rolled = pltpu.roll(z, 1, axis=0)
    row = lax.broadcasted_iota(jnp.int32, z.shape, 0)
    if seq_len >= tm:
        tiles_per_seq = seq_len // tm
        first = jnp.where(i % tiles_per_seq == 0, z0_ref[...], carry_ref[0:1, :])
        z_prev = jnp.where(row == 0, first, rolled)
        carry_ref[0:1, :] = z[tm - 1:tm, :]
    else:
        z_prev = jnp.where(row % seq_len == 0, z0_ref[...], rolled)
    zs = z + (z_prev - z) * mu_ref[...]
    r = zs[:, 0:RW_W]
    k = zs[:, RW_W:2 * RW_W]
    v = zs[:, 2 * RW_W:3 * RW_W]
    tail = zs[:, 3 * RW_W:]
    w0, a0, k_k, k_a, r_k = (vec_ref[j:j + 1, :] for j in range(5))
    dw = _dot(jnp.tanh(tail), wup_ref[0])
    da = _dot(tail, wup_ref[1])
    g = _dot(jax.nn.sigmoid(tail), wup_ref[2])
    t = -(w0 + dw)
    softplus = jnp.maximum(t, 0.0) + jnp.log1p(jnp.exp(-jnp.abs(t)))
    lw = -jnp.exp(-softplus - 0.5)
    a = jax.nn.sigmoid(a0 + da)
    kk = k * k_k
    kk = kk / jnp.maximum(jnp.sqrt(_head_sum(kk * kk)), 1e-12)
    k2 = k * (1.0 + (a - 1.0) * k_a)
    r_ref[...] = r
    lw_ref[...] = lw
    kk_ref[...] = kk
    ka_ref[...] = kk * a
    k2_ref[...] = k2
    v_ref[...] = v
    g_ref[...] = g
    bon_ref[...] = _head_sum(r * k2 * r_k) * v


def _rw_prep(proj, z0, mu, vecs, wup, tm, seq_len):
    N = proj.shape[0]
    kern = functools.partial(_rw_prep_kernel, tm=tm, seq_len=seq_len)
    if seq_len >= tm:
        tps = seq_len // tm
        z0_spec = pl.BlockSpec((None, 1, RW_PROJ_PAD), lambda i: (i // tps, 0, 0))
    else:
        z0_spec = pl.BlockSpec((tm, RW_PROJ_PAD), lambda i: (i, 0))
    o_spec = pl.BlockSpec((tm, RW_W), lambda i: (i, 0))
    return pl.pallas_call(
        kern,
        grid=(N // tm,),
        in_specs=[pl.BlockSpec((tm, RW_PROJ_PAD), lambda i: (i, COL_Z // RW_PROJ_PAD)),
                  z0_spec,
                  pl.BlockSpec((1, RW_PROJ_PAD), lambda i: (0, 0)),
                  pl.BlockSpec((8, RW_W), lambda i: (0, 0)),
                  pl.BlockSpec((3, LORA_PAD, RW_W), lambda i: (0, 0, 0))],
        out_specs=[o_spec] * 8,
        out_shape=[jax.ShapeDtypeStruct((N, RW_W), F32)] * 8,
        scratch_shapes=[pltpu.VMEM((8, RW_PROJ_PAD), F32)],
        compiler_params=_cparams(("arbitrary",)),
    )(proj, z0, mu, vecs, wup)


def _wkv_kernel(r_ref, lw_ref, kk_ref, ka_ref, k2_ref, v_ref, s0_ref, y_ref, sout_ref,
                s_ref, *, tb, C):
    tblk = pl.program_id(1)
    n_pair = RW_HEADS // 2
    zero64 = jnp.zeros((RW_HEAD, RW_HEAD), F32)

    @pl.when(tblk == 0)
    def _():
        for p in range(n_pair):
            s_ref[p] = jnp.concatenate(
                [jnp.concatenate([s0_ref[2 * p], zero64], axis=1),
                 jnp.concatenate([zero64, s0_ref[2 * p + 1]], axis=1)], axis=0)

    C2 = 2 * C
    ri = lax.broadcasted_iota(jnp.int32, (C, C), 0)
    ci = lax.broadcasted_iota(jnp.int32, (C, C), 1)
    tri = (ri >= ci).astype(F32)
    r2 = lax.broadcasted_iota(jnp.int32, (C2, C2), 0)
    c2 = lax.broadcasted_iota(jnp.int32, (C2, C2), 1)
    low_incl = r2 >= c2
    low_strict = r2 > c2
    eye = (r2 == c2).astype(F32)
    lo_lanes = lax.broadcasted_iota(jnp.int32, (C, 128), 1) < RW_HEAD
    n_sq = int(math.log2(C)) - 1
    pairs = range(n_pair)

    def stack(x, p):
        xp = x[:, p * 128:(p + 1) * 128]
        return jnp.concatenate([jnp.where(lo_lanes, xp, 0.0), jnp.where(lo_lanes, 0.0, xp)], axis=0)

    def chunk(c, carry):
        rows = pl.ds(pl.multiple_of(c * C, C), C)
        r = r_ref[rows, :]
        lw = lw_ref[rows, :]
        kk = kk_ref[rows, :]
        ka = ka_ref[rows, :]
        k2 = k2_ref[rows, :]
        v = v_ref[rows, :]
        cs = _dot_hi(tri, lw)
        tot = cs[C - 1:C, :]
        p_inv = jnp.exp(-cs)
        p_rem = jnp.exp(tot - cs)
        p_tot = jnp.exp(tot)
        a_s = [stack(-kk * jnp.exp(cs - lw), p).astype(BF16) for p in pairs]
        r_s = [stack(r * jnp.exp(cs), p).astype(BF16) for p in pairs]
        b_s = [stack(ka * p_inv, p).astype(BF16) for p in pairs]
        k_s = [stack(k2 * p_inv, p).astype(BF16) for p in pairs]
        bk_r = [jnp.concatenate([stack(ka * p_rem, p), stack(k2 * p_rem, p)], axis=0).astype(BF16)
                for p in pairs]
        v_s = [stack(v, p).astype(BF16) for p in pairs]
        ar = [jnp.concatenate([a_s[p], r_s[p]], axis=0) for p in pairs]
        gb = [_dot(ar[p], b_s[p], _NT) for p in pairs]
        gk = [_dot(ar[p], k_s[p], _NT) for p in pairs]
        mb = [jnp.where(low_strict, gb[p][:C2], 0.0) for p in pairs]
        nb = [jnp.where(low_incl, gb[p][C2:], 0.0) for p in pairs]
        mk = [jnp.where(low_strict, gk[p][:C2], 0.0) for p in pairs]
        nk = [jnp.where(low_incl, gk[p][C2:], 0.0) for p in pairs]
        tinv = [eye + mb[p] for p in pairs]
        pw = [_dot(mb[p], mb[p]) for p in pairs]
        for lvl in range(n_sq):
            if lvl < n_sq - 1:
                both = [_dot(jnp.concatenate([tinv[p], pw[p]], axis=0), pw[p]) for p in pairs]
                tinv = [tinv[p] + both[p][:C2] for p in pairs]
                pw = [both[p][C2:] for p in pairs]
            else:
                tinv = [tinv[p] + _dot(tinv[p], pw[p]) for p in pairs]
        mkv = [_dot(mk[p], v_s[p]) for p in pairs]
        x = [_dot(tinv[p], jnp.concatenate([a_s[p], mkv[p].astype(BF16)], axis=1)) for p in pairs]
        s_old = [s_ref[p] for p in pairs]
        s_bf = [s.astype(BF16) for s in s_old]
        u = [_dot(x[p][:, :128], s_bf[p], _NT) + x[p][:, 128:] for p in pairs]
        y = [_dot(r_s[p], s_bf[p], _NT) + _dot(nb[p], u[p]) + _dot(nk[p], v_s[p]) for p in pairs]
        for p in pairs:
            uv = jnp.concatenate([u[p].astype(BF16), v_s[p]], axis=0)
            s_ref[p] = s_old[p] * p_tot[:, p * 128:(p + 1) * 128] + _dot(uv, bk_r[p], _TN)
            y_ref[rows, p * 128:(p + 1) * 128] = y[p][:C] + y[p][C:]
        return carry

    lax.fori_loop(0, tb // C, chunk, 0)

    @pl.when(tblk == pl.num_programs(1) - 1)
    def _():
        for p in range(n_pair):
            s = s_ref[p]
            sout_ref[2 * p] = s[:RW_HEAD, :RW_HEAD]
            sout_ref[2 * p + 1] = s[RW_HEAD:, RW_HEAD:]


def _wkv(r, lw, kk, ka, k2, v, s0, n_seq, tb, C):
    N = r.shape[0]
    nt = (N // n_seq) // tb
    kern = functools.partial(_wkv_kernel, tb=tb, C=C)
    in_spec = pl.BlockSpec((tb, RW_W), lambda b, t: (b * nt + t, 0))
    s_spec = pl.BlockSpec((None, RW_HEADS, RW_HEAD, RW_HEAD), lambda b, t: (b, 0, 0, 0))
    return pl.pallas_call(
        kern,
        grid=(n_seq, nt),
        in_specs=[in_spec] * 6 + [s_spec],
        out_specs=[in_spec, s_spec],
        out_shape=[jax.ShapeDtypeStruct((N, RW_W), F32),
                   jax.ShapeDtypeStruct((n_seq, RW_HEADS, RW_HEAD, RW_HEAD), F32)],
        scratch_shapes=[pltpu.VMEM((RW_HEADS // 2, 128, 128), F32)],
        compiler_params=_cparams(("arbitrary", "arbitrary")),
    )(r, lw, kk, ka, k2, v, s0)


def _rope_kernel(qk_ref, cos_ref, s1_ref, s2_ref, q_ref, k_ref):
    cos, s1, s2 = cos_ref[...], s1_ref[...], s2_ref[...]
    half = ROPE_DIM // 2
    for dst, base, scale in ((q_ref, 0, HEAD_QK ** -0.5), (k_ref, ATT_W, 1.0)):
        for cblk in range(ATT_W // 128):
            x = qk_ref[:, base + cblk * 128: base + (cblk + 1) * 128]
            up = pltpu.roll(x, 128 - half, axis=1)
            dn = pltpu.roll(x, half, axis=1)
            y = x * cos + up * s1 + dn * s2
            dst[:, cblk * 128:(cblk + 1) * 128] = y * scale if scale != 1.0 else y


def _rope(proj, tabs, tm):
    N = proj.shape[0]
    ntab = tabs[0].shape[0] // tm
    t_spec = pl.BlockSpec((tm, 128), lambda i: (i % ntab, 0))
    o_spec = pl.BlockSpec((tm, ATT_W), lambda i: (i, 0))
    return pl.pallas_call(
        _rope_kernel,
        grid=(N // tm,),
        in_specs=[pl.BlockSpec((tm, 2 * ATT_W), lambda i: (i, COL_Q // (2 * ATT_W))),
                  t_spec, t_spec, t_spec],
        out_specs=[o_spec, o_spec],
        out_shape=[jax.ShapeDtypeStruct((N, ATT_W), F32)] * 2,
        compiler_params=_cparams(("arbitrary",)),
    )(proj, *tabs)


def _rope_tables(pos):
    half = ROPE_DIM // 2
    inv = ROPE_THETA ** (-jnp.arange(0, ROPE_DIM, 2, dtype=F32) / ROPE_DIM)
    ang = pos.astype(F32)[:, None] * inv[None, :]
    cos, sin = jnp.cos(ang), jnp.sin(ang)
    n = pos.shape[0]
    one = jnp.ones((n, HEAD_QK - ROPE_DIM), F32)
    zero = jnp.zeros((n, HEAD_QK - ROPE_DIM), F32)
    zh = jnp.zeros((n, half), F32)
    c64 = jnp.concatenate([cos, cos, one], axis=1)
    s1 = jnp.concatenate([-sin, zh, zero], axis=1)
    s2 = jnp.concatenate([zh, sin, zero], axis=1)
    return tuple(jnp.concatenate([t, t], axis=1) for t in (c64, s1, s2))


def _lambda(lp_ref, lam_init):
    lp = lp_ref[...]
    d1 = jnp.sum(lp[0:1, :] * lp[1:2, :], axis=-1, keepdims=True)
    d2 = jnp.sum(lp[2:3, :] * lp[3:4, :], axis=-1, keepdims=True)
    return jnp.exp(d1) - jnp.exp(d2) + lam_init


def _flash_kernel(qi_ref, kj_ref, q_ref, k_ref, v_ref, lp_ref, sub_ref, o_ref,
                  qs_ref, m_ref, l_ref, acc_ref, *, tq, hps, lam_init):
    t = pl.program_id(2)
    i = qi_ref[t]
    j = kj_ref[t]
    heads = range(hps)
    cols = lambda hh: slice(hh * 128, (hh + 1) * 128)

    @pl.when(j == 0)
    def _():
        q = q_ref[...] * math.log2(math.e)
        lo = lax.broadcasted_iota(jnp.int32, (tq, 128), 1) < HEAD_QK
        for hh in heads:
            qh = q[:, cols(hh)]
            qs_ref[hh] = jnp.concatenate([jnp.where(lo, qh, 0.0), jnp.where(lo, 0.0, qh)],
                                         axis=0).astype(BF16)
        m_ref[...] = jnp.full(m_ref.shape, NEG_INF, F32)
        l_ref[...] = jnp.zeros(l_ref.shape, F32)
        acc_ref[...] = jnp.zeros(acc_ref.shape, F32)

    def step(diagonal):
        k = k_ref[...].astype(BF16)
        v = v_ref[...].astype(BF16)
        s = [lax.dot_general(k[:, cols(hh)], qs_ref[hh], _NT, preferred_element_type=F32)
             for hh in heads]
        if diagonal:
            key = lax.broadcasted_iota(jnp.int32, s[0].shape, 0)
            qry = lax.broadcasted_iota(jnp.int32, s[0].shape, 1)
            keep = key <= jnp.where(qry >= tq, qry - tq, qry)
            s = [jnp.where(keep, s[hh], NEG_INF) for hh in heads]
        m_old = [m_ref[hh] for hh in heads]
        m_new = [jnp.maximum(m_old[hh], jnp.max(s[hh], axis=0, keepdims=True)) for hh in heads]
        alpha = [jnp.exp2(m_old[hh] - m_new[hh]) for hh in heads]
        p = [jnp.exp2(s[hh] - m_new[hh]) for hh in heads]
        pv = [lax.dot_general(v[:, cols(hh)], p[hh].astype(BF16), _TN, preferred_element_type=F32)
              for hh in heads]
        for hh in heads:
            l_ref[hh] = alpha[hh] * l_ref[hh] + jnp.sum(p[hh], axis=0, keepdims=True)
            acc_ref[hh] = alpha[hh] * acc_ref[hh] + pv[hh]
            m_ref[hh] = m_new[hh]

    @pl.when(j < i)
    def _():
        step(False)

    @pl.when(j == i)
    def _():
        step(True)
        lam = _lambda(lp_ref, lam_init)
        for hh in heads:
            on = acc_ref[hh] / l_ref[hh]
            o_t = on[:, :tq] - lam * on[:, tq:]
            ms = jnp.mean(o_t * o_t, axis=0, keepdims=True)
            o_t = o_t * lax.rsqrt(ms + SUBLN_EPS) * sub_ref[...] * (1.0 - lam_init)
            o_ref[:, cols(hh)] = o_t.T


def _flash(qs, kr, proj, lam_p, subln, n_seq, tq, lam_init):
    N = qs.shape[0]
    nq = (N // n_seq) // tq
    hps = FLASH_HEADS_PER_STEP
    wid = hps * 128
    kern = functools.partial(_flash_kernel, tq=tq, hps=hps, lam_init=lam_init)
    vcol = COL_V // wid
    pairs = [(i, j) for i in range(nq) for j in range(i + 1)]
    qi = jnp.asarray([p[0] for p in pairs], jnp.int32)
    kj = jnp.asarray([p[1] for p in pairs], jnp.int32)
    grid_spec = pltpu.PrefetchScalarGridSpec(
        num_scalar_prefetch=2,
        grid=(n_seq, ATT_HEADS // hps, len(pairs)),
        in_specs=[pl.BlockSpec((tq, wid), lambda b, h, t, qi, kj: (b * nq + qi[t], h)),
                  pl.BlockSpec((tq, wid), lambda b, h, t, qi, kj: (b * nq + kj[t], h)),
                  pl.BlockSpec((tq, wid), lambda b, h, t, qi, kj: (b * nq + kj[t], vcol + h)),
                  pl.BlockSpec((8, HEAD_QK), lambda b, h, t, qi, kj: (0, 0)),
                  pl.BlockSpec((HEAD_V, 1), lambda b, h, t, qi, kj: (0, 0))],
        out_specs=pl.BlockSpec((tq, wid), lambda b, h, t, qi, kj: (b * nq + qi[t], h)),
        scratch_shapes=[pltpu.VMEM((hps, 2 * tq, 128), BF16), pltpu.VMEM((hps, 1, 2 * tq), F32),
                        pltpu.VMEM((hps, 1, 2 * tq), F32), pltpu.VMEM((hps, HEAD_V, 2 * tq), F32)],
    )
    return pl.pallas_call(
        kern, grid_spec=grid_spec,
        out_shape=jax.ShapeDtypeStruct((N, ATT_W), F32),
        compiler_params=_cparams(("arbitrary",) * 3),
    )(qi, kj, qs, kr, proj, lam_p, subln.reshape(HEAD_V, 1))


def _decode_kernel(pt_ref, q_ref, *refs, ts, n_steps, pps, lam_init):
    k_refs, v_refs = refs[:pps], refs[pps:2 * pps]
    kn_ref, vn_ref, lp_ref, sub_ref, o_ref, qrow, m_s, l_s, acc = refs[2 * pps:]
    p = pl.program_id(1)
    nrow = 2 * ATT_HEADS * ts
    rows_per_head = 2 * ts

    @pl.when(p == 0)
    def _():
        row = lax.broadcasted_iota(jnp.int32, (nrow, ATT_W), 0)
        lane = lax.broadcasted_iota(jnp.int32, (nrow, ATT_W), 1)
        qrow[...] = jnp.where(lane // HEAD_QK == row // ts, q_ref[...], 0.0).astype(BF16)
        m_s[...] = jnp.full(m_s.shape, NEG_INF, F32)
        l_s[...] = jnp.zeros(l_s.shape, F32)
        acc[...] = jnp.zeros(acc.shape, F32)

    def attend(k_refs, v_refs, own):
        kt = jnp.concatenate([r[...] for r in k_refs], axis=1)
        s = jnp.dot(qrow[...], kt.astype(BF16), preferred_element_type=F32)
        if own:
            r2 = lax.broadcasted_iota(jnp.int32, s.shape, 0)
            c2 = lax.broadcasted_iota(jnp.int32, s.shape, 1)
            s = jnp.where(c2 <= r2 % ts, s, NEG_INF)
        m_new = jnp.maximum(m_s[...], jnp.max(s, axis=-1, keepdims=True))
        alpha = jnp.exp(m_s[...] - m_new)
        pr = jnp.exp(s - m_new)
        l_s[...] = alpha * l_s[...] + jnp.sum(pr, axis=-1, keepdims=True)
        pr = pr.astype(BF16)
        pv = []
        for h in range(ATT_HEADS):
            vh = jnp.concatenate([r[pl.ds(h, PAGE, stride=ATT_HEADS), :] for r in v_refs], axis=0)
            pv.append(jnp.dot(pr[h * rows_per_head:(h + 1) * rows_per_head], vh.astype(BF16),
                              preferred_element_type=F32))
        acc[...] = alpha * acc[...] + jnp.concatenate(pv, axis=0)
        m_s[...] = m_new

    @pl.when(p < n_steps - 1)
    def _():
        attend(k_refs, v_refs, False)

    @pl.when(p == n_steps - 1)
    def _():
        attend((kn_ref,), (vn_ref,), True)
        lam = _lambda(lp_ref, lam_init)
        on = acc[...] / l_s[...]
        outs = []
        for h in range(ATT_HEADS):
            blk = on[h * rows_per_head:(h + 1) * rows_per_head]
            oh = blk[:ts] - lam * blk[ts:]
            outs.append(_rms(oh, SUBLN_EPS) * sub_ref[...] * (1.0 - lam_init))
        o_ref[...] = jnp.concatenate(outs, axis=-1)


def _decode_attn(q_rows, kn_t, vn, cache_kt, cache_v, page_table, lam_p, subln, layer, ts, lam_init):
    bs, nrow, _ = q_rows.shape
    n_pages = page_table.shape[1]
    pps = math.gcd(n_pages, DECODE_PAGES_PER_STEP)
    n_steps = n_pages // pps + 1
    kern = functools.partial(_decode_kernel, ts=ts, n_steps=n_steps, pps=pps, lam_init=lam_init)

    def page(which):
        return lambda b, p, pt: (pt[b, jnp.minimum(pps * p + which, n_pages - 1)], layer, 0, 0)

    k_specs = [pl.BlockSpec((None, None, ATT_W, PAGE), page(w)) for w in range(pps)]
    v_specs = [pl.BlockSpec((None, None, PAGE * ATT_HEADS, HEAD_V), page(w)) for w in range(pps)]
    grid_spec = pltpu.PrefetchScalarGridSpec(
        num_scalar_prefetch=1,
        grid=(bs, n_steps),
        in_specs=[pl.BlockSpec((None, nrow, ATT_W), lambda b, p, pt: (b, 0, 0))] + k_specs + v_specs + [
            pl.BlockSpec((None, ATT_W, PAGE), lambda b, p, pt: (b, 0, 0)),
            pl.BlockSpec((None, PAGE * ATT_HEADS, HEAD_V), lambda b, p, pt: (b, 0, 0)),
            pl.BlockSpec((8, HEAD_QK), lambda b, p, pt: (0, 0)),
            pl.BlockSpec((1, HEAD_V), lambda b, p, pt: (0, 0))],
        out_specs=pl.BlockSpec((None, ts, ATT_W), lambda b, p, pt: (b, 0, 0)),
        scratch_shapes=[pltpu.VMEM((nrow, ATT_W), BF16), pltpu.VMEM((nrow, 1), F32),
                        pltpu.VMEM((nrow, 1), F32), pltpu.VMEM((nrow, HEAD_V), F32)],
    )
    return pl.pallas_call(
        kern, grid_spec=grid_spec,
        out_shape=jax.ShapeDtypeStruct((bs, ts, ATT_W), F32),
        compiler_params=_cparams(("arbitrary", "arbitrary")),
    )(page_table, q_rows, *([cache_kt] * pps), *([cache_v] * pps), kn_t, vn, lam_p, subln)


def _merge_kernel(x_ref, y_ref, bon_ref, g_ref, oatt_ref, gates_ref, gt_ref, ln_ref,
                  wrw_ref, watt_ref, wo_ref, o_ref):
    y = y_ref[...]
    n = y.shape[0]
    parts = []
    for h in range(RW_HEADS):
        yh = y[:, h * RW_HEAD:(h + 1) * RW_HEAD]
        mu = jnp.mean(yh, axis=-1, keepdims=True)
        d = yh - mu
        var = jnp.mean(d * d, axis=-1, keepdims=True)
        parts.append(d * lax.rsqrt(var + RW_GN_EPS))
    yn = jnp.concatenate(parts, axis=-1) * ln_ref[0:1, :] + ln_ref[1:2, :]
    out_rw = (yn + bon_ref[...]) * g_ref[...]
    y_rw = _dot(out_rw, wrw_ref[...])
    y_att = _dot(oatt_ref[...], watt_ref[...])
    gates = gates_ref[...]
    merged = (jax.nn.sigmoid(gates[:, :D_MODEL]) * y_rw
              + jax.nn.sigmoid(gates[:, D_MODEL:]) * y_att)
    o_ref[...] = x_ref[...] + gt_ref[...] * _dot(merged, wo_ref[...])


def _merge(x, y, bon, g, oatt, proj, gt, ln, wrw, watt, wo, tm):
    N = x.shape[0]
    tpg = (N // gt.shape[0]) // tm
    s512 = pl.BlockSpec((tm, RW_W), lambda i: (i, 0))
    full = lambda a: pl.BlockSpec(a.shape, lambda i: (0,) * a.ndim)
    return pl.pallas_call(
        _merge_kernel,
        grid=(N // tm,),
        in_specs=[pl.BlockSpec((tm, D_MODEL), lambda i: (i, 0)), s512, s512, s512, s512,
                  pl.BlockSpec((tm, 2 * D_MODEL), lambda i: (i, 0)),
                  _mod_spec(gt, tm, tpg), full(ln), full(wrw), full(watt), full(wo)],
        out_specs=pl.BlockSpec((tm, D_MODEL), lambda i: (i, 0)),
        out_shape=jax.ShapeDtypeStruct((N, D_MODEL), F32),
        compiler_params=_cparams(("arbitrary",)),
    )(x, y, bon, g, oatt, proj, gt, ln, wrw, watt, wo)


def _first_argmax(cols):
    best = cols[0]
    idx = jnp.zeros(best.shape, jnp.int32)
    for n, c in enumerate(cols[1:], start=1):
        take = c > best
        best = jnp.where(take, c, best)
        idx = jnp.where(take, n, idx)
    return best, idx


def _router_kernel(x_ref, g_ref, sc_ref, sh_ref, wr_ref, eb_ref, h_ref, gate_ref):
    y = _rms(x_ref[...], NORM_EPS) * g_ref[...]
    h = y * (1.0 + sc_ref[...]) + sh_ref[...]
    h_ref[...] = h.astype(BF16)
    logits = _dot_hi(h, wr_ref[...])
    s = jax.nn.sigmoid(logits)
    sel = s + eb_ref[...]
    sc_cols = [sel[:, e:e + 1] for e in range(N_EXPERTS)]
    s_cols = [s[:, e:e + 1] for e in range(N_EXPERTS)]
    grp = []
    for gi in range(N_GROUPS):
        cols = sc_cols[gi * EPG:(gi + 1) * EPG]
        m1, i1 = _first_argmax(cols)
        rest = [jnp.where(i1 == n, -jnp.inf, c) for n, c in enumerate(cols)]
        m2, _ = _first_argmax(rest)
        grp.append(m1 + m2)
    _, g_idx = _first_argmax(grp)
    pick = lambda cols_all, n: sum(jnp.where(g_idx == gi, cols_all[gi * EPG + n], 0.0)
                                   for gi in range(N_GROUPS))
    sel_g = [pick(sc_cols, n) for n in range(EPG)]
    s_g = [pick(s_cols, n) for n in range(EPG)]
    _, loc1 = _first_argmax(sel_g)
    _, loc2 = _first_argmax([jnp.where(loc1 == n, -jnp.inf, c) for n, c in enumerate(sel_g)])
    w_1 = sum(jnp.where(loc1 == n, s_g[n], 0.0) for n in range(EPG))
    w_2 = sum(jnp.where(loc2 == n, s_g[n], 0.0) for n in range(EPG))
    tot = w_1 + w_2
    w_1, w_2 = w_1 / tot, w_2 / tot
    e1 = g_idx * EPG + loc1
    e2 = g_idx * EPG + loc2
    lane = lax.broadcasted_iota(jnp.int32, logits.shape, 1)
    gate_ref[...] = jnp.where(lane == e1, w_1, 0.0) + jnp.where(lane == e2, w_2, 0.0)


def _router(x, g, sc, sh, wr, eb, tm):
    N, D = x.shape
    tpg = (N // sc.shape[0]) // tm
    return pl.pallas_call(
        _router_kernel,
        grid=(N // tm,),
        in_specs=[pl.BlockSpec((tm, D), lambda i: (i, 0)),
                  pl.BlockSpec((1, D), lambda i: (0, 0)),
                  _mod_spec(sc, tm, tpg), _mod_spec(sh, tm, tpg),
                  pl.BlockSpec((D, 128), lambda i: (0, 0)),
                  pl.BlockSpec((1, 128), lambda i: (0, 0))],
        out_specs=[pl.BlockSpec((tm, D), lambda i: (i, 0)),
                   pl.BlockSpec((tm, 128), lambda i: (i, 0))],
        out_shape=[jax.ShapeDtypeStruct((N, D), BF16), jax.ShapeDtypeStruct((N, 128), F32)],
        compiler_params=_cparams(("arbitrary",)),
    )(x, g, sc, sh, wr, eb)


def _moe_kernel(x_ref, h_ref, gate_ref, gt_ref, w1_ref, w3_ref, w2_ref, o_ref, acc_ref):
    e = pl.program_id(1)

    @pl.when(e == 0)
    def _():
        acc_ref[...] = jnp.zeros(acc_ref.shape, F32)

    h = h_ref[...]
    a1 = jnp.dot(h, w1_ref[...], preferred_element_type=F32)
    a3 = jnp.dot(h, w3_ref[...], preferred_element_type=F32)
    lane = lax.broadcasted_iota(jnp.int32, gate_ref.shape, 1)
    gcol = jnp.sum(jnp.where(lane == e, gate_ref[...], 0.0), axis=-1, keepdims=True)
    act = (a1 * jax.nn.sigmoid(a1)) * a3 * gcol
    acc_ref[...] += _dot(act, w2_ref[...])

    @pl.when(e == pl.num_programs(1) - 1)
    def _():
        o_ref[...] = x_ref[...] + gt_ref[...] * acc_ref[...]


def _moe(x, h, gate, gt, w1, w3, w2, tm):
    N, D = x.shape
    tpg = (N // gt.shape[0]) // tm
    return pl.pallas_call(
        _moe_kernel,
        grid=(N // tm, N_EXPERTS),
        in_specs=[pl.BlockSpec((tm, D), lambda i, e: (i, 0)),
                  pl.BlockSpec((tm, D), lambda i, e: (i, 0)),
                  pl.BlockSpec((tm, 128), lambda i, e: (i, 0)),
                  _mod_spec(gt, tm, tpg),
                  pl.BlockSpec((None, D, D_EXPERT), lambda i, e: (e, 0, 0)),
                  pl.BlockSpec((None, D, D_EXPERT), lambda i, e: (e, 0, 0)),
                  pl.BlockSpec((None, D_EXPERT, D), lambda i, e: (e, 0, 0))],
        out_specs=pl.BlockSpec((tm, D), lambda i, e: (i, 0)),
        out_shape=jax.ShapeDtypeStruct((N, D), F32),
        scratch_shapes=[pltpu.VMEM((tm, D), F32)],
        compiler_params=_cparams(("arbitrary", "arbitrary")),
    )(x, h, gate, gt, w1, w3, w2)


def _final_norm_kernel(x_ref, g_ref, o_ref):
    o_ref[...] = _rms(x_ref[...], NORM_EPS) * g_ref[...]


def _final_norm(x, g, tm):
    N, D = x.shape
    return pl.pallas_call(
        _final_norm_kernel,
        grid=(N // tm,),
        in_specs=[pl.BlockSpec((tm, D), lambda i: (i, 0)), pl.BlockSpec((1, D), lambda i: (0, 0))],
        out_specs=pl.BlockSpec((tm, D), lambda i: (i, 0)),
        out_shape=jax.ShapeDtypeStruct((N, D), F32),
        compiler_params=_cparams(("arbitrary",)),
    )(x, g)


def _group_layer(x, mods, lw, cfg, z0, s0, attend):
    sh1, sc1, gt1, sh2, sc2, gt2 = mods
    tm = cfg["tm"]
    proj = _nm_matmul(x, lw["norm1_g"], sc1, sh1, lw["w_in"], cfg["tm_in"], cfg["tn_in"])
    r, lwd, kk, ka, k2, v, g, bon = _rw_prep(proj, z0, lw["mu"], lw["rw_vecs"], lw["wup"],
                                               cfg["tm_prep"], cfg["seq"])
    y, s_fin = _wkv(r, lwd, kk, ka, k2, v, s0, cfg["n_seq"], cfg["tb"], cfg["chunk"])
    qs, kr = _rope(proj, cfg["rope_tabs"], cfg["tm_prep"])
    oatt = attend(qs, kr, proj)
    x = _merge(x, y, bon, g, oatt, proj, gt1, lw["ln"], lw["w_rw_out"], lw["w_att_out"],
               lw["w_o"], tm)
    h2, gate = _router(x, lw["norm2_g"], sc2, sh2, lw["w_router"], lw["e_bias"], tm)
    x = _moe(x, h2, gate, gt2, lw["moe_w1"], lw["moe_w3"], lw["moe_w2"], cfg["tm_moe"])
    return x, proj, kr, s_fin


def kernel(x_prompt, x_sample, c_prompt, c_sample, cache_k, cache_v, page_table, state_shift, state_wkv, w_ada, b_ada, norm1_g, norm2_g, w_in, rw_mu, rw_w0, rw_w_up, rw_a0, rw_a_up, rw_g_up, rw_k_k, rw_k_a, rw_r_k, rw_ln_w, rw_ln_b, w_rw_out, att_lq1, att_lk1, att_lq2, att_lk2, att_subln, w_att_out, w_o, w_router, e_bias, moe_w1, moe_w3, moe_w2, normf_g):
    bp, tp, D = x_prompt.shape
    bs, ts, _ = x_sample.shape
    depth = w_in.shape[0]
    n_pages = page_table.shape[1]
    past_len = n_pages * PAGE
    n_p, n_s = bp * tp, bs * ts

    z_w, q_w, k_w, v_w, grw_w, gatt_w = jnp.split(
        w_in, [RW_PROJ, RW_PROJ + ATT_W, RW_PROJ + 2 * ATT_W, RW_PROJ + 3 * ATT_W,
               RW_PROJ + 3 * ATT_W + D], axis=-1)
    w_in_p = jnp.concatenate(
        [grw_w, gatt_w, q_w, k_w, v_w, z_w,
         jnp.zeros((depth, D, RW_PROJ_PAD - RW_PROJ), F32)], axis=-1).astype(BF16)
    mu_p = jnp.pad(rw_mu, ((0, 0), (0, RW_PROJ_PAD - RW_PROJ)))
    wup = jnp.zeros((depth, 3, LORA_PAD, RW_W), F32)
    wup = wup.at[:, 0, 0:DECAY_LORA].set(rw_w_up)
    wup = wup.at[:, 1, DECAY_LORA:DECAY_LORA + AAA_LORA].set(rw_a_up)
    wup = wup.at[:, 2, DECAY_LORA + AAA_LORA:DECAY_LORA + AAA_LORA + GATE_LORA].set(rw_g_up)
    wup = wup.astype(BF16)
    rw_vecs = jnp.stack([rw_w0, rw_a0, rw_k_k, rw_k_a, rw_r_k.reshape(depth, RW_W),
                         jnp.zeros_like(rw_w0), jnp.zeros_like(rw_w0), jnp.zeros_like(rw_w0)], axis=1)
    ln = jnp.stack([rw_ln_w, rw_ln_b], axis=1)
    lam_p = jnp.stack([att_lq1, att_lk1, att_lq2, att_lk2] + [jnp.zeros_like(att_lq1)] * 4, axis=1)
    w_router_p = jnp.pad(w_router, ((0, 0), (0, 128 - N_EXPERTS)))
    e_bias_p = jnp.pad(e_bias, (0, 128 - N_EXPERTS)).reshape(1, 128)
    w_rw_out_b, w_att_out_b, w_o_b = (w.astype(BF16) for w in (w_rw_out, w_att_out, w_o))
    moe_w1_b, moe_w3_b, moe_w2_b = (w.astype(BF16) for w in (moe_w1, moe_w3, moe_w2))

    n_c = bp + bs
    n_cp = -(-n_c // 8) * 8
    c_all = jnp.pad(jnp.concatenate([c_prompt, c_sample], axis=0), ((0, n_cp - n_c), (0, 0)))
    mod = _ada_mod(c_all, w_ada, b_ada)

    tm_p = min(512, tp)
    cfg_p = dict(tm=tm_p, tm_in=min(1024, tp), tn_in=1792, tm_prep=min(256, tp), seq=tp, n_seq=bp,
                 tb=min(256, tp), chunk=WKV_CHUNK, tm_moe=min(1024, tp),
                 rope_tabs=_rope_tables(jnp.arange(tp)))
    cfg_s = dict(tm=n_s, tm_in=n_s, tn_in=1792, tm_prep=n_s, seq=ts, n_seq=bs, tb=8, chunk=8, tm_moe=n_s,
                 rope_tabs=_rope_tables(jnp.tile(past_len + jnp.arange(ts), bs)))

    cache_kt = jnp.transpose(cache_k, (0, 1, 3, 4, 5, 2)).reshape(cache_k.shape[0], depth, ATT_W, PAGE)
    cache_v2 = cache_v.reshape(cache_v.shape[0], depth, PAGE * ATT_HEADS, HEAD_V)
    zp0 = jnp.zeros((bp, 1, RW_PROJ_PAD), F32)
    sp0 = jnp.zeros((bp, RW_HEADS, RW_HEAD, RW_HEAD), F32)

    xp = x_prompt.reshape(n_p, D)
    xs = x_sample.reshape(n_s, D)
    outs = {k: [] for k in ("kp", "vp", "zp", "sp", "ks", "vs", "zs", "ss")}
    for l in range(depth):
        lam_init = 0.8 - 0.6 * math.exp(-0.3 * l)
        lw = dict(norm1_g=norm1_g[l].reshape(1, D), norm2_g=norm2_g[l].reshape(1, D),
                  w_in=w_in_p[l], mu=mu_p[l].reshape(1, RW_PROJ_PAD), rw_vecs=rw_vecs[l],
                  wup=wup[l], ln=ln[l], w_rw_out=w_rw_out_b[l], w_att_out=w_att_out_b[l],
                  w_o=w_o_b[l], w_router=w_router_p, e_bias=e_bias_p,
                  moe_w1=moe_w1_b[l], moe_w3=moe_w3_b[l], moe_w2=moe_w2_b[l])
        subln = att_subln[l].reshape(1, HEAD_V)
        mods_p = [m.reshape(bp, 1, D) for m in jnp.split(mod[l, :bp], 6, axis=-1)]
        mods_s = [jnp.repeat(m, ts, axis=0).reshape(1, n_s, D)
                  for m in jnp.split(mod[l, bp:n_c], 6, axis=-1)]

        attend_p = lambda qs, kr, proj: _flash(qs, kr, proj, lam_p[l], subln, bp,
                                               min(512, tp), lam_init)
        xp, proj_p, kr_p, s_fin = _group_layer(xp, mods_p, lw, cfg_p, zp0, sp0, attend_p)
        outs["kp"].append(kr_p.reshape(bp, tp, ATT_HEADS, 2, HEAD_QK))
        outs["vp"].append(proj_p[:, COL_V:COL_V + ATT_W].reshape(bp, tp, ATT_HEADS, HEAD_V))
        outs["zp"].append(proj_p.reshape(bp, tp, IN_W_PAD)[:, -1, COL_Z:COL_Z + RW_PROJ])
        outs["sp"].append(s_fin)

        def attend_s(qs, kr, proj):
            q_rows = jnp.tile(qs.reshape(bs, ts, ATT_W), (1, 2 * ATT_HEADS, 1))
            kn_t = jnp.pad(jnp.swapaxes(kr.reshape(bs, ts, ATT_W), 1, 2),
                           ((0, 0), (0, 0), (0, PAGE - ts)))
            vn = jnp.pad(proj[:, COL_V:COL_V + ATT_W].reshape(bs, ts * ATT_HEADS, HEAD_V),
                         ((0, 0), (0, (PAGE - ts) * ATT_HEADS), (0, 0)))
            return _decode_attn(q_rows, kn_t, vn, cache_kt, cache_v2, page_table, lam_p[l], subln, l,
                                ts, lam_init).reshape(n_s, ATT_W)

        z0_s = jnp.pad(jnp.repeat(state_shift[l], ts, axis=0), ((0, 0), (0, RW_PROJ_PAD - RW_PROJ)))
        xs, proj_s, kr_s, s_fin = _sample_layer(xs, mods_s, lw, cfg_s, z0_s, state_wkv[l], attend_s, bs, ts)
        outs["ks"].append(kr_s.reshape(bs, ts, ATT_HEADS, 2, HEAD_QK))
        outs["vs"].append(proj_s[:, COL_V:COL_V + ATT_W].reshape(bs, ts, ATT_HEADS, HEAD_V))
        outs["zs"].append(proj_s.reshape(bs, ts, IN_W_PAD)[:, -1, COL_Z:COL_Z + RW_PROJ])
        outs["ss"].append(s_fin)

    y_prompt = _final_norm(xp, normf_g.reshape(1, D), tm_p).reshape(bp, tp, D)
    y_sample = _final_norm(xs, normf_g.reshape(1, D), n_s).reshape(bs, ts, D)
    return (y_prompt, y_sample,
            jnp.stack(outs["kp"], axis=1), jnp.stack(outs["vp"], axis=1),
            jnp.stack(outs["zp"], axis=0), jnp.stack(outs["sp"], axis=0),
            jnp.stack(outs["ks"], axis=1), jnp.stack(outs["vs"], axis=1),
            jnp.stack(outs["zs"], axis=0), jnp.stack(outs["ss"], axis=0))


def _sample_layer(x, mods, lw, cfg, z0, s0, attend, bs, ts):
    sh1, sc1, gt1, sh2, sc2, gt2 = mods
    tm = cfg["tm"]
    proj = _nm_matmul(x, lw["norm1_g"], sc1, sh1, lw["w_in"], cfg["tm_in"], cfg["tn_in"])
    prep = _rw_prep(proj, z0, lw["mu"], lw["rw_vecs"], lw["wup"], cfg["tm_prep"], cfg["seq"])
    r, lwd, kk, ka, k2, v, g, bon = prep
    pad8 = lambda a: jnp.pad(a.reshape(bs, ts, RW_W), ((0, 0), (0, 8 - ts), (0, 0))).reshape(bs * 8, RW_W)
    y8, s_fin = _wkv(*(pad8(a) for a in (r, lwd, kk, ka, k2, v)), s0, bs, 8, 8)
    y = y8.reshape(bs, 8, RW_W)[:, :ts].reshape(bs * ts, RW_W)
    qs, kr = _rope(proj, cfg["rope_tabs"], cfg["tm_prep"])
    oatt = attend(qs, kr, proj)
    x = _merge(x, y, bon, g, oatt, proj, gt1, lw["ln"], lw["w_rw_out"], lw["w_att_out"],
               lw["w_o"], tm)
    h2, gate = _router(x, lw["norm2_g"], sc2, sh2, lw["w_router"], lw["e_bias"], tm)
    x = _moe(x, h2, gate, gt2, lw["moe_w1"], lw["moe_w3"], lw["moe_w2"], cfg["tm_moe"])
    return x, proj, kr, s_fin
```

```python
import functools
import math
from typing import NamedTuple

import jax
import jax.numpy as jnp
from jax import lax
from jax.experimental import pallas as pl
from jax.experimental.pallas import tpu as pltpu

F32 = jnp.float32
BF16 = jnp.bfloat16

D_MODEL = 1024
RW_HEAD = 64
RW_HEADS = 8
RW_W = RW_HEADS * RW_HEAD
DECAY_LORA = 32
AAA_LORA = 32
GATE_LORA = 96
RW_PROJ = 3 * RW_W + DECAY_LORA + AAA_LORA + GATE_LORA
RW_PROJ_PAD = 1792
LORA_PAD = RW_PROJ_PAD - 3 * RW_W
RW_GN_EPS = 64e-5
ATT_HEADS = 4
HEAD_QK = 64
HEAD_V = 128
ATT_W = 512
ROPE_DIM = HEAD_QK // 4
ROPE_THETA = 500000.0
SUBLN_EPS = 1e-5
NEG_INF = -1e30
N_EXPERTS = 16
N_GROUPS = 4
EPG = 4
D_EXPERT = 512
NORM_EPS = 1e-6
PAGE = 128
DECODE_PAGES_PER_STEP = 8
WKV_SEQS_PER_STEP = 2
FLASH_HEADS_PER_STEP = 4
WKV_CHUNK = 64

COL_GATE = 0
COL_Q = 2048
COL_V = 3072
COL_Z = 3584
IN_W_PAD = COL_Z + RW_PROJ_PAD

VMEM_LIMIT = 56 * 1024 * 1024

_NN = (((1,), (0,)), ((), ()))
_NT = (((1,), (1,)), ((), ()))
_TN = (((0,), (0,)), ((), ()))


def _dot(a, b, dims=_NN):
    return lax.dot_general(a.astype(BF16), b.astype(BF16), dims, preferred_element_type=F32)


def _dot_hi(a, b, dims=_NN):
    return lax.dot_general(a, b, dims, preferred_element_type=F32,
                           precision=lax.Precision.HIGHEST)


def _cparams(sem):
    return pltpu.CompilerParams(dimension_semantics=sem, vmem_limit_bytes=VMEM_LIMIT)


def _rms(x, eps):
    return x * lax.rsqrt(jnp.mean(x * x, axis=-1, keepdims=True) + eps)


def _ada_kernel(c_ref, w_ref, b_ref, o_ref):
    c = c_ref[...]
    sc = c * jax.nn.sigmoid(c)
    o_ref[...] = _dot(sc, w_ref[...]) + b_ref[...]


def _ada_mod(c_all, w_ada, b_ada):
    L, D, N6 = w_ada.shape
    M = c_all.shape[0]
    tn = 1536
    return pl.pallas_call(
        _ada_kernel,
        grid=(L, N6 // tn),
        in_specs=[pl.BlockSpec((M, D), lambda l, j: (0, 0)),
                  pl.BlockSpec((None, D, tn), lambda l, j: (l, 0, j)),
                  pl.BlockSpec((None, 1, tn), lambda l, j: (l, 0, j))],
        out_specs=pl.BlockSpec((None, M, tn), lambda l, j: (l, 0, j)),
        out_shape=jax.ShapeDtypeStruct((L, M, N6), F32),
        compiler_params=_cparams(("arbitrary", "arbitrary")),
    )(c_all, w_ada, b_ada.reshape(L, 1, N6))


class _Mod(NamedTuple):
    arr: jax.Array
    layer: int
    sec: int


def _mod_spec(m, tm, tiles_per_group):
    rows = m.arr.shape[2]
    return pl.BlockSpec((None, None, rows, D_MODEL),
                        lambda i, *_: (m.layer, i // tiles_per_group, 0, m.sec))


def _nm_matmul_kernel(x_ref, g_ref, sc_ref, sh_ref, w_ref, o_ref, h_ref):
    @pl.when(pl.program_id(1) == 0)
    def _():
        y = _rms(x_ref[...], NORM_EPS) * g_ref[...]
        h_ref[...] = (y * (1.0 + sc_ref[...]) + sh_ref[...]).astype(BF16)

    o_ref[...] = jnp.dot(h_ref[...], w_ref[...], preferred_element_type=F32)


def _nm_matmul(x, g, sc, sh, w, tm, tn):
    N, D = x.shape
    n_out = w.shape[2]
    tpg = (N // sc.arr.shape[1]) // tm
    layer = sc.layer
    return pl.pallas_call(
        _nm_matmul_kernel,
        grid=(N // tm, n_out // tn),
        in_specs=[pl.BlockSpec((tm, D), lambda i, j: (i, 0)),
                  pl.BlockSpec((1, D), lambda i, j: (0, 0)),
                  _mod_spec(sc, tm, tpg), _mod_spec(sh, tm, tpg),
                  pl.BlockSpec((None, D, tn), lambda i, j: (layer, 0, j))],
        out_specs=pl.BlockSpec((tm, tn), lambda i, j: (i, j)),
        out_shape=jax.ShapeDtypeStruct((N, n_out), F32),
        scratch_shapes=[pltpu.VMEM((tm, D), BF16)],
        compiler_params=_cparams(("arbitrary", "arbitrary")),
    )(x, g, sc.arr, sh.arr, w)


def _head_sum(x):
    parts = []
    for h in range(RW_HEADS):
        s = jnp.sum(x[:, h * RW_HEAD:(h + 1) * RW_HEAD], axis=-1, keepdims=True)
        parts.append(jnp.broadcast_to(s, (x.shape[0], RW_HEAD)))
    return jnp.concatenate(parts, axis=-1)


def _rw_prep_kernel(z_ref, z0_ref, mu_ref, vec_ref, wup_ref,
                    r_ref, lw_ref, kk_ref, ka_ref, k2_ref, v_ref, g_ref, bon_ref,
                    carry_ref, *, tm, seq_len):
    i = pl.program_id(0)
    z = z_ref[...]
    rolled = pltpu.roll(z, 1, axis=0)
    row = lax.broadcasted_iota(jnp.int32, z.shape, 0)
    if seq_len >= tm:
        tiles_per_seq = seq_len // tm
        first = jnp.where(i % tiles_per_seq == 0, z0_ref[...], carry_ref[0:1, :])
        z_prev = jnp.where(row == 0, first, rolled)
        carry_ref[0:1, :] = z[tm - 1:tm, :]
    else:
        z_prev = jnp.where(row % seq_len == 0, z0_ref[...], rolled)
    zs = z + (z_prev - z) * mu_ref[...]
    r = zs[:, 0:RW_W]
    k = zs[:, RW_W:2 * RW_W]
    v = zs[:, 2 * RW_W:3 * RW_W]
    tail = zs[:, 3 * RW_W:]
    w0, a0, k_k, k_a, r_k = (vec_ref[j:j + 1, :] for j in range(5))
    dw = _dot(jnp.tanh(tail), wup_ref[0])
    da = _dot(tail, wup_ref[1])
    g = _dot(jax.nn.sigmoid(tail), wup_ref[2])
    t = -(w0 + dw)
    softplus = jnp.maximum(t, 0.0) + jnp.log1p(jnp.exp(-jnp.abs(t)))
    lw = -jnp.exp(-softplus - 0.5)
    a = jax.nn.sigmoid(a0 + da)
    kk = k * k_k
    kk = kk / jnp.maximum(jnp.sqrt(_head_sum(kk * kk)), 1e-12)
    k2 = k * (1.0 + (a - 1.0) * k_a)
    r_ref[...] = r
    lw_ref[...] = lw
    kk_ref[...] = kk
    ka_ref[...] = kk * a
    k2_ref[...] = k2
    v_ref[...] = v
    g_ref[...] = g
    bon_ref[...] = _head_sum(r * k2 * r_k) * v


def _rw_prep(proj, z0, mu, vecs, wup, tm, seq_len):
    N = proj.shape[0]
    kern = functools.partial(_rw_prep_kernel, tm=tm, seq_len=seq_len)
    if seq_len >= tm:
        tps = seq_len // tm
        z0_spec = pl.BlockSpec((None, 1, RW_PROJ_PAD), lambda i: (i // tps, 0, 0))
    else:
        z0_spec = pl.BlockSpec((tm, RW_PROJ_PAD), lambda i: (i, 0))
    o_spec = pl.BlockSpec((tm, RW_W), lambda i: (i, 0))
    return pl.pallas_call(
        kern,
        grid=(N // tm,),
        in_specs=[pl.BlockSpec((tm, RW_PROJ_PAD), lambda i: (i, COL_Z // RW_PROJ_PAD)),
                  z0_spec,
                  pl.BlockSpec((1, RW_PROJ_PAD), lambda i: (0, 0)),
                  pl.BlockSpec((8, RW_W), lambda i: (0, 0)),
                  pl.BlockSpec((3, LORA_PAD, RW_W), lambda i: (0, 0, 0))],
        out_specs=[o_spec] * 8,
        out_shape=[jax.ShapeDtypeStruct((N, RW_W), F32)] * 8,
        scratch_shapes=[pltpu.VMEM((8, RW_PROJ_PAD), F32)],
        compiler_params=_cparams(("arbitrary",)),
    )(proj, z0, mu, vecs, wup)


def _wkv_kernel(r_ref, lw_ref, kk_ref, ka_ref, k2_ref, v_ref, s0_ref, y_ref, sout_ref,
                s_ref, *, tb, C, sps):
    tblk = pl.program_id(1)
    n_pair = RW_HEADS // 2
    zero64 = jnp.zeros((RW_HEAD, RW_HEAD), F32)
    units = [(q, p) for q in range(sps) for p in range(n_pair)]
    nu = range(len(units))

    @pl.when(tblk == 0)
    def _():
        for n, (q, p) in enumerate(units):
            s_ref[n] = jnp.concatenate(
                [jnp.concatenate([s0_ref[q, 2 * p], zero64], axis=1),
                 jnp.concatenate([zero64, s0_ref[q, 2 * p + 1]], axis=1)], axis=0)

    C2 = 2 * C
    ri = lax.broadcasted_iota(jnp.int32, (C, C), 0)
    ci = lax.broadcasted_iota(jnp.int32, (C, C), 1)
    tri = (ri >= ci).astype(F32)
    r2 = lax.broadcasted_iota(jnp.int32, (C2, C2), 0)
    c2 = lax.broadcasted_iota(jnp.int32, (C2, C2), 1)
    low_incl = r2 >= c2
    low_strict = r2 > c2
    eye = (r2 == c2).astype(F32)
    lo_lanes = lax.broadcasted_iota(jnp.int32, (C, 128), 1) < RW_HEAD
    n_sq = int(math.log2(C)) - 1

    def stack(x, p):
        xp = x[:, p * 128:(p + 1) * 128]
        return jnp.concatenate([jnp.where(lo_lanes, xp, 0.0), jnp.where(lo_lanes, 0.0, xp)], axis=0)

    def chunk(c, carry):
        rows = pl.ds(pl.multiple_of(c * C, C), C)
        a_t, r_t, b_t, k_t, b_r, k_r, v_q, p_tot = [], [], [], [], [], [], [], []
        for q in range(sps):
            lw = lw_ref[q, rows, :]
            kk = kk_ref[q, rows, :]
            ka = ka_ref[q, rows, :]
            k2 = k2_ref[q, rows, :]
            cs = _dot_hi(tri, lw)
            tot = cs[C - 1:C, :]
            p_inv = jnp.exp(-cs)
            p_rem = jnp.exp(tot - cs)
            a_t.append(-kk * jnp.exp(cs - lw))
            r_t.append(r_ref[q, rows, :] * jnp.exp(cs))
            b_t.append(ka * p_inv)
            k_t.append(k2 * p_inv)
            b_r.append(ka * p_rem)
            k_r.append(k2 * p_rem)
            v_q.append(v_ref[q, rows, :])
            p_tot.append(jnp.exp(tot))
        a_s = [stack(a_t[q], p).astype(BF16) for q, p in units]
        r_s = [stack(r_t[q], p).astype(BF16) for q, p in units]
        b_s = [stack(b_t[q], p).astype(BF16) for q, p in units]
        k_s = [stack(k_t[q], p).astype(BF16) for q, p in units]
        bk_r = [jnp.concatenate([stack(b_r[q], p), stack(k_r[q], p)], axis=0).astype(BF16)
                for q, p in units]
        v_s = [stack(v_q[q], p).astype(BF16) for q, p in units]
        ar = [jnp.concatenate([a_s[n], r_s[n]], axis=0) for n in nu]
        gb = [_dot(ar[n], b_s[n], _NT) for n in nu]
        gk = [_dot(ar[n], k_s[n], _NT) for n in nu]
        mb = [jnp.where(low_strict, gb[n][:C2], 0.0) for n in nu]
        nb = [jnp.where(low_incl, gb[n][C2:], 0.0) for n in nu]
        mk = [jnp.where(low_strict, gk[n][:C2], 0.0) for n in nu]
        nk = [jnp.where(low_incl, gk[n][C2:], 0.0) for n in nu]
        tinv = [eye + mb[n] for n in nu]
        pw = [_dot(mb[n], mb[n]) for n in nu]
        for lvl in range(n_sq):
            if lvl < n_sq - 1:
                both = [_dot(jnp.concatenate([tinv[n], pw[n]], axis=0), pw[n]) for n in nu]
                tinv = [tinv[n] + both[n][:C2] for n in nu]
                pw = [both[n][C2:] for n in nu]
            else:
                tinv = [tinv[n] + _dot(tinv[n], pw[n]) for n in nu]
        mkv = [_dot(mk[n], v_s[n]) for n in nu]
        x = [_dot(tinv[n], jnp.concatenate([a_s[n], mkv[n].astype(BF16)], axis=1)) for n in nu]
        s_old = [s_ref[n] for n in nu]
        s_bf = [s.astype(BF16) for s in s_old]
        u = [_dot(x[n][:, :128], s_bf[n], _NT) + x[n][:, 128:] for n in nu]
        y = [_dot(r_s[n], s_bf[n], _NT) + _dot(nb[n], u[n]) + _dot(nk[n], v_s[n]) for n in nu]
        for n, (q, p) in enumerate(units):
            uv = jnp.concatenate([u[n].astype(BF16), v_s[n]], axis=0)
            s_ref[n] = s_old[n] * p_tot[q][:, p * 128:(p + 1) * 128] + _dot(uv, bk_r[n], _TN)
            y_ref[q, rows, p * 128:(p + 1) * 128] = y[n][:C] + y[n][C:]
        return carry

    lax.fori_loop(0, tb // C, chunk, 0)

    @pl.when(tblk == pl.num_programs(1) - 1)
    def _():
        for n, (q, p) in enumerate(units):
            s = s_ref[n]
            sout_ref[q, 2 * p] = s[:RW_HEAD, :RW_HEAD]
            sout_ref[q, 2 * p + 1] = s[RW_HEAD:, RW_HEAD:]


def _wkv(r, lw, kk, ka, k2, v, s0, n_seq, tb, C):
    N = r.shape[0]
    T = N // n_seq
    sps = math.gcd(n_seq, WKV_SEQS_PER_STEP)
    kern = functools.partial(_wkv_kernel, tb=tb, C=C, sps=sps)
    in_spec = pl.BlockSpec((sps, tb, RW_W), lambda b, t: (b, t, 0))
    s_spec = pl.BlockSpec((sps, RW_HEADS, RW_HEAD, RW_HEAD), lambda b, t: (b, 0, 0, 0))
    seq = lambda a: a.reshape(n_seq, T, RW_W)
    y, s_fin = pl.pallas_call(
        kern,
        grid=(n_seq // sps, T // tb),
        in_specs=[in_spec] * 6 + [s_spec],
        out_specs=[in_spec, s_spec],
        out_shape=[jax.ShapeDtypeStruct((n_seq, T, RW_W), F32),
                   jax.ShapeDtypeStruct((n_seq, RW_HEADS, RW_HEAD, RW_HEAD), F32)],
        scratch_shapes=[pltpu.VMEM((sps * RW_HEADS // 2, 128, 128), F32)],
        compiler_params=_cparams(("arbitrary", "arbitrary")),
    )(seq(r), seq(lw), seq(kk), seq(ka), seq(k2), seq(v), s0)
    return y.reshape(N, RW_W), s_fin


def _rope_kernel(qk_ref, cos_ref, s1_ref, s2_ref, q_ref, k_ref):
    cos, s1, s2 = cos_ref[...], s1_ref[...], s2_ref[...]
    half = ROPE_DIM // 2
    for dst, base, scale in ((q_ref, 0, HEAD_QK ** -0.5), (k_ref, ATT_W, 1.0)):
        for cblk in range(ATT_W // 128):
            x = qk_ref[:, base + cblk * 128: base + (cblk + 1) * 128]
            up = pltpu.roll(x, 128 - half, axis=1)
            dn = pltpu.roll(x, half, axis=1)
            y = x * cos + up * s1 + dn * s2
            dst[:, cblk * 128:(cblk + 1) * 128] = y * scale if scale != 1.0 else y


def _rope(proj, tabs, tm):
    N = proj.shape[0]
    ntab = tabs[0].shape[0] // tm
    t_spec = pl.BlockSpec((tm, 128), lambda i: (i % ntab, 0))
    o_spec = pl.BlockSpec((tm, ATT_W), lambda i: (i, 0))
    return pl.pallas_call(
        _rope_kernel,
        grid=(N // tm,),
        in_specs=[pl.BlockSpec((tm, 2 * ATT_W), lambda i: (i, COL_Q // (2 * ATT_W))),
                  t_spec, t_spec, t_spec],
        out_specs=[o_spec, o_spec],
        out_shape=[jax.ShapeDtypeStruct((N, ATT_W), F32)] * 2,
        compiler_params=_cparams(("arbitrary",)),
    )(proj, *tabs)


def _rope_tables(pos):
    half = ROPE_DIM // 2
    inv = ROPE_THETA ** (-jnp.arange(0, ROPE_DIM, 2, dtype=F32) / ROPE_DIM)
    ang = pos.astype(F32)[:, None] * inv[None, :]
    cos, sin = jnp.cos(ang), jnp.sin(ang)
    n = pos.shape[0]
    one = jnp.ones((n, HEAD_QK - ROPE_DIM), F32)
    zero = jnp.zeros((n, HEAD_QK - ROPE_DIM), F32)
    zh = jnp.zeros((n, half), F32)
    c64 = jnp.concatenate([cos, cos, one], axis=1)
    s1 = jnp.concatenate([-sin, zh, zero], axis=1)
    s2 = jnp.concatenate([zh, sin, zero], axis=1)
    return tuple(jnp.concatenate([t, t], axis=1) for t in (c64, s1, s2))


def _lambda(lp_ref, lam_init):
    lp = lp_ref[...]
    d1 = jnp.sum(lp[0:1, :] * lp[1:2, :], axis=-1, keepdims=True)
    d2 = jnp.sum(lp[2:3, :] * lp[3:4, :], axis=-1, keepdims=True)
    return jnp.exp(d1) - jnp.exp(d2) + lam_init


def _flash_kernel(qi_ref, kj_ref, q_ref, k_ref, v_ref, lp_ref, sub_ref, o_ref,
                  qs_ref, m_ref, l_ref, acc_ref, *, tq, hps, lam_init):
    t = pl.program_id(2)
    i = qi_ref[t]
    j = kj_ref[t]
    heads = range(hps)
    cols = lambda hh: slice(hh * 128, (hh + 1) * 128)

    @pl.when(j == 0)
    def _():
        q = q_ref[...] * math.log2(math.e)
        lo = lax.broadcasted_iota(jnp.int32, (tq, 128), 1) < HEAD_QK
        for hh in heads:
            qh = q[:, cols(hh)]
            qs_ref[hh] = jnp.concatenate([jnp.where(lo, qh, 0.0), jnp.where(lo, 0.0, qh)],
                                         axis=0).astype(BF16)
        m_ref[...] = jnp.full(m_ref.shape, NEG_INF, F32)
        l_ref[...] = jnp.zeros(l_ref.shape, F32)
        acc_ref[...] = jnp.zeros(acc_ref.shape, F32)

    def step(diagonal):
        k = k_ref[...].astype(BF16)
        v = v_ref[...].astype(BF16)
        s = [lax.dot_general(k[:, cols(hh)], qs_ref[hh], _NT, preferred_element_type=F32)
             for hh in heads]
        if diagonal:
            key = lax.broadcasted_iota(jnp.int32, s[0].shape, 0)
            qry = lax.broadcasted_iota(jnp.int32, s[0].shape, 1)
            keep = key <= jnp.where(qry >= tq, qry - tq, qry)
            s = [jnp.where(keep, s[hh], NEG_INF) for hh in heads]
        m_old = [m_ref[hh] for hh in heads]
        m_new = [jnp.maximum(m_old[hh], jnp.max(s[hh], axis=0, keepdims=True)) for hh in heads]
        alpha = [jnp.exp2(m_old[hh] - m_new[hh]) for hh in heads]
        p = [jnp.exp2(s[hh] - m_new[hh]) for hh in heads]
        pv = [lax.dot_general(v[:, cols(hh)], p[hh].astype(BF16), _TN, preferred_element_type=F32)
              for hh in heads]
        for hh in heads:
            l_ref[hh] = alpha[hh] * l_ref[hh] + jnp.sum(p[hh], axis=0, keepdims=True)
            acc_ref[hh] = alpha[hh] * acc_ref[hh] + pv[hh]
            m_ref[hh] = m_new[hh]

    @pl.when(j < i)
    def _():
        step(False)

    @pl.when(j == i)
    def _():
        step(True)
        lam = _lambda(lp_ref, lam_init)
        for hh in heads:
            on = acc_ref[hh] / l_ref[hh]
            o_t = on[:, :tq] - lam * on[:, tq:]
            ms = jnp.mean(o_t * o_t, axis=0, keepdims=True)
            o_t = o_t * lax.rsqrt(ms + SUBLN_EPS) * sub_ref[...] * (1.0 - lam_init)
            o_ref[:, cols(hh)] = o_t.T


def _flash(qs, kr, proj, lam_p, subln, n_seq, tq, lam_init):
    N = qs.shape[0]
    nq = (N // n_seq) // tq
    hps = FLASH_HEADS_PER_STEP
    wid = hps * 128
    kern = functools.partial(_flash_kernel, tq=tq, hps=hps, lam_init=lam_init)
    vcol = COL_V // wid
    pairs = [(i, j) for i in range(nq) for j in range(i + 1)]
    qi = jnp.asarray([p[0] for p in pairs], jnp.int32)
    kj = jnp.asarray([p[1] for p in pairs], jnp.int32)
    grid_spec = pltpu.PrefetchScalarGridSpec(
        num_scalar_prefetch=2,
        grid=(n_seq, ATT_HEADS // hps, len(pairs)),
        in_specs=[pl.BlockSpec((tq, wid), lambda b, h, t, qi, kj: (b * nq + qi[t], h)),
                  pl.BlockSpec((tq, wid), lambda b, h, t, qi, kj: (b * nq + kj[t], h)),
                  pl.BlockSpec((tq, wid), lambda b, h, t, qi, kj: (b * nq + kj[t], vcol + h)),
                  pl.BlockSpec((8, HEAD_QK), lambda b, h, t, qi, kj: (0, 0)),
                  pl.BlockSpec((HEAD_V, 1), lambda b, h, t, qi, kj: (0, 0))],
        out_specs=pl.BlockSpec((tq, wid), lambda b, h, t, qi, kj: (b * nq + qi[t], h)),
        scratch_shapes=[pltpu.VMEM((hps, 2 * tq, 128), BF16), pltpu.VMEM((hps, 1, 2 * tq), F32),
                        pltpu.VMEM((hps, 1, 2 * tq), F32), pltpu.VMEM((hps, HEAD_V, 2 * tq), F32)],
    )
    return pl.pallas_call(
        kern, grid_spec=grid_spec,
        out_shape=jax.ShapeDtypeStruct((N, ATT_W), F32),
        compiler_params=_cparams(("arbitrary",) * 3),
    )(qi, kj, qs, kr, proj, lam_p, subln.reshape(HEAD_V, 1))


def _decode_kernel(pt_ref, q_ref, *refs, ts, n_steps, pps, lam_init):
    k_refs, v_refs = refs[:pps], refs[pps:2 * pps]
    kn_ref, vn_ref, lp_ref, sub_ref, o_ref, qrow, m_s, l_s, acc = refs[2 * pps:]
    p = pl.program_id(1)
    nrow = 2 * ATT_HEADS * ts
    rows_per_head = 2 * ts

    @pl.when(p == 0)
    def _():
        row = lax.broadcasted_iota(jnp.int32, (nrow, ATT_W), 0)
        lane = lax.broadcasted_iota(jnp.int32, (nrow, ATT_W), 1)
        qrow[...] = jnp.where(lane // HEAD_QK == row // ts, q_ref[...], 0.0).astype(BF16)
        m_s[...] = jnp.full(m_s.shape, NEG_INF, F32)
        l_s[...] = jnp.zeros(l_s.shape, F32)
        acc[...] = jnp.zeros(acc.shape, F32)

    def attend(k_refs, v_refs, own):
        kt = jnp.concatenate([r[...] for r in k_refs], axis=1)
        s = jnp.dot(qrow[...], kt.astype(BF16), preferred_element_type=F32)
        if own:
            r2 = lax.broadcasted_iota(jnp.int32, s.shape, 0)
            c2 = lax.broadcasted_iota(jnp.int32, s.shape, 1)
            s = jnp.where(c2 <= r2 % ts, s, NEG_INF)
        m_new = jnp.maximum(m_s[...], jnp.max(s, axis=-1, keepdims=True))
        alpha = jnp.exp(m_s[...] - m_new)
        pr = jnp.exp(s - m_new)
        l_s[...] = alpha * l_s[...] + jnp.sum(pr, axis=-1, keepdims=True)
        pr = pr.astype(BF16)
        pv = []
        for h in range(ATT_HEADS):
            vh = jnp.concatenate([r[pl.ds(h, PAGE, stride=ATT_HEADS), :] for r in v_refs], axis=0)
            pv.append(jnp.dot(pr[h * rows_per_head:(h + 1) * rows_per_head], vh.astype(BF16),
                              preferred_element_type=F32))
        acc[...] = alpha * acc[...] + jnp.concatenate(pv, axis=0)
        m_s[...] = m_new

    @pl.when(p < n_steps - 1)
    def _():
        attend(k_refs, v_refs, False)

    @pl.when(p == n_steps - 1)
    def _():
        attend((kn_ref,), (vn_ref,), True)
        lam = _lambda(lp_ref, lam_init)
        on = acc[...] / l_s[...]
        outs = []
        for h in range(ATT_HEADS):
            blk = on[h * rows_per_head:(h + 1) * rows_per_head]
            oh = blk[:ts] - lam * blk[ts:]
            outs.append(_rms(oh, SUBLN_EPS) * sub_ref[...] * (1.0 - lam_init))
        o_ref[...] = jnp.concatenate(outs, axis=-1)


def _decode_attn(q_rows, kn_t, vn, cache_kt, cache_v, page_table, lam_p, subln, layer, ts, lam_init):
    bs, nrow, _ = q_rows.shape
    n_pages = page_table.shape[1]
    pps = math.gcd(n_pages, DECODE_PAGES_PER_STEP)
    n_steps = n_pages // pps + 1
    kern = functools.partial(_decode_kernel, ts=ts, n_steps=n_steps, pps=pps, lam_init=lam_init)

    def page(which):
        return lambda b, p, pt: (pt[b, jnp.minimum(pps * p + which, n_pages - 1)], layer, 0, 0)

    k_specs = [pl.BlockSpec((None, None, ATT_W, PAGE), page(w)) for w in range(pps)]
    v_specs = [pl.BlockSpec((None, None, PAGE * ATT_HEADS, HEAD_V), page(w)) for w in range(pps)]
    grid_spec = pltpu.PrefetchScalarGridSpec(
        num_scalar_prefetch=1,
        grid=(bs, n_steps),
        in_specs=[pl.BlockSpec((None, nrow, ATT_W), lambda b, p, pt: (b, 0, 0))] + k_specs + v_specs + [
            pl.BlockSpec((None, ATT_W, PAGE), lambda b, p, pt: (b, 0, 0)),
            pl.BlockSpec((None, PAGE * ATT_HEADS, HEAD_V), lambda b, p, pt: (b, 0, 0)),
            pl.BlockSpec((8, HEAD_QK), lambda b, p, pt: (0, 0)),
            pl.BlockSpec((1, HEAD_V), lambda b, p, pt: (0, 0))],
        out_specs=pl.BlockSpec((None, ts, ATT_W), lambda b, p, pt: (b, 0, 0)),
        scratch_shapes=[pltpu.VMEM((nrow, ATT_W), BF16), pltpu.VMEM((nrow, 1), F32),
                        pltpu.VMEM((nrow, 1), F32), pltpu.VMEM((nrow, HEAD_V), F32)],
    )
    return pl.pallas_call(
        kern, grid_spec=grid_spec,
        out_shape=jax.ShapeDtypeStruct((bs, ts, ATT_W), F32),
        compiler_params=_cparams(("arbitrary", "arbitrary")),
    )(page_table, q_rows, *([cache_kt] * pps), *([cache_v] * pps), kn_t, vn, lam_p, subln)


def _merge_kernel(x_ref, y_ref, bon_ref, g_ref, oatt_ref, gates_ref, gt_ref, ln_ref,
                  wrw_ref, watt_ref, wo_ref, o_ref):
    y = y_ref[...]
    n = y.shape[0]
    parts = []
    for h in range(RW_HEADS):
        yh = y[:, h * RW_HEAD:(h + 1) * RW_HEAD]
        mu = jnp.mean(yh, axis=-1, keepdims=True)
        d = yh - mu
        var = jnp.mean(d * d, axis=-1, keepdims=True)
        parts.append(d * lax.rsqrt(var + RW_GN_EPS))
    yn = jnp.concatenate(parts, axis=-1) * ln_ref[0:1, :] + ln_ref[1:2, :]
    out_rw = (yn + bon_ref[...]) * g_ref[...]
    y_rw = _dot(out_rw, wrw_ref[...])
    y_att = _dot(oatt_ref[...], watt_ref[...])
    gates = gates_ref[...]
    merged = (jax.nn.sigmoid(gates[:, :D_MODEL]) * y_rw
              + jax.nn.sigmoid(gates[:, D_MODEL:]) * y_att)
    o_ref[...] = x_ref[...] + gt_ref[...] * _dot(merged, wo_ref[...])


def _merge(x, y, bon, g, oatt, proj, gt, ln, wrw, watt, wo, tm):
    N = x.shape[0]
    tpg = (N // gt.arr.shape[1]) // tm
    s512 = pl.BlockSpec((tm, RW_W), lambda i: (i, 0))
    full = lambda a: pl.BlockSpec(a.shape, lambda i: (0,) * a.ndim)
    return pl.pallas_call(
        _merge_kernel,
        grid=(N // tm,),
        in_specs=[pl.BlockSpec((tm, D_MODEL), lambda i: (i, 0)), s512, s512, s512, s512,
                  pl.BlockSpec((tm, 2 * D_MODEL), lambda i: (i, 0)),
                  _mod_spec(gt, tm, tpg), full(ln), full(wrw), full(watt), full(wo)],
        out_specs=pl.BlockSpec((tm, D_MODEL), lambda i: (i, 0)),
        out_shape=jax.ShapeDtypeStruct((N, D_MODEL), F32),
        compiler_params=_cparams(("arbitrary",)),
    )(x, y, bon, g, oatt, proj, gt.arr, ln, wrw, watt, wo)


def _first_argmax(cols):
    best = cols[0]
    idx = jnp.zeros(best.shape, jnp.int32)
    for n, c in enumerate(cols[1:], start=1):
        take = c > best
        best = jnp.where(take, c, best)
        idx = jnp.where(take, n, idx)
    return best, idx


def _router_kernel(x_ref, g_ref, sc_ref, sh_ref, wr_ref, eb_ref, h_ref, gate_ref):
    y = _rms(x_ref[...], NORM_EPS) * g_ref[...]
    h = y * (1.0 + sc_ref[...]) + sh_ref[...]
    h_hi = h.astype(BF16)
    h_ref[...] = h_hi
    h_lo = (h - h_hi.astype(F32)).astype(BF16)
    w = wr_ref[...]
    w_hi = w.astype(BF16)
    w_lo = (w - w_hi.astype(F32)).astype(BF16)
    dot = functools.partial(jnp.dot, preferred_element_type=F32)
    logits = dot(h_hi, w_hi) + (dot(h_lo, w_hi) + dot(h_hi, w_lo))
    s = jax.nn.sigmoid(logits.T[:N_EXPERTS])
    sel = s + eb_ref[...][:N_EXPERTS]
    sc_cols = [sel[e:e + 1, :] for e in range(N_EXPERTS)]
    s_cols = [s[e:e + 1, :] for e in range(N_EXPERTS)]
    grp = []
    for gi in range(N_GROUPS):
        cols = sc_cols[gi * EPG:(gi + 1) * EPG]
        m1, i1 = _first_argmax(cols)
        rest = [jnp.where(i1 == n, -jnp.inf, c) for n, c in enumerate(cols)]
        m2, _ = _first_argmax(rest)
        grp.append(m1 + m2)
    _, g_idx = _first_argmax(grp)
    pick = lambda cols_all, n: sum(jnp.where(g_idx == gi, cols_all[gi * EPG + n], 0.0)
                                   for gi in range(N_GROUPS))
    sel_g = [pick(sc_cols, n) for n in range(EPG)]
    s_g = [pick(s_cols, n) for n in range(EPG)]
    _, loc1 = _first_argmax(sel_g)
    _, loc2 = _first_argmax([jnp.where(loc1 == n, -jnp.inf, c) for n, c in enumerate(sel_g)])
    w_1 = sum(jnp.where(loc1 == n, s_g[n], 0.0) for n in range(EPG))
    w_2 = sum(jnp.where(loc2 == n, s_g[n], 0.0) for n in range(EPG))
    tot = w_1 + w_2
    w_1, w_2 = w_1 / tot, w_2 / tot
    e1 = g_idx * EPG + loc1
    e2 = g_idx * EPG + loc2
    expert = lax.broadcasted_iota(jnp.int32, (logits.shape[1], logits.shape[0]), 0)
    gate_t = jnp.where(expert == e1, w_1, 0.0) + jnp.where(expert == e2, w_2, 0.0)
    gate_ref[...] = gate_t.T


def _router(x, g, sc, sh, wr, eb, tm):
    N, D = x.shape
    tpg = (N // sc.arr.shape[1]) // tm
    return pl.pallas_call(
        _router_kernel,
        grid=(N // tm,),
        in_specs=[pl.BlockSpec((tm, D), lambda i: (i, 0)),
                  pl.BlockSpec((1, D), lambda i: (0, 0)),
                  _mod_spec(sc, tm, tpg), _mod_spec(sh, tm, tpg),
                  pl.BlockSpec((D, 128), lambda i: (0, 0)),
                  pl.BlockSpec((128, 1), lambda i: (0, 0))],
        out_specs=[pl.BlockSpec((tm, D), lambda i: (i, 0)),
                   pl.BlockSpec((tm, 128), lambda i: (i, 0))],
        out_shape=[jax.ShapeDtypeStruct((N, D), BF16), jax.ShapeDtypeStruct((N, 128), F32)],
        compiler_params=_cparams(("arbitrary",)),
    )(x, g, sc.arr, sh.arr, wr, eb)


def _moe_kernel(x_ref, h_ref, gate_ref, gt_ref, w1_ref, w3_ref, w2_ref, o_ref, acc_ref):
    e = pl.program_id(1)

    @pl.when(e == 0)
    def _():
        acc_ref[...] = jnp.zeros(acc_ref.shape, F32)

    h = h_ref[...]
    a1 = jnp.dot(h, w1_ref[...], preferred_element_type=F32)
    a3 = jnp.dot(h, w3_ref[...], preferred_element_type=F32)
    lane = lax.broadcasted_iota(jnp.int32, gate_ref.shape, 1)
    gcol = jnp.sum(jnp.where(lane == e, gate_ref[...], 0.0), axis=-1, keepdims=True)
    act = (a1 * jax.nn.sigmoid(a1)) * a3 * gcol
    acc_ref[...] += _dot(act, w2_ref[...])

    @pl.when(e == pl.num_programs(1) - 1)
    def _():
        o_ref[...] = x_ref[...] + gt_ref[...] * acc_ref[...]


def _moe(x, h, gate, gt, w1, w3, w2, tm):
    N, D = x.shape
    tpg = (N // gt.arr.shape[1]) // tm
    layer = gt.layer
    return pl.pallas_call(
        _moe_kernel,
        grid=(N // tm, N_EXPERTS),
        in_specs=[pl.BlockSpec((tm, D), lambda i, e: (i, 0)),
                  pl.BlockSpec((tm, D), lambda i, e: (i, 0)),
                  pl.BlockSpec((tm, 128), lambda i, e: (i, 0)),
                  _mod_spec(gt, tm, tpg),
                  pl.BlockSpec((None, None, D, D_EXPERT), lambda i, e: (layer, e, 0, 0)),
                  pl.BlockSpec((None, None, D, D_EXPERT), lambda i, e: (layer, e, 0, 0)),
                  pl.BlockSpec((None, None, D_EXPERT, D), lambda i, e: (layer, e, 0, 0))],
        out_specs=pl.BlockSpec((tm, D), lambda i, e: (i, 0)),
        out_shape=jax.ShapeDtypeStruct((N, D), F32),
        scratch_shapes=[pltpu.VMEM((tm, D), F32)],
        compiler_params=_cparams(("arbitrary", "arbitrary")),
    )(x, h, gate, gt.arr, w1, w3, w2)


def _final_norm_kernel(x_ref, g_ref, o_ref):
    o_ref[...] = _rms(x_ref[...], NORM_EPS) * g_ref[...]


def _final_norm(x, g, tm):
    N, D = x.shape
    return pl.pallas_call(
        _final_norm_kernel,
        grid=(N // tm,),
        in_specs=[pl.BlockSpec((tm, D), lambda i: (i, 0)), pl.BlockSpec((1, D), lambda i: (0, 0))],
        out_specs=pl.BlockSpec((tm, D), lambda i: (i, 0)),
        out_shape=jax.ShapeDtypeStruct((N, D), F32),
        compiler_params=_cparams(("arbitrary",)),
    )(x, g)


def _group_layer(x, mods, lw, cfg, z0, s0, attend):
    sh1, sc1, gt1, sh2, sc2, gt2 = mods
    tm = cfg["tm"]
    proj = _nm_matmul(x, lw["norm1_g"], sc1, sh1, lw["w_in"], cfg["tm_in"], cfg["tn_in"])
    r, lwd, kk, ka, k2, v, g, bon = _rw_prep(proj, z0, lw["mu"], lw["rw_vecs"], lw["wup"],
                                               cfg["tm_prep"], cfg["seq"])
    y, s_fin = _wkv(r, lwd, kk, ka, k2, v, s0, cfg["n_seq"], cfg["tb"], cfg["chunk"])
    qs, kr = _rope(proj, cfg["rope_tabs"], cfg["tm_prep"])
    oatt = attend(qs, kr, proj)
    x = _merge(x, y, bon, g, oatt, proj, gt1, lw["ln"], lw["w_rw_out"], lw["w_att_out"],
               lw["w_o"], tm)
    h2, gate = _router(x, lw["norm2_g"], sc2, sh2, lw["w_router"], lw["e_bias"], tm)
    x = _moe(x, h2, gate, gt2, lw["moe_w1"], lw["moe_w3"], lw["moe_w2"], cfg["tm_moe"])
    return x, proj, kr, s_fin


def kernel(x_prompt, x_sample, c_prompt, c_sample, cache_k, cache_v, page_table, state_shift, state_wkv, w_ada, b_ada, norm1_g, norm2_g, w_in, rw_mu, rw_w0, rw_w_up, rw_a0, rw_a_up, rw_g_up, rw_k_k, rw_k_a, rw_r_k, rw_ln_w, rw_ln_b, w_rw_out, att_lq1, att_lk1, att_lq2, att_lk2, att_subln, w_att_out, w_o, w_router, e_bias, moe_w1, moe_w3, moe_w2, normf_g):
    bp, tp, D = x_prompt.shape
    bs, ts, _ = x_sample.shape
    depth = w_in.shape[0]
    n_pages = page_table.shape[1]
    past_len = n_pages * PAGE
    n_p, n_s = bp * tp, bs * ts

    z_w, q_w, k_w, v_w, grw_w, gatt_w = jnp.split(
        w_in, [RW_PROJ, RW_PROJ + ATT_W, RW_PROJ + 2 * ATT_W, RW_PROJ + 3 * ATT_W,
               RW_PROJ + 3 * ATT_W + D], axis=-1)
    w_in_p = jnp.concatenate(
        [grw_w, gatt_w, q_w, k_w, v_w, z_w,
         jnp.zeros((depth, D, RW_PROJ_PAD - RW_PROJ), F32)], axis=-1).astype(BF16)
    mu_p = jnp.pad(rw_mu, ((0, 0), (0, RW_PROJ_PAD - RW_PROJ)))
    wup = jnp.zeros((depth, 3, LORA_PAD, RW_W), F32)
    wup = wup.at[:, 0, 0:DECAY_LORA].set(rw_w_up)
    wup = wup.at[:, 1, DECAY_LORA:DECAY_LORA + AAA_LORA].set(rw_a_up)
    wup = wup.at[:, 2, DECAY_LORA + AAA_LORA:DECAY_LORA + AAA_LORA + GATE_LORA].set(rw_g_up)
    wup = wup.astype(BF16)
    rw_vecs = jnp.stack([rw_w0, rw_a0, rw_k_k, rw_k_a, rw_r_k.reshape(depth, RW_W),
                         jnp.zeros_like(rw_w0), jnp.zeros_like(rw_w0), jnp.zeros_like(rw_w0)], axis=1)
    ln = jnp.stack([rw_ln_w, rw_ln_b], axis=1)
    lam_p = jnp.stack([att_lq1, att_lk1, att_lq2, att_lk2] + [jnp.zeros_like(att_lq1)] * 4, axis=1)
    w_router_p = jnp.pad(w_router, ((0, 0), (0, 128 - N_EXPERTS)))
    e_bias_p = jnp.pad(e_bias, (0, 128 - N_EXPERTS)).reshape(128, 1)
    w_rw_out_b, w_att_out_b, w_o_b = (w.astype(BF16) for w in (w_rw_out, w_att_out, w_o))
    moe_w1_b, moe_w3_b, moe_w2_b = (w.astype(BF16) for w in (moe_w1, moe_w3, moe_w2))

    n_c = bp + bs
    n_cp = -(-n_c // 8) * 8
    c_all = jnp.pad(jnp.concatenate([c_prompt, c_sample], axis=0), ((0, n_cp - n_c), (0, 0)))
    mod = _ada_mod(c_all, w_ada, b_ada)
    mod_p = mod[:, :bp].reshape(depth, bp, 1, 6 * D)
    mod_s = jnp.repeat(mod[:, bp:n_c], ts, axis=1).reshape(depth, 1, n_s, 6 * D)

    tm_p = min(512, tp)
    cfg_p = dict(tm=tm_p, tm_in=min(1024, tp), tn_in=1792, tm_prep=min(256, tp), seq=tp, n_seq=bp,
                 tb=min(256, tp), chunk=WKV_CHUNK, tm_moe=min(1024, tp),
                 rope_tabs=_rope_tables(jnp.arange(tp)))
    cfg_s = dict(tm=n_s, tm_in=n_s, tn_in=1792, tm_prep=n_s, seq=ts, n_seq=bs, tb=8, chunk=8, tm_moe=n_s,
                 rope_tabs=_rope_tables(jnp.tile(past_len + jnp.arange(ts), bs)))

    cache_kt = jnp.transpose(cache_k, (0, 1, 3, 4, 5, 2)).reshape(cache_k.shape[0], depth, ATT_W, PAGE)
    cache_v2 = cache_v.reshape(cache_v.shape[0], depth, PAGE * ATT_HEADS, HEAD_V)
    zp0 = jnp.zeros((bp, 1, RW_PROJ_PAD), F32)
    sp0 = jnp.zeros((bp, RW_HEADS, RW_HEAD, RW_HEAD), F32)

    xp = x_prompt.reshape(n_p, D)
    xs = x_sample.reshape(n_s, D)
    outs = {k: [] for k in ("kp", "vp", "zp", "sp", "ks", "vs", "zs", "ss")}
    for l in range(depth):
        lam_init = 0.8 - 0.6 * math.exp(-0.3 * l)
        lw = dict(norm1_g=norm1_g[l].reshape(1, D), norm2_g=norm2_g[l].reshape(1, D),
                  w_in=w_in_p, mu=mu_p[l].reshape(1, RW_PROJ_PAD), rw_vecs=rw_vecs[l],
                  wup=wup[l], ln=ln[l], w_rw_out=w_rw_out_b[l], w_att_out=w_att_out_b[l],
                  w_o=w_o_b[l], w_router=w_router_p, e_bias=e_bias_p,
                  moe_w1=moe_w1_b, moe_w3=moe_w3_b, moe_w2=moe_w2_b)
        subln = att_subln[l].reshape(1, HEAD_V)
        mods_p = [_Mod(mod_p, l, sec) for sec in range(6)]
        mods_s = [_Mod(mod_s, l, sec) for sec in range(6)]

        attend_p = lambda qs, kr, proj: _flash(qs, kr, proj, lam_p[l], subln, bp,
                                               min(512, tp), lam_init)
        xp, proj_p, kr_p, s_fin = _group_layer(xp, mods_p, lw, cfg_p, zp0, sp0, attend_p)
        outs["kp"].append(kr_p.reshape(bp, tp, ATT_HEADS, 2, HEAD_QK))
        outs["vp"].append(proj_p[:, COL_V:COL_V + ATT_W].reshape(bp, tp, ATT_HEADS, HEAD_V))
        outs["zp"].append(proj_p.reshape(bp, tp, IN_W_PAD)[:, -1, COL_Z:COL_Z + RW_PROJ])
        outs["sp"].append(s_fin)

        def attend_s(qs, kr, proj):
            q_rows = jnp.tile(qs.reshape(bs, ts, ATT_W), (1, 2 * ATT_HEADS, 1))
            kn_t = jnp.pad(jnp.swapaxes(kr.reshape(bs, ts, ATT_W), 1, 2),
                           ((0, 0), (0, 0), (0, PAGE - ts)))
            vn = jnp.pad(proj[:, COL_V:COL_V + ATT_W].reshape(bs, ts * ATT_HEADS, HEAD_V),
                         ((0, 0), (0, (PAGE - ts) * ATT_HEADS), (0, 0)))
            return _decode_attn(q_rows, kn_t, vn, cache_kt, cache_v2, page_table, lam_p[l], subln, l,
                                ts, lam_init).reshape(n_s, ATT_W)

        z0_s = jnp.pad(jnp.repeat(state_shift[l], ts, axis=0), ((0, 0), (0, RW_PROJ_PAD - RW_PROJ)))
        xs, proj_s, kr_s, s_fin = _sample_layer(xs, mods_s, lw, cfg_s, z0_s, state_wkv[l], attend_s, bs, ts)
        outs["ks"].append(kr_s.reshape(bs, ts, ATT_HEADS, 2, HEAD_QK))
        outs["vs"].append(proj_s[:, COL_V:COL_V + ATT_W].reshape(bs, ts, ATT_HEADS, HEAD_V))
        outs["zs"].append(proj_s.reshape(bs, ts, IN_W_PAD)[:, -1, COL_Z:COL_Z + RW_PROJ])
        outs["ss"].append(s_fin)

    y_prompt = _final_norm(xp, normf_g.reshape(1, D), tm_p).reshape(bp, tp, D)
    y_sample = _final_norm(xs, normf_g.reshape(1, D), n_s).reshape(bs, ts, D)
    return (y_prompt, y_sample,
            jnp.stack(outs["kp"], axis=1), jnp.stack(outs["vp"], axis=1),
            jnp.stack(outs["zp"], axis=0), jnp.stack(outs["sp"], axis=0),
            jnp.stack(outs["ks"], axis=1), jnp.stack(outs["vs"], axis=1),
            jnp.stack(outs["zs"], axis=0), jnp.stack(outs["ss"], axis=0))


def _sample_layer(x, mods, lw, cfg, z0, s0, attend, bs, ts):
    sh1, sc1, gt1, sh2, sc2, gt2 = mods
    tm = cfg["tm"]
    proj = _nm_matmul(x, lw["norm1_g"], sc1, sh1, lw["w_in"], cfg["tm_in"], cfg["tn_in"])
    prep = _rw_prep(proj, z0, lw["mu"], lw["rw_vecs"], lw["wup"], cfg["tm_prep"], cfg["seq"])
    r, lwd, kk, ka, k2, v, g, bon = prep
    pad8 = lambda a: jnp.pad(a.reshape(bs, ts, RW_W), ((0, 0), (0, 8 - ts), (0, 0))).reshape(bs * 8, RW_W)
    y8, s_fin = _wkv(*(pad8(a) for a in (r, lwd, kk, ka, k2, v)), s0, bs, 8, 8)
    y = y8.reshape(bs, 8, RW_W)[:, :ts].reshape(bs * ts, RW_W)
    qs, kr = _rope(proj, cfg["rope_tabs"], cfg["tm_prep"])
    oatt = attend(qs, kr, proj)
    x = _merge(x, y, bon, g, oatt, proj, gt1, lw["ln"], lw["w_rw_out"], lw["w_att_out"],
               lw["w_o"], tm)
    h2, gate = _router(x, lw["norm2_g"], sc2, sh2, lw["w_router"], lw["e_bias"], tm)
    x = _moe(x, h2, gate, gt2, lw["moe_w1"], lw["moe_w3"], lw["moe_w2"], cfg["tm_moe"])
    return x, proj, kr, s_fin
```

```python
import functools
import math
from typing import NamedTuple

import jax
import jax.numpy as jnp
from jax import lax
from jax.experimental import pallas as pl
from jax.experimental.pallas import tpu as pltpu

F32 = jnp.float32
BF16 = jnp.bfloat16

D_MODEL = 1024
RW_HEAD = 64
RW_HEADS = 8
RW_W = RW_HEADS * RW_HEAD
DECAY_LORA = 32
AAA_LORA = 32
GATE_LORA = 96
RW_PROJ = 3 * RW_W + DECAY_LORA + AAA_LORA + GATE_LORA
RW_PROJ_PAD = 1792
LORA_PAD = RW_PROJ_PAD - 3 * RW_W
RW_GN_EPS = 64e-5
ATT_HEADS = 4
HEAD_QK = 64
HEAD_V = 128
ATT_W = 512
ROPE_DIM = HEAD_QK // 4
ROPE_THETA = 500000.0
SUBLN_EPS = 1e-5
NEG_INF = -1e30
N_EXPERTS = 16
N_GROUPS = 4
EPG = 4
D_EXPERT = 512
NORM_EPS = 1e-6
PAGE = 128
DECODE_PAGES_PER_STEP = 8
WKV_SEQS_PER_STEP = 2
MOE_EXPERTS_PER_STEP = 4
FLASH_HEADS_PER_STEP = 4
WKV_CHUNK = 64

COL_GATE = 0
COL_Q = 2048
COL_V = 3072
COL_Z = 3584
IN_W_PAD = COL_Z + RW_PROJ_PAD

VMEM_LIMIT = 56 * 1024 * 1024

_NN = (((1,), (0,)), ((), ()))
_NT = (((1,), (1,)), ((), ()))
_TN = (((0,), (0,)), ((), ()))


def _dot(a, b, dims=_NN):
    return lax.dot_general(a.astype(BF16), b.astype(BF16), dims, preferred_element_type=F32)


def _dot_hi(a, b, dims=_NN):
    return lax.dot_general(a, b, dims, preferred_element_type=F32,
                           precision=lax.Precision.HIGHEST)


def _cparams(sem):
    return pltpu.CompilerParams(dimension_semantics=sem, vmem_limit_bytes=VMEM_LIMIT)


def _rms(x, eps):
    return x * lax.rsqrt(jnp.mean(x * x, axis=-1, keepdims=True) + eps)


def _ada_kernel(c_ref, w_ref, b_ref, o_ref):
    c = c_ref[...]
    sc = c * jax.nn.sigmoid(c)
    o_ref[...] = _dot(sc, w_ref[...]) + b_ref[...]


def _ada_mod(c_all, w_ada, b_ada):
    L, D, N6 = w_ada.shape
    M = c_all.shape[0]
    tn = 1536
    return pl.pallas_call(
        _ada_kernel,
        grid=(L, N6 // tn),
        in_specs=[pl.BlockSpec((M, D), lambda l, j: (0, 0)),
                  pl.BlockSpec((None, D, tn), lambda l, j: (l, 0, j)),
                  pl.BlockSpec((None, 1, tn), lambda l, j: (l, 0, j))],
        out_specs=pl.BlockSpec((None, M, tn), lambda l, j: (l, 0, j)),
        out_shape=jax.ShapeDtypeStruct((L, M, N6), F32),
        compiler_params=_cparams(("arbitrary", "arbitrary")),
    )(c_all, w_ada, b_ada.reshape(L, 1, N6))


class _Mod(NamedTuple):
    arr: jax.Array
    layer: int
    sec: int


def _mod_spec(m, tm, tiles_per_group):
    rows = m.arr.shape[2]
    return pl.BlockSpec((None, None, rows, D_MODEL),
                        lambda i, *_: (m.layer, i // tiles_per_group, 0, m.sec))


def _nm_matmul_kernel(x_ref, g_ref, sc_ref, sh_ref, w_ref, o_ref, h_ref):
    @pl.when(pl.program_id(1) == 0)
    def _():
        y = _rms(x_ref[...], NORM_EPS) * g_ref[...]
        h_ref[...] = (y * (1.0 + sc_ref[...]) + sh_ref[...]).astype(BF16)

    o_ref[...] = jnp.dot(h_ref[...], w_ref[...], preferred_element_type=F32)


def _nm_matmul(x, g, sc, sh, w, tm, tn):
    N, D = x.shape
    n_out = w.shape[2]
    tpg = (N // sc.arr.shape[1]) // tm
    layer = sc.layer
    return pl.pallas_call(
        _nm_matmul_kernel,
        grid=(N // tm, n_out // tn),
        in_specs=[pl.BlockSpec((tm, D), lambda i, j: (i, 0)),
                  pl.BlockSpec((1, D), lambda i, j: (0, 0)),
                  _mod_spec(sc, tm, tpg), _mod_spec(sh, tm, tpg),
                  pl.BlockSpec((None, D, tn), lambda i, j: (layer, 0, j))],
        out_specs=pl.BlockSpec((tm, tn), lambda i, j: (i, j)),
        out_shape=jax.ShapeDtypeStruct((N, n_out), F32),
        scratch_shapes=[pltpu.VMEM((tm, D), BF16)],
        compiler_params=_cparams(("arbitrary", "arbitrary")),
    )(x, g, sc.arr, sh.arr, w)


def _head_sum(x):
    parts = []
    for h in range(RW_HEADS):
        s = jnp.sum(x[:, h * RW_HEAD:(h + 1) * RW_HEAD], axis=-1, keepdims=True)
        parts.append(jnp.broadcast_to(s, (x.shape[0], RW_HEAD)))
    return jnp.concatenate(parts, axis=-1)


def _rw_prep_kernel(z_ref, z0_ref, mu_ref, vec_ref, wup_ref,
                    r_ref, lw_ref, kk_ref, ka_ref, k2_ref, v_ref, g_ref, bon_ref,
                    carry_ref, *, tm, seq_len):
    i = pl.program_id(0)
    z = z_ref[...]
    rolled = pltpu.roll(z, 1, axis=0)
    row = lax.broadcasted_iota(jnp.int32, z.shape, 0)
    if seq_len >= tm:
        tiles_per_seq = seq_len // tm
        first = jnp.where(i % tiles_per_seq == 0, z0_ref[...], carry_ref[0:1, :])
        z_prev = jnp.where(row == 0, first, rolled)
        carry_ref[0:1, :] = z[tm - 1:tm, :]
    else:
        z_prev = jnp.where(row % seq_len == 0, z0_ref[...], rolled)
    zs = z + (z_prev - z) * mu_ref[...]
    r = zs[:, 0:RW_W]
    k = zs[:, RW_W:2 * RW_W]
    v = zs[:, 2 * RW_W:3 * RW_W]
    tail = zs[:, 3 * RW_W:]
    w0, a0, k_k, k_a, r_k = (vec_ref[j:j + 1, :] for j in range(5))
    dw = _dot(jnp.tanh(tail), wup_ref[0])
    da = _dot(tail, wup_ref[1])
    g = _dot(jax.nn.sigmoid(tail), wup_ref[2])
    t = -(w0 + dw)
    softplus = jnp.maximum(t, 0.0) + jnp.log1p(jnp.exp(-jnp.abs(t)))
    lw = -jnp.exp(-softplus - 0.5)
    a = jax.nn.sigmoid(a0 + da)
    kk = k * k_k
    kk = kk / jnp.maximum(jnp.sqrt(_head_sum(kk * kk)), 1e-12)
    k2 = k * (1.0 + (a - 1.0) * k_a)
    r_ref[...] = r
    lw_ref[...] = lw
    kk_ref[...] = kk
    ka_ref[...] = kk * a
    k2_ref[...] = k2
    v_ref[...] = v
    g_ref[...] = g
    bon_ref[...] = _head_sum(r * k2 * r_k) * v


def _rw_prep(proj, z0, mu, vecs, wup, tm, seq_len):
    N = proj.shape[0]
    kern = functools.partial(_rw_prep_kernel, tm=tm, seq_len=seq_len)
    if seq_len >= tm:
        tps = seq_len // tm
        z0_spec = pl.BlockSpec((None, 1, RW_PROJ_PAD), lambda i: (i // tps, 0, 0))
    else:
        z0_spec = pl.BlockSpec((tm, RW_PROJ_PAD), lambda i: (i, 0))
    o_spec = pl.BlockSpec((tm, RW_W), lambda i: (i, 0))
    return pl.pallas_call(
        kern,
        grid=(N // tm,),
        in_specs=[pl.BlockSpec((tm, RW_PROJ_PAD), lambda i: (i, COL_Z // RW_PROJ_PAD)),
                  z0_spec,
                  pl.BlockSpec((1, RW_PROJ_PAD), lambda i: (0, 0)),
                  pl.BlockSpec((8, RW_W), lambda i: (0, 0)),
                  pl.BlockSpec((3, LORA_PAD, RW_W), lambda i: (0, 0, 0))],
        out_specs=[o_spec] * 8,
        out_shape=[jax.ShapeDtypeStruct((N, RW_W), F32)] * 8,
        scratch_shapes=[pltpu.VMEM((8, RW_PROJ_PAD), F32)],
        compiler_params=_cparams(("arbitrary",)),
    )(proj, z0, mu, vecs, wup)


def _wkv_kernel(r_ref, lw_ref, kk_ref, ka_ref, k2_ref, v_ref, s0_ref, y_ref, sout_ref,
                s_ref, *, tb, C, sps):
    tblk = pl.program_id(1)
    n_pair = RW_HEADS // 2
    zero64 = jnp.zeros((RW_HEAD, RW_HEAD), F32)
    units = [(q, p) for q in range(sps) for p in range(n_pair)]
    nu = range(len(units))

    @pl.when(tblk == 0)
    def _():
        for n, (q, p) in enumerate(units):
            s_ref[n] = jnp.concatenate(
                [jnp.concatenate([s0_ref[q, 2 * p], zero64], axis=1),
                 jnp.concatenate([zero64, s0_ref[q, 2 * p + 1]], axis=1)], axis=0)

    C2 = 2 * C
    ri = lax.broadcasted_iota(jnp.int32, (C, C), 0)
    ci = lax.broadcasted_iota(jnp.int32, (C, C), 1)
    tri = (ri >= ci).astype(F32)
    r2 = lax.broadcasted_iota(jnp.int32, (C2, C2), 0)
    c2 = lax.broadcasted_iota(jnp.int32, (C2, C2), 1)
    low_incl = r2 >= c2
    low_strict = r2 > c2
    eye = (r2 == c2).astype(F32)
    lo_lanes = lax.broadcasted_iota(jnp.int32, (C, 128), 1) < RW_HEAD
    n_sq = int(math.log2(C)) - 1

    def stack(x, p):
        xp = x[:, p * 128:(p + 1) * 128]
        return jnp.concatenate([jnp.where(lo_lanes, xp, 0.0), jnp.where(lo_lanes, 0.0, xp)], axis=0)

    def chunk(c, carry):
        rows = pl.ds(pl.multiple_of(c * C, C), C)
        a_t, r_t, b_t, k_t, b_r, k_r, v_q, p_tot = [], [], [], [], [], [], [], []
        for q in range(sps):
            lw = lw_ref[q, rows, :]
            kk = kk_ref[q, rows, :]
            ka = ka_ref[q, rows, :]
            k2 = k2_ref[q, rows, :]
            cs = _dot_hi(tri, lw)
            tot = cs[C - 1:C, :]
            p_inv = jnp.exp(-cs)
            p_rem = jnp.exp(tot - cs)
            a_t.append(-kk * jnp.exp(cs - lw))
            r_t.append(r_ref[q, rows, :] * jnp.exp(cs))
            b_t.append(ka * p_inv)
            k_t.append(k2 * p_inv)
            b_r.append(ka * p_rem)
            k_r.append(k2 * p_rem)
            v_q.append(v_ref[q, rows, :])
            p_tot.append(jnp.exp(tot))
        a_s = [stack(a_t[q], p).astype(BF16) for q, p in units]
        r_s = [stack(r_t[q], p).astype(BF16) for q, p in units]
        b_s = [stack(b_t[q], p).astype(BF16) for q, p in units]
        k_s = [stack(k_t[q], p).astype(BF16) for q, p in units]
        bk_r = [jnp.concatenate([stack(b_r[q], p), stack(k_r[q], p)], axis=0).astype(BF16)
                for q, p in units]
        v_s = [stack(v_q[q], p).astype(BF16) for q, p in units]
        ar = [jnp.concatenate([a_s[n], r_s[n]], axis=0) for n in nu]
        gb = [_dot(ar[n], b_s[n], _NT) for n in nu]
        gk = [_dot(ar[n], k_s[n], _NT) for n in nu]
        mb = [jnp.where(low_strict, gb[n][:C2], 0.0) for n in nu]
        nb = [jnp.where(low_incl, gb[n][C2:], 0.0) for n in nu]
        mk = [jnp.where(low_strict, gk[n][:C2], 0.0) for n in nu]
        nk = [jnp.where(low_incl, gk[n][C2:], 0.0) for n in nu]
        tinv = [eye + mb[n] for n in nu]
        pw = [_dot(mb[n], mb[n]) for n in nu]
        for lvl in range(n_sq):
            if lvl < n_sq - 1:
                both = [_dot(jnp.concatenate([tinv[n], pw[n]], axis=0), pw[n]) for n in nu]
                tinv = [tinv[n] + both[n][:C2] for n in nu]
                pw = [both[n][C2:] for n in nu]
            else:
                tinv = [tinv[n] + _dot(tinv[n], pw[n]) for n in nu]
        mkv = [_dot(mk[n], v_s[n]) for n in nu]
        x = [_dot(tinv[n], jnp.concatenate([a_s[n], mkv[n].astype(BF16)], axis=1)) for n in nu]
        s_old = [s_ref[n] for n in nu]
        s_bf = [s.astype(BF16) for s in s_old]
        u = [_dot(x[n][:, :128], s_bf[n], _NT) + x[n][:, 128:] for n in nu]
        y = [_dot(r_s[n], s_bf[n], _NT) + _dot(nb[n], u[n]) + _dot(nk[n], v_s[n]) for n in nu]
        for n, (q, p) in enumerate(units):
            uv = jnp.concatenate([u[n].astype(BF16), v_s[n]], axis=0)
            s_ref[n] = s_old[n] * p_tot[q][:, p * 128:(p + 1) * 128] + _dot(uv, bk_r[n], _TN)
            y_ref[q, rows, p * 128:(p + 1) * 128] = y[n][:C] + y[n][C:]
        return carry

    lax.fori_loop(0, tb // C, chunk, 0)

    @pl.when(tblk == pl.num_programs(1) - 1)
    def _():
        for n, (q, p) in enumerate(units):
            s = s_ref[n]
            sout_ref[q, 2 * p] = s[:RW_HEAD, :RW_HEAD]
            sout_ref[q, 2 * p + 1] = s[RW_HEAD:, RW_HEAD:]


def _wkv(r, lw, kk, ka, k2, v, s0, n_seq, tb, C):
    N = r.shape[0]
    T = N // n_seq
    sps = math.gcd(n_seq, WKV_SEQS_PER_STEP)
    kern = functools.partial(_wkv_kernel, tb=tb, C=C, sps=sps)
    in_spec = pl.BlockSpec((sps, tb, RW_W), lambda b, t: (b, t, 0))
    s_spec = pl.BlockSpec((sps, RW_HEADS, RW_HEAD, RW_HEAD), lambda b, t: (b, 0, 0, 0))
    seq = lambda a: a.reshape(n_seq, T, RW_W)
    y, s_fin = pl.pallas_call(
        kern,
        grid=(n_seq // sps, T // tb),
        in_specs=[in_spec] * 6 + [s_spec],
        out_specs=[in_spec, s_spec],
        out_shape=[jax.ShapeDtypeStruct((n_seq, T, RW_W), F32),
                   jax.ShapeDtypeStruct((n_seq, RW_HEADS, RW_HEAD, RW_HEAD), F32)],
        scratch_shapes=[pltpu.VMEM((sps * RW_HEADS // 2, 128, 128), F32)],
        compiler_params=_cparams(("arbitrary", "arbitrary")),
    )(seq(r), seq(lw), seq(kk), seq(ka), seq(k2), seq(v), s0)
    return y.reshape(N, RW_W), s_fin


def _rope_kernel(qk_ref, cos_ref, s1_ref, s2_ref, *rest, stacked):
    if stacked:
        v_ref, q_ref, k_ref, kt_ref, v4_ref = rest[0], *rest[-4:]
    else:
        q_ref, k_ref = rest
    cos, s1, s2 = cos_ref[...], s1_ref[...], s2_ref[...]
    half = ROPE_DIM // 2
    for dst, base, scale in ((q_ref, 0, HEAD_QK ** -0.5), (k_ref, ATT_W, 1.0)):
        for cblk in range(ATT_W // 128):
            x = qk_ref[:, base + cblk * 128: base + (cblk + 1) * 128]
            up = pltpu.roll(x, 128 - half, axis=1)
            dn = pltpu.roll(x, half, axis=1)
            y = x * cos + up * s1 + dn * s2
            dst[:, cblk * 128:(cblk + 1) * 128] = y * scale if scale != 1.0 else y
    if stacked:
        kt_ref[...] = k_ref[...].T
        rows = v_ref.shape[0]
        for h in range(ATT_HEADS):
            v4_ref[pl.ds(h, rows, stride=ATT_HEADS), :] = v_ref[:, h * HEAD_V:(h + 1) * HEAD_V]


def _rope(proj, tabs, tm, stack=None):
    N = proj.shape[0]
    ntab = tabs[0].shape[0] // tm
    t_spec = pl.BlockSpec((tm, 128), lambda i: (i % ntab, 0))
    o_spec = pl.BlockSpec((tm, ATT_W), lambda i: (i, 0))
    in_specs = [pl.BlockSpec((tm, 2 * ATT_W), lambda i: (i, COL_Q // (2 * ATT_W))),
                t_spec, t_spec, t_spec]
    out_specs = [o_spec, o_spec]
    out_shape = [jax.ShapeDtypeStruct((N, ATT_W), F32)] * 2
    operands = [proj, *tabs]
    aliases = {}
    if stack is not None:
        layer, depth, n_seq, kt_buf, v_buf = stack
        T = N // n_seq
        tps = T // tm
        in_specs.append(pl.BlockSpec((tm, ATT_W), lambda i: (i, COL_V // ATT_W)))
        operands.append(proj)
        out_specs += [pl.BlockSpec((None, None, ATT_W, tm), lambda i: (i // tps, layer, 0, i % tps)),
                      pl.BlockSpec((None, None, tm * ATT_HEADS, HEAD_V),
                                   lambda i: (i // tps, layer, i % tps, 0))]
        out_shape += [jax.ShapeDtypeStruct((n_seq, depth, ATT_W, T), F32),
                      jax.ShapeDtypeStruct((n_seq, depth, T * ATT_HEADS, HEAD_V), F32)]
        if kt_buf is not None:
            aliases = {len(operands): 2, len(operands) + 1: 3}
            in_specs += [pl.BlockSpec(memory_space=pl.ANY)] * 2
            operands += [kt_buf, v_buf]
    return pl.pallas_call(
        functools.partial(_rope_kernel, stacked=stack is not None),
        grid=(N // tm,),
        in_specs=in_specs,
        out_specs=out_specs,
        out_shape=out_shape,
        input_output_aliases=aliases,
        compiler_params=_cparams(("arbitrary",)),
    )(*operands)


def _rope_tables(pos):
    half = ROPE_DIM // 2
    inv = ROPE_THETA ** (-jnp.arange(0, ROPE_DIM, 2, dtype=F32) / ROPE_DIM)
    ang = pos.astype(F32)[:, None] * inv[None, :]
    cos, sin = jnp.cos(ang), jnp.sin(ang)
    n = pos.shape[0]
    one = jnp.ones((n, HEAD_QK - ROPE_DIM), F32)
    zero = jnp.zeros((n, HEAD_QK - ROPE_DIM), F32)
    zh = jnp.zeros((n, half), F32)
    c64 = jnp.concatenate([cos, cos, one], axis=1)
    s1 = jnp.concatenate([-sin, zh, zero], axis=1)
    s2 = jnp.concatenate([zh, sin, zero], axis=1)
    return tuple(jnp.concatenate([t, t], axis=1) for t in (c64, s1, s2))


def _lambda(lp_ref, lam_init):
    lp = lp_ref[...]
    d1 = jnp.sum(lp[0:1, :] * lp[1:2, :], axis=-1, keepdims=True)
    d2 = jnp.sum(lp[2:3, :] * lp[3:4, :], axis=-1, keepdims=True)
    return jnp.exp(d1) - jnp.exp(d2) + lam_init


def _flash_kernel(qi_ref, kj_ref, q_ref, k_ref, v_ref, lp_ref, sub_ref, o_ref,
                  qs_ref, m_ref, l_ref, acc_ref, *, tq, hps, lam_init):
    t = pl.program_id(2)
    i = qi_ref[t]
    j = kj_ref[t]
    heads = range(hps)
    cols = lambda hh: slice(hh * 128, (hh + 1) * 128)

    @pl.when(j == 0)
    def _():
        q = q_ref[...] * math.log2(math.e)
        lo = lax.broadcasted_iota(jnp.int32, (tq, 128), 1) < HEAD_QK
        for hh in heads:
            qh = q[:, cols(hh)]
            qs_ref[hh] = jnp.concatenate([jnp.where(lo, qh, 0.0), jnp.where(lo, 0.0, qh)],
                                         axis=0).astype(BF16)
        m_ref[...] = jnp.full(m_ref.shape, NEG_INF, F32)
        l_ref[...] = jnp.zeros(l_ref.shape, F32)
        acc_ref[...] = jnp.zeros(acc_ref.shape, F32)

    def step(diagonal):
        k = k_ref[...].astype(BF16)
        v = v_ref[...].astype(BF16)
        s = [lax.dot_general(k[:, cols(hh)], qs_ref[hh], _NT, preferred_element_type=F32)
             for hh in heads]
        if diagonal:
            key = lax.broadcasted_iota(jnp.int32, s[0].shape, 0)
            qry = lax.broadcasted_iota(jnp.int32, s[0].shape, 1)
            keep = key <= jnp.where(qry >= tq, qry - tq, qry)
            s = [jnp.where(keep, s[hh], NEG_INF) for hh in heads]
        m_old = [m_ref[hh] for hh in heads]
        m_new = [jnp.maximum(m_old[hh], jnp.max(s[hh], axis=0, keepdims=True)) for hh in heads]
        alpha = [jnp.exp2(m_old[hh] - m_new[hh]) for hh in heads]
        p = [jnp.exp2(s[hh] - m_new[hh]) for hh in heads]
        pv = [lax.dot_general(v[:, cols(hh)], p[hh].astype(BF16), _TN, preferred_element_type=F32)
              for hh in heads]
        for hh in heads:
            l_ref[hh] = alpha[hh] * l_ref[hh] + jnp.sum(p[hh], axis=0, keepdims=True)
            acc_ref[hh] = alpha[hh] * acc_ref[hh] + pv[hh]
            m_ref[hh] = m_new[hh]

    @pl.when(j < i)
    def _():
        step(False)

    @pl.when(j == i)
    def _():
        step(True)
        lam = _lambda(lp_ref, lam_init)
        for hh in heads:
            on = acc_ref[hh] / l_ref[hh]
            o_t = on[:, :tq] - lam * on[:, tq:]
            ms = jnp.mean(o_t * o_t, axis=0, keepdims=True)
            o_t = o_t * lax.rsqrt(ms + SUBLN_EPS) * sub_ref[...] * (1.0 - lam_init)
            o_ref[:, cols(hh)] = o_t.T


def _flash(qs, kr, proj, lam_p, subln, n_seq, tq, lam_init):
    N = qs.shape[0]
    nq = (N // n_seq) // tq
    hps = FLASH_HEADS_PER_STEP
    wid = hps * 128
    kern = functools.partial(_flash_kernel, tq=tq, hps=hps, lam_init=lam_init)
    vcol = COL_V // wid
    pairs = [(i, j) for i in range(nq) for j in range(i + 1)]
    qi = jnp.asarray([p[0] for p in pairs], jnp.int32)
    kj = jnp.asarray([p[1] for p in pairs], jnp.int32)
    grid_spec = pltpu.PrefetchScalarGridSpec(
        num_scalar_prefetch=2,
        grid=(n_seq, ATT_HEADS // hps, len(pairs)),
        in_specs=[pl.BlockSpec((tq, wid), lambda b, h, t, qi, kj: (b * nq + qi[t], h)),
                  pl.BlockSpec((tq, wid), lambda b, h, t, qi, kj: (b * nq + kj[t], h)),
                  pl.BlockSpec((tq, wid), lambda b, h, t, qi, kj: (b * nq + kj[t], vcol + h)),
                  pl.BlockSpec((8, HEAD_QK), lambda b, h, t, qi, kj: (0, 0)),
                  pl.BlockSpec((HEAD_V, 1), lambda b, h, t, qi, kj: (0, 0))],
        out_specs=pl.BlockSpec((tq, wid), lambda b, h, t, qi, kj: (b * nq + qi[t], h)),
        scratch_shapes=[pltpu.VMEM((hps, 2 * tq, 128), BF16), pltpu.VMEM((hps, 1, 2 * tq), F32),
                        pltpu.VMEM((hps, 1, 2 * tq), F32), pltpu.VMEM((hps, HEAD_V, 2 * tq), F32)],
    )
    return pl.pallas_call(
        kern, grid_spec=grid_spec,
        out_shape=jax.ShapeDtypeStruct((N, ATT_W), F32),
        compiler_params=_cparams(("arbitrary",) * 3),
    )(qi, kj, qs, kr, proj, lam_p, subln.reshape(HEAD_V, 1))


def _decode_kernel(pt_ref, q_ref, *refs, ts, n_steps, pps, lam_init):
    k_refs, v_refs = refs[:pps], refs[pps:2 * pps]
    kn_ref, vn_ref, lp_ref, sub_ref, o_ref, qrow, m_s, l_s, acc = refs[2 * pps:]
    p = pl.program_id(1)
    nrow = 2 * ATT_HEADS * ts
    rows_per_head = 2 * ts

    @pl.when(p == 0)
    def _():
        row = lax.broadcasted_iota(jnp.int32, (nrow, ATT_W), 0)
        lane = lax.broadcasted_iota(jnp.int32, (nrow, ATT_W), 1)
        qrow[...] = jnp.where(lane // HEAD_QK == row // ts, q_ref[...], 0.0).astype(BF16)
        m_s[...] = jnp.full(m_s.shape, NEG_INF, F32)
        l_s[...] = jnp.zeros(l_s.shape, F32)
        acc[...] = jnp.zeros(acc.shape, F32)

    def attend(k_refs, v_refs, own):
        kt = jnp.concatenate([r[...] for r in k_refs], axis=1)
        s = jnp.dot(qrow[...], kt.astype(BF16), preferred_element_type=F32)
        if own:
            r2 = lax.broadcasted_iota(jnp.int32, s.shape, 0)
            c2 = lax.broadcasted_iota(jnp.int32, s.shape, 1)
            s = jnp.where(c2 <= r2 % ts, s, NEG_INF)
        m_new = jnp.maximum(m_s[...], jnp.max(s, axis=-1, keepdims=True))
        alpha = jnp.exp(m_s[...] - m_new)
        pr = jnp.exp(s - m_new)
        l_s[...] = alpha * l_s[...] + jnp.sum(pr, axis=-1, keepdims=True)
        pr = pr.astype(BF16)
        pv = []
        for h in range(ATT_HEADS):
            vh = jnp.concatenate([r[pl.ds(h, PAGE, stride=ATT_HEADS), :] for r in v_refs], axis=0)
            pv.append(jnp.dot(pr[h * rows_per_head:(h + 1) * rows_per_head], vh.astype(BF16),
                              preferred_element_type=F32))
        acc[...] = alpha * acc[...] + jnp.concatenate(pv, axis=0)
        m_s[...] = m_new

    @pl.when(p < n_steps - 1)
    def _():
        attend(k_refs, v_refs, False)

    @pl.when(p == n_steps - 1)
    def _():
        attend((kn_ref,), (vn_ref,), True)
        lam = _lambda(lp_ref, lam_init)
        on = acc[...] / l_s[...]
        outs = []
        for h in range(ATT_HEADS):
            blk = on[h * rows_per_head:(h + 1) * rows_per_head]
            oh = blk[:ts] - lam * blk[ts:]
            outs.append(_rms(oh, SUBLN_EPS) * sub_ref[...] * (1.0 - lam_init))
        o_ref[...] = jnp.concatenate(outs, axis=-1)


def _decode_attn(q_rows, kn_t, vn, cache_kt, cache_v, page_table, lam_p, subln, layer, ts, lam_init):
    bs, nrow, _ = q_rows.shape
    n_pages = page_table.shape[1]
    pps = math.gcd(n_pages, DECODE_PAGES_PER_STEP)
    n_steps = n_pages // pps + 1
    kern = functools.partial(_decode_kernel, ts=ts, n_steps=n_steps, pps=pps, lam_init=lam_init)

    def page(which):
        return lambda b, p, pt: (pt[b, jnp.minimum(pps * p + which, n_pages - 1)], layer, 0, 0)

    k_specs = [pl.BlockSpec((None, None, ATT_W, PAGE), page(w)) for w in range(pps)]
    v_specs = [pl.BlockSpec((None, None, PAGE * ATT_HEADS, HEAD_V), page(w)) for w in range(pps)]
    grid_spec = pltpu.PrefetchScalarGridSpec(
        num_scalar_prefetch=1,
        grid=(bs, n_steps),
        in_specs=[pl.BlockSpec((None, nrow, ATT_W), lambda b, p, pt: (b, 0, 0))] + k_specs + v_specs + [
            pl.BlockSpec((None, ATT_W, PAGE), lambda b, p, pt: (b, 0, 0)),
            pl.BlockSpec((None, PAGE * ATT_HEADS, HEAD_V), lambda b, p, pt: (b, 0, 0)),
            pl.BlockSpec((8, HEAD_QK), lambda b, p, pt: (0, 0)),
            pl.BlockSpec((1, HEAD_V), lambda b, p, pt: (0, 0))],
        out_specs=pl.BlockSpec((None, ts, ATT_W), lambda b, p, pt: (b, 0, 0)),
        scratch_shapes=[pltpu.VMEM((nrow, ATT_W), BF16), pltpu.VMEM((nrow, 1), F32),
                        pltpu.VMEM((nrow, 1), F32), pltpu.VMEM((nrow, HEAD_V), F32)],
    )
    return pl.pallas_call(
        kern, grid_spec=grid_spec,
        out_shape=jax.ShapeDtypeStruct((bs, ts, ATT_W), F32),
        compiler_params=_cparams(("arbitrary", "arbitrary")),
    )(page_table, q_rows, *([cache_kt] * pps), *([cache_v] * pps), kn_t, vn, lam_p, subln)


def _merge_kernel(x_ref, y_ref, bon_ref, g_ref, oatt_ref, gates_ref, gt_ref, ln_ref,
                  wrw_ref, watt_ref, wo_ref, o_ref):
    y = y_ref[...]
    n = y.shape[0]
    parts = []
    for h in range(RW_HEADS):
        yh = y[:, h * RW_HEAD:(h + 1) * RW_HEAD]
        mu = jnp.mean(yh, axis=-1, keepdims=True)
        d = yh - mu
        var = jnp.mean(d * d, axis=-1, keepdims=True)
        parts.append(d * lax.rsqrt(var + RW_GN_EPS))
    yn = jnp.concatenate(parts, axis=-1) * ln_ref[0:1, :] + ln_ref[1:2, :]
    out_rw = (yn + bon_ref[...]) * g_ref[...]
    y_rw = _dot(out_rw, wrw_ref[...])
    y_att = _dot(oatt_ref[...], watt_ref[...])
    gates = gates_ref[...]
    merged = (jax.nn.sigmoid(gates[:, :D_MODEL]) * y_rw
              + jax.nn.sigmoid(gates[:, D_MODEL:]) * y_att)
    o_ref[...] = x_ref[...] + gt_ref[...] * _dot(merged, wo_ref[...])


def _merge(x, y, bon, g, oatt, proj, gt, ln, wrw, watt, wo, tm):
    N = x.shape[0]
    tpg = (N // gt.arr.shape[1]) // tm
    s512 = pl.BlockSpec((tm, RW_W), lambda i: (i, 0))
    full = lambda a: pl.BlockSpec(a.shape, lambda i: (0,) * a.ndim)
    return pl.pallas_call(
        _merge_kernel,
        grid=(N // tm,),
        in_specs=[pl.BlockSpec((tm, D_MODEL), lambda i: (i, 0)), s512, s512, s512, s512,
                  pl.BlockSpec((tm, 2 * D_MODEL), lambda i: (i, 0)),
                  _mod_spec(gt, tm, tpg), full(ln), full(wrw), full(watt), full(wo)],
        out_specs=pl.BlockSpec((tm, D_MODEL), lambda i: (i, 0)),
        out_shape=jax.ShapeDtypeStruct((N, D_MODEL), F32),
        compiler_params=_cparams(("arbitrary",)),
    )(x, y, bon, g, oatt, proj, gt.arr, ln, wrw, watt, wo)


def _first_argmax(cols):
    best = cols[0]
    idx = jnp.zeros(best.shape, jnp.int32)
    for n, c in enumerate(cols[1:], start=1):
        take = c > best
        best = jnp.where(take, c, best)
        idx = jnp.where(take, n, idx)
    return best, idx


def _router_kernel(x_ref, g_ref, sc_ref, sh_ref, wr_ref, eb_ref, h_ref, gate_ref):
    y = _rms(x_ref[...], NORM_EPS) * g_ref[...]
    h = y * (1.0 + sc_ref[...]) + sh_ref[...]
    h_hi = h.astype(BF16)
    h_ref[...] = h_hi
    h_lo = (h - h_hi.astype(F32)).astype(BF16)
    w = wr_ref[...]
    w_hi = w.astype(BF16)
    w_lo = (w - w_hi.astype(F32)).astype(BF16)
    dot = functools.partial(jnp.dot, preferred_element_type=F32)
    logits = dot(h_hi, w_hi) + (dot(h_lo, w_hi) + dot(h_hi, w_lo))
    s = jax.nn.sigmoid(logits.T[:N_EXPERTS])
    sel = s + eb_ref[...][:N_EXPERTS]
    sc_cols = [sel[e:e + 1, :] for e in range(N_EXPERTS)]
    s_cols = [s[e:e + 1, :] for e in range(N_EXPERTS)]
    grp = []
    for gi in range(N_GROUPS):
        cols = sc_cols[gi * EPG:(gi + 1) * EPG]
        m1, i1 = _first_argmax(cols)
        rest = [jnp.where(i1 == n, -jnp.inf, c) for n, c in enumerate(cols)]
        m2, _ = _first_argmax(rest)
        grp.append(m1 + m2)
    _, g_idx = _first_argmax(grp)
    pick = lambda cols_all, n: sum(jnp.where(g_idx == gi, cols_all[gi * EPG + n], 0.0)
                                   for gi in range(N_GROUPS))
    sel_g = [pick(sc_cols, n) for n in range(EPG)]
    s_g = [pick(s_cols, n) for n in range(EPG)]
    _, loc1 = _first_argmax(sel_g)
    _, loc2 = _first_argmax([jnp.where(loc1 == n, -jnp.inf, c) for n, c in enumerate(sel_g)])
    w_1 = sum(jnp.where(loc1 == n, s_g[n], 0.0) for n in range(EPG))
    w_2 = sum(jnp.where(loc2 == n, s_g[n], 0.0) for n in range(EPG))
    tot = w_1 + w_2
    w_1, w_2 = w_1 / tot, w_2 / tot
    e1 = g_idx * EPG + loc1
    e2 = g_idx * EPG + loc2
    expert = lax.broadcasted_iota(jnp.int32, (logits.shape[1], logits.shape[0]), 0)
    gate_t = jnp.where(expert == e1, w_1, 0.0) + jnp.where(expert == e2, w_2, 0.0)
    gate_ref[...] = gate_t.T


def _router(x, g, sc, sh, wr, eb, tm):
    N, D = x.shape
    tpg = (N // sc.arr.shape[1]) // tm
    return pl.pallas_call(
        _router_kernel,
        grid=(N // tm,),
        in_specs=[pl.BlockSpec((tm, D), lambda i: (i, 0)),
                  pl.BlockSpec((1, D), lambda i: (0, 0)),
                  _mod_spec(sc, tm, tpg), _mod_spec(sh, tm, tpg),
                  pl.BlockSpec((D, 128), lambda i: (0, 0)),
                  pl.BlockSpec((128, 1), lambda i: (0, 0))],
        out_specs=[pl.BlockSpec((tm, D), lambda i: (i, 0)),
                   pl.BlockSpec((tm, 128), lambda i: (i, 0))],
        out_shape=[jax.ShapeDtypeStruct((N, D), BF16), jax.ShapeDtypeStruct((N, 128), F32)],
        compiler_params=_cparams(("arbitrary",)),
    )(x, g, sc.arr, sh.arr, wr, eb)


def _moe_kernel(x_ref, h_ref, gate_ref, gt_ref, w1_ref, w3_ref, w2_ref, o_ref, acc_ref, act_ref):
    grp = pl.program_id(1)
    n_e = w1_ref.shape[0]

    @pl.when(grp == 0)
    def _():
        acc_ref[...] = jnp.zeros(acc_ref.shape, F32)

    h = h_ref[...]
    gate = gate_ref[...]
    lane = lax.broadcasted_iota(jnp.int32, gate.shape, 1)
    for el in range(n_e):
        a1 = jnp.dot(h, w1_ref[el], preferred_element_type=F32)
        a3 = jnp.dot(h, w3_ref[el], preferred_element_type=F32)
        gcol = jnp.sum(jnp.where(lane == grp * n_e + el, gate, 0.0), axis=-1, keepdims=True)
        act_ref[:, el * D_EXPERT:(el + 1) * D_EXPERT] = (
            (a1 * jax.nn.sigmoid(a1)) * a3 * gcol).astype(BF16)
    w2 = w2_ref[...].reshape(n_e * D_EXPERT, D_MODEL)
    acc_ref[...] += jnp.dot(act_ref[...], w2, preferred_element_type=F32)

    @pl.when(grp == pl.num_programs(1) - 1)
    def _():
        o_ref[...] = x_ref[...] + gt_ref[...] * acc_ref[...]


def _moe(x, h, gate, gt, w1, w3, w2, tm):
    N, D = x.shape
    tpg = (N // gt.arr.shape[1]) // tm
    layer = gt.layer
    eps = MOE_EXPERTS_PER_STEP
    return pl.pallas_call(
        _moe_kernel,
        grid=(N // tm, N_EXPERTS // eps),
        in_specs=[pl.BlockSpec((tm, D), lambda i, e: (i, 0)),
                  pl.BlockSpec((tm, D), lambda i, e: (i, 0)),
                  pl.BlockSpec((tm, 128), lambda i, e: (i, 0)),
                  _mod_spec(gt, tm, tpg),
                  pl.BlockSpec((None, eps, D, D_EXPERT), lambda i, e: (layer, e, 0, 0)),
                  pl.BlockSpec((None, eps, D, D_EXPERT), lambda i, e: (layer, e, 0, 0)),
                  pl.BlockSpec((None, eps, D_EXPERT, D), lambda i, e: (layer, e, 0, 0))],
        out_specs=pl.BlockSpec((tm, D), lambda i, e: (i, 0)),
        out_shape=jax.ShapeDtypeStruct((N, D), F32),
        scratch_shapes=[pltpu.VMEM((tm, D), F32), pltpu.VMEM((tm, eps * D_EXPERT), BF16)],
        compiler_params=_cparams(("arbitrary", "arbitrary")),
    )(x, h, gate, gt.arr, w1, w3, w2)


def _final_norm_kernel(x_ref, g_ref, o_ref):
    o_ref[...] = _rms(x_ref[...], NORM_EPS) * g_ref[...]


def _final_norm(x, g, tm):
    N, D = x.shape
    return pl.pallas_call(
        _final_norm_kernel,
        grid=(N // tm,),
        in_specs=[pl.BlockSpec((tm, D), lambda i: (i, 0)), pl.BlockSpec((1, D), lambda i: (0, 0))],
        out_specs=pl.BlockSpec((tm, D), lambda i: (i, 0)),
        out_shape=jax.ShapeDtypeStruct((N, D), F32),
        compiler_params=_cparams(("arbitrary",)),
    )(x, g)


def _group_layer(x, mods, lw, cfg, z0, s0, attend, stack):
    sh1, sc1, gt1, sh2, sc2, gt2 = mods
    tm = cfg["tm"]
    proj = _nm_matmul(x, lw["norm1_g"], sc1, sh1, lw["w_in"], cfg["tm_in"], cfg["tn_in"])
    r, lwd, kk, ka, k2, v, g, bon = _rw_prep(proj, z0, lw["mu"], lw["rw_vecs"], lw["wup"],
                                               cfg["tm_prep"], cfg["seq"])
    y, s_fin = _wkv(r, lwd, kk, ka, k2, v, s0, cfg["n_seq"], cfg["tb"], cfg["chunk"])
    qs, kr, kt_all, v_all = _rope(proj, cfg["rope_tabs"], cfg["tm_prep"], stack)
    oatt = attend(qs, kr, proj)
    x = _merge(x, y, bon, g, oatt, proj, gt1, lw["ln"], lw["w_rw_out"], lw["w_att_out"],
               lw["w_o"], tm)
    h2, gate = _router(x, lw["norm2_g"], sc2, sh2, lw["w_router"], lw["e_bias"], tm)
    x = _moe(x, h2, gate, gt2, lw["moe_w1"], lw["moe_w3"], lw["moe_w2"], cfg["tm_moe"])
    return x, proj, kt_all, v_all, s_fin


def kernel(x_prompt, x_sample, c_prompt, c_sample, cache_k, cache_v, page_table, state_shift, state_wkv, w_ada, b_ada, norm1_g, norm2_g, w_in, rw_mu, rw_w0, rw_w_up, rw_a0, rw_a_up, rw_g_up, rw_k_k, rw_k_a, rw_r_k, rw_ln_w, rw_ln_b, w_rw_out, att_lq1, att_lk1, att_lq2, att_lk2, att_subln, w_att_out, w_o, w_router, e_bias, moe_w1, moe_w3, moe_w2, normf_g):
    bp, tp, D = x_prompt.shape
    bs, ts, _ = x_sample.shape
    depth = w_in.shape[0]
    n_pages = page_table.shape[1]
    past_len = n_pages * PAGE
    n_p, n_s = bp * tp, bs * ts

    z_w, q_w, k_w, v_w, grw_w, gatt_w = jnp.split(
        w_in, [RW_PROJ, RW_PROJ + ATT_W, RW_PROJ + 2 * ATT_W, RW_PROJ + 3 * ATT_W,
               RW_PROJ + 3 * ATT_W + D], axis=-1)
    w_in_p = jnp.concatenate(
        [grw_w, gatt_w, q_w, k_w, v_w, z_w,
         jnp.zeros((depth, D, RW_PROJ_PAD - RW_PROJ), F32)], axis=-1).astype(BF16)
    mu_p = jnp.pad(rw_mu, ((0, 0), (0, RW_PROJ_PAD - RW_PROJ)))
    wup = jnp.zeros((depth, 3, LORA_PAD, RW_W), F32)
    wup = wup.at[:, 0, 0:DECAY_LORA].set(rw_w_up)
    wup = wup.at[:, 1, DECAY_LORA:DECAY_LORA + AAA_LORA].set(rw_a_up)
    wup = wup.at[:, 2, DECAY_LORA + AAA_LORA:DECAY_LORA + AAA_LORA + GATE_LORA].set(rw_g_up)
    wup = wup.astype(BF16)
    rw_vecs = jnp.stack([rw_w0, rw_a0, rw_k_k, rw_k_a, rw_r_k.reshape(depth, RW_W),
                         jnp.zeros_like(rw_w0), jnp.zeros_like(rw_w0), jnp.zeros_like(rw_w0)], axis=1)
    ln = jnp.stack([rw_ln_w, rw_ln_b], axis=1)
    lam_p = jnp.stack([att_lq1, att_lk1, att_lq2, att_lk2] + [jnp.zeros_like(att_lq1)] * 4, axis=1)
    w_router_p = jnp.pad(w_router, ((0, 0), (0, 128 - N_EXPERTS)))
    e_bias_p = jnp.pad(e_bias, (0, 128 - N_EXPERTS)).reshape(128, 1)
    w_rw_out_b, w_att_out_b, w_o_b = (w.astype(BF16) for w in (w_rw_out, w_att_out, w_o))
    moe_w1_b, moe_w3_b, moe_w2_b = (w.astype(BF16) for w in (moe_w1, moe_w3, moe_w2))

    n_c = bp + bs
    n_cp = -(-n_c // 8) * 8
    c_all = jnp.pad(jnp.concatenate([c_prompt, c_sample], axis=0), ((0, n_cp - n_c), (0, 0)))
    mod = _ada_mod(c_all, w_ada, b_ada)
    mod_p = mod[:, :bp].reshape(depth, bp, 1, 6 * D)
    mod_s = jnp.repeat(mod[:, bp:n_c], ts, axis=1).reshape(depth, 1, n_s, 6 * D)

    tm_p = min(512, tp)
    cfg_p = dict(tm=tm_p, tm_in=min(1024, tp), tn_in=1792, tm_prep=min(256, tp), seq=tp, n_seq=bp,
                 tb=min(256, tp), chunk=WKV_CHUNK, tm_moe=min(512, tp),
                 rope_tabs=_rope_tables(jnp.arange(tp)))
    cfg_s = dict(tm=n_s, tm_in=n_s, tn_in=1792, tm_prep=n_s, seq=ts, n_seq=bs, tb=8, chunk=8, tm_moe=n_s,
                 rope_tabs=_rope_tables(jnp.tile(past_len + jnp.arange(ts), bs)))

    cache_kt = jnp.transpose(cache_k, (0, 1, 3, 4, 5, 2)).reshape(cache_k.shape[0], depth, ATT_W, PAGE)
    cache_v2 = cache_v.reshape(cache_v.shape[0], depth, PAGE * ATT_HEADS, HEAD_V)
    zp0 = jnp.zeros((bp, 1, RW_PROJ_PAD), F32)
    sp0 = jnp.zeros((bp, RW_HEADS, RW_HEAD, RW_HEAD), F32)

    xp = x_prompt.reshape(n_p, D)
    xs = x_sample.reshape(n_s, D)
    outs = {k: [] for k in ("zp", "sp", "ks", "vs", "zs", "ss")}
    kt_all = v_all = None
    for l in range(depth):
        lam_init = 0.8 - 0.6 * math.exp(-0.3 * l)
        lw = dict(norm1_g=norm1_g[l].reshape(1, D), norm2_g=norm2_g[l].reshape(1, D),
                  w_in=w_in_p, mu=mu_p[l].reshape(1, RW_PROJ_PAD), rw_vecs=rw_vecs[l],
                  wup=wup[l], ln=ln[l], w_rw_out=w_rw_out_b[l], w_att_out=w_att_out_b[l],
                  w_o=w_o_b[l], w_router=w_router_p, e_bias=e_bias_p,
                  moe_w1=moe_w1_b, moe_w3=moe_w3_b, moe_w2=moe_w2_b)
        subln = att_subln[l].reshape(1, HEAD_V)
        mods_p = [_Mod(mod_p, l, sec) for sec in range(6)]
        mods_s = [_Mod(mod_s, l, sec) for sec in range(6)]

        attend_p = lambda qs, kr, proj: _flash(qs, kr, proj, lam_p[l], subln, bp,
                                               min(512, tp), lam_init)
        xp, proj_p, kt_all, v_all, s_fin = _group_layer(xp, mods_p, lw, cfg_p, zp0, sp0, attend_p,
                                                         (l, depth, bp, kt_all, v_all))
        outs["zp"].append(proj_p.reshape(bp, tp, IN_W_PAD)[:, -1, COL_Z:COL_Z + RW_PROJ])
        outs["sp"].append(s_fin)

        def attend_s(qs, kr, proj):
            q_rows = jnp.tile(qs.reshape(bs, ts, ATT_W), (1, 2 * ATT_HEADS, 1))
            kn_t = jnp.pad(jnp.swapaxes(kr.reshape(bs, ts, ATT_W), 1, 2),
                           ((0, 0), (0, 0), (0, PAGE - ts)))
            vn = jnp.pad(proj[:, COL_V:COL_V + ATT_W].reshape(bs, ts * ATT_HEADS, HEAD_V),
                         ((0, 0), (0, (PAGE - ts) * ATT_HEADS), (0, 0)))
            return _decode_attn(q_rows, kn_t, vn, cache_kt, cache_v2, page_table, lam_p[l], subln, l,
                                ts, lam_init).reshape(n_s, ATT_W)

        z0_s = jnp.pad(jnp.repeat(state_shift[l], ts, axis=0), ((0, 0), (0, RW_PROJ_PAD - RW_PROJ)))
        xs, proj_s, kr_s, s_fin = _sample_layer(xs, mods_s, lw, cfg_s, z0_s, state_wkv[l], attend_s, bs, ts)
        outs["ks"].append(kr_s.reshape(bs, ts, ATT_HEADS, 2, HEAD_QK))
        outs["vs"].append(proj_s[:, COL_V:COL_V + ATT_W].reshape(bs, ts, ATT_HEADS, HEAD_V))
        outs["zs"].append(proj_s.reshape(bs, ts, IN_W_PAD)[:, -1, COL_Z:COL_Z + RW_PROJ])
        outs["ss"].append(s_fin)

    y_prompt = _final_norm(xp, normf_g.reshape(1, D), tm_p).reshape(bp, tp, D)
    y_sample = _final_norm(xs, normf_g.reshape(1, D), n_s).reshape(bs, ts, D)
    k_prompt = jnp.transpose(kt_all.reshape(bp, depth, ATT_HEADS, 2, HEAD_QK, tp), (0, 1, 5, 2, 3, 4))
    v_prompt = v_all.reshape(bp, depth, tp, ATT_HEADS, HEAD_V)
    return (y_prompt, y_sample, k_prompt, v_prompt,
            jnp.stack(outs["zp"], axis=0), jnp.stack(outs["sp"], axis=0),
            jnp.stack(outs["ks"], axis=1), jnp.stack(outs["vs"], axis=1),
            jnp.stack(outs["zs"], axis=0), jnp.stack(outs["ss"], axis=0))


def _sample_layer(x, mods, lw, cfg, z0, s0, attend, bs, ts):
    sh1, sc1, gt1, sh2, sc2, gt2 = mods
    tm = cfg["tm"]
    proj = _nm_matmul(x, lw["norm1_g"], sc1, sh1, lw["w_in"], cfg["tm_in"], cfg["tn_in"])
    prep = _rw_prep(proj, z0, lw["mu"], lw["rw_vecs"], lw["wup"], cfg["tm_prep"], cfg["seq"])
    r, lwd, kk, ka, k2, v, g, bon = prep
    pad8 = lambda a: jnp.pad(a.reshape(bs, ts, RW_W), ((0, 0), (0, 8 - ts), (0, 0))).reshape(bs * 8, RW_W)
    y8, s_fin = _wkv(*(pad8(a) for a in (r, lwd, kk, ka, k2, v)), s0, bs, 8, 8)
    y = y8.reshape(bs, 8, RW_W)[:, :ts].reshape(bs * ts, RW_W)
    qs, kr = _rope(proj, cfg["rope_tabs"], cfg["tm_prep"])
    oatt = attend(qs, kr, proj)
    x = _merge(x, y, bon, g, oatt, proj, gt1, lw["ln"], lw["w_rw_out"], lw["w_att_out"],
               lw["w_o"], tm)
    h2, gate = _router(x, lw["norm2_g"], sc2, sh2, lw["w_router"], lw["e_bias"], tm)
    x = _moe(x, h2, gate, gt2, lw["moe_w1"], lw["moe_w3"], lw["moe_w2"], cfg["tm_moe"])
    return x, proj, kr, s_fin
```

```python
import functools
import math
from typing import NamedTuple

import jax
import jax.numpy as jnp
from jax import lax
from jax.experimental import pallas as pl
from jax.experimental.pallas import tpu as pltpu

F32 = jnp.float32
BF16 = jnp.bfloat16

D_MODEL = 1024
RW_HEAD = 64
RW_HEADS = 8
RW_W = RW_HEADS * RW_HEAD
DECAY_LORA = 32
AAA_LORA = 32
GATE_LORA = 96
RW_PROJ = 3 * RW_W + DECAY_LORA + AAA_LORA + GATE_LORA
RW_PROJ_PAD = 1792
LORA_PAD = RW_PROJ_PAD - 3 * RW_W
RW_GN_EPS = 64e-5
ATT_HEADS = 4
HEAD_QK = 64
HEAD_V = 128
ATT_W = 512
ROPE_DIM = HEAD_QK // 4
ROPE_THETA = 500000.0
SUBLN_EPS = 1e-5
NEG_INF = -1e30
N_EXPERTS = 16
N_GROUPS = 4
EPG = 4
D_EXPERT = 512
NORM_EPS = 1e-6
PAGE = 128
DECODE_PAGES_PER_STEP = 8
WKV_SEQS_PER_STEP = 2
MOE_EXPERTS_PER_STEP = 2
FLASH_HEADS_PER_STEP = 4
WKV_CHUNK = 64

COL_GATE = 0
COL_Q = 2048
COL_V = 3072
COL_Z = 3584
IN_W_PAD = COL_Z + RW_PROJ_PAD

VMEM_LIMIT = 56 * 1024 * 1024

_NN = (((1,), (0,)), ((), ()))
_NT = (((1,), (1,)), ((), ()))
_TN = (((0,), (0,)), ((), ()))


def _dot(a, b, dims=_NN):
    return lax.dot_general(a.astype(BF16), b.astype(BF16), dims, preferred_element_type=F32)


def _dot_hi(a, b, dims=_NN):
    return lax.dot_general(a, b, dims, preferred_element_type=F32,
                           precision=lax.Precision.HIGHEST)


def _cparams(sem):
    return pltpu.CompilerParams(dimension_semantics=sem, vmem_limit_bytes=VMEM_LIMIT)


def _rms(x, eps):
    return x * lax.rsqrt(jnp.mean(x * x, axis=-1, keepdims=True) + eps)


def _ada_kernel(c_ref, w_ref, b_ref, o_ref):
    c = c_ref[...]
    sc = c * jax.nn.sigmoid(c)
    o_ref[...] = _dot(sc, w_ref[...]) + b_ref[...]


def _ada_mod(c_all, w_ada, b_ada):
    L, D, N6 = w_ada.shape
    M = c_all.shape[0]
    tn = 1536
    return pl.pallas_call(
        _ada_kernel,
        grid=(L, N6 // tn),
        in_specs=[pl.BlockSpec((M, D), lambda l, j: (0, 0)),
                  pl.BlockSpec((None, D, tn), lambda l, j: (l, 0, j)),
                  pl.BlockSpec((None, 1, tn), lambda l, j: (l, 0, j))],
        out_specs=pl.BlockSpec((None, M, tn), lambda l, j: (l, 0, j)),
        out_shape=jax.ShapeDtypeStruct((L, M, N6), F32),
        compiler_params=_cparams(("arbitrary", "arbitrary")),
    )(c_all, w_ada, b_ada.reshape(L, 1, N6))


class _Mod(NamedTuple):
    arr: jax.Array
    layer: int
    sec: int


def _mod_spec(m, tm, tiles_per_group):
    rows = m.arr.shape[2]
    return pl.BlockSpec((None, None, rows, D_MODEL),
                        lambda i, *_: (m.layer, i // tiles_per_group, 0, m.sec))


def _nm_matmul_kernel(x_ref, g_ref, sc_ref, sh_ref, w_ref, o_ref, h_ref):
    @pl.when(pl.program_id(1) == 0)
    def _():
        y = _rms(x_ref[...], NORM_EPS) * g_ref[...]
        h_ref[...] = (y * (1.0 + sc_ref[...]) + sh_ref[...]).astype(BF16)

    o_ref[...] = jnp.dot(h_ref[...], w_ref[...], preferred_element_type=F32)


def _nm_matmul(x, g, sc, sh, w, tm, tn):
    N, D = x.shape
    n_out = w.shape[2]
    tpg = (N // sc.arr.shape[1]) // tm
    layer = sc.layer
    return pl.pallas_call(
        _nm_matmul_kernel,
        grid=(N // tm, n_out // tn),
        in_specs=[pl.BlockSpec((tm, D), lambda i, j: (i, 0)),
                  pl.BlockSpec((1, D), lambda i, j: (0, 0)),
                  _mod_spec(sc, tm, tpg), _mod_spec(sh, tm, tpg),
                  pl.BlockSpec((None, D, tn), lambda i, j: (layer, 0, j))],
        out_specs=pl.BlockSpec((tm, tn), lambda i, j: (i, j)),
        out_shape=jax.ShapeDtypeStruct((N, n_out), F32),
        scratch_shapes=[pltpu.VMEM((tm, D), BF16)],
        compiler_params=_cparams(("arbitrary", "arbitrary")),
    )(x, g, sc.arr, sh.arr, w)


def _head_sum(x):
    parts = []
    for h in range(RW_HEADS):
        s = jnp.sum(x[:, h * RW_HEAD:(h + 1) * RW_HEAD], axis=-1, keepdims=True)
        parts.append(jnp.broadcast_to(s, (x.shape[0], RW_HEAD)))
    return jnp.concatenate(parts, axis=-1)


def _rw_prep_kernel(z_ref, z0_ref, mu_ref, vec_ref, wup_ref,
                    r_ref, lw_ref, kk_ref, ka_ref, k2_ref, v_ref, g_ref, bon_ref,
                    carry_ref, *, tm, seq_len):
    i = pl.program_id(0)
    z = z_ref[...]
    rolled = pltpu.roll(z, 1, axis=0)
    row = lax.broadcasted_iota(jnp.int32, z.shape, 0)
    if seq_len >= tm:
        tiles_per_seq = seq_len // tm
        first = jnp.where(i % tiles_per_seq == 0, z0_ref[...], carry_ref[0:1, :])
        z_prev = jnp.where(row == 0, first, rolled)
        carry_ref[0:1, :] = z[tm - 1:tm, :]
    else:
        z_prev = jnp.where(row % seq_len == 0, z0_ref[...], rolled)
    zs = z + (z_prev - z) * mu_ref[...]
    r = zs[:, 0:RW_W]
    k = zs[:, RW_W:2 * RW_W]
    v = zs[:, 2 * RW_W:3 * RW_W]
    tail = zs[:, 3 * RW_W:]
    w0, a0, k_k, k_a, r_k = (vec_ref[j:j + 1, :] for j in range(5))
    dw = _dot(jnp.tanh(tail), wup_ref[0])
    da = _dot(tail, wup_ref[1])
    g = _dot(jax.nn.sigmoid(tail), wup_ref[2])
    t = -(w0 + dw)
    softplus = jnp.maximum(t, 0.0) + jnp.log1p(jnp.exp(-jnp.abs(t)))
    lw = -jnp.exp(-softplus - 0.5)
    a = jax.nn.sigmoid(a0 + da)
    kk = k * k_k
    kk = kk / jnp.maximum(jnp.sqrt(_head_sum(kk * kk)), 1e-12)
    k2 = k * (1.0 + (a - 1.0) * k_a)
    r_ref[...] = r
    lw_ref[...] = lw
    kk_ref[...] = kk
    ka_ref[...] = kk * a
    k2_ref[...] = k2
    v_ref[...] = v
    g_ref[...] = g
    bon_ref[...] = _head_sum(r * k2 * r_k) * v


def _rw_prep(proj, z0, mu, vecs, wup, tm, seq_len):
    N = proj.shape[0]
    kern = functools.partial(_rw_prep_kernel, tm=tm, seq_len=seq_len)
    if seq_len >= tm:
        tps = seq_len // tm
        z0_spec = pl.BlockSpec((None, 1, RW_PROJ_PAD), lambda i: (i // tps, 0, 0))
    else:
        z0_spec = pl.BlockSpec((tm, RW_PROJ_PAD), lambda i: (i, 0))
    o_spec = pl.BlockSpec((tm, RW_W), lambda i: (i, 0))
    return pl.pallas_call(
        kern,
        grid=(N // tm,),
        in_specs=[pl.BlockSpec((tm, RW_PROJ_PAD), lambda i: (i, COL_Z // RW_PROJ_PAD)),
                  z0_spec,
                  pl.BlockSpec((1, RW_PROJ_PAD), lambda i: (0, 0)),
                  pl.BlockSpec((8, RW_W), lambda i: (0, 0)),
                  pl.BlockSpec((3, LORA_PAD, RW_W), lambda i: (0, 0, 0))],
        out_specs=[o_spec] * 8,
        out_shape=[jax.ShapeDtypeStruct((N, RW_W), F32)] * 8,
        scratch_shapes=[pltpu.VMEM((8, RW_PROJ_PAD), F32)],
        compiler_params=_cparams(("arbitrary",)),
    )(proj, z0, mu, vecs, wup)


def _wkv_kernel(r_ref, lw_ref, kk_ref, ka_ref, k2_ref, v_ref, s0_ref, y_ref, sout_ref,
                s_ref, *, tb, C, sps):
    tblk = pl.program_id(1)
    n_pair = RW_HEADS // 2
    zero64 = jnp.zeros((RW_HEAD, RW_HEAD), F32)
    units = [(q, p) for q in range(sps) for p in range(n_pair)]
    nu = range(len(units))

    @pl.when(tblk == 0)
    def _():
        for n, (q, p) in enumerate(units):
            s_ref[n] = jnp.concatenate(
                [jnp.concatenate([s0_ref[q, 2 * p], zero64], axis=1),
                 jnp.concatenate([zero64, s0_ref[q, 2 * p + 1]], axis=1)], axis=0)

    C2 = 2 * C
    ri = lax.broadcasted_iota(jnp.int32, (C, C), 0)
    ci = lax.broadcasted_iota(jnp.int32, (C, C), 1)
    tri = (ri >= ci).astype(F32)
    r2 = lax.broadcasted_iota(jnp.int32, (C2, C2), 0)
    c2 = lax.broadcasted_iota(jnp.int32, (C2, C2), 1)
    low_incl = r2 >= c2
    low_strict = r2 > c2
    eye = (r2 == c2).astype(F32)
    lo_lanes = lax.broadcasted_iota(jnp.int32, (C, 128), 1) < RW_HEAD
    n_sq = int(math.log2(C)) - 1

    def stack(x, p):
        xp = x[:, p * 128:(p + 1) * 128]
        return jnp.concatenate([jnp.where(lo_lanes, xp, 0.0), jnp.where(lo_lanes, 0.0, xp)], axis=0)

    def chunk(c, carry):
        rows = pl.ds(pl.multiple_of(c * C, C), C)
        a_t, r_t, b_t, k_t, b_r, k_r, v_q, p_tot = [], [], [], [], [], [], [], []
        for q in range(sps):
            lw = lw_ref[q, rows, :]
            kk = kk_ref[q, rows, :]
            ka = ka_ref[q, rows, :]
            k2 = k2_ref[q, rows, :]
            cs = _dot_hi(tri, lw)
            tot = cs[C - 1:C, :]
            p_inv = jnp.exp(-cs)
            p_rem = jnp.exp(tot - cs)
            a_t.append(-kk * jnp.exp(cs - lw))
            r_t.append(r_ref[q, rows, :] * jnp.exp(cs))
            b_t.append(ka * p_inv)
            k_t.append(k2 * p_inv)
            b_r.append(ka * p_rem)
            k_r.append(k2 * p_rem)
            v_q.append(v_ref[q, rows, :])
            p_tot.append(jnp.exp(tot))
        a_s = [stack(a_t[q], p).astype(BF16) for q, p in units]
        r_s = [stack(r_t[q], p).astype(BF16) for q, p in units]
        b_s = [stack(b_t[q], p).astype(BF16) for q, p in units]
        k_s = [stack(k_t[q], p).astype(BF16) for q, p in units]
        bk_r = [jnp.concatenate([stack(b_r[q], p), stack(k_r[q], p)], axis=0).astype(BF16)
                for q, p in units]
        v_s = [stack(v_q[q], p).astype(BF16) for q, p in units]
        ar = [jnp.concatenate([a_s[n], r_s[n]], axis=0) for n in nu]
        gb = [_dot(ar[n], b_s[n], _NT) for n in nu]
        gk = [_dot(ar[n], k_s[n], _NT) for n in nu]
        mb = [jnp.where(low_strict, gb[n][:C2], 0.0) for n in nu]
        nb = [jnp.where(low_incl, gb[n][C2:], 0.0) for n in nu]
        mk = [jnp.where(low_strict, gk[n][:C2], 0.0) for n in nu]
        nk = [jnp.where(low_incl, gk[n][C2:], 0.0) for n in nu]
        tinv = [eye + mb[n] for n in nu]
        pw = [_dot(mb[n], mb[n]) for n in nu]
        for lvl in range(n_sq):
            if lvl < n_sq - 1:
                both = [_dot(jnp.concatenate([tinv[n], pw[n]], axis=0), pw[n]) for n in nu]
                tinv = [tinv[n] + both[n][:C2] for n in nu]
                pw = [both[n][C2:] for n in nu]
            else:
                tinv = [tinv[n] + _dot(tinv[n], pw[n]) for n in nu]
        mkv = [_dot(mk[n], v_s[n]) for n in nu]
        x = [_dot(tinv[n], jnp.concatenate([a_s[n], mkv[n].astype(BF16)], axis=1)) for n in nu]
        s_old = [s_ref[n] for n in nu]
        s_bf = [s.astype(BF16) for s in s_old]
        u = [_dot(x[n][:, :128], s_bf[n], _NT) + x[n][:, 128:] for n in nu]
        y = [_dot(r_s[n], s_bf[n], _NT) + _dot(nb[n], u[n]) + _dot(nk[n], v_s[n]) for n in nu]
        for n, (q, p) in enumerate(units):
            uv = jnp.concatenate([u[n].astype(BF16), v_s[n]], axis=0)
            s_ref[n] = s_old[n] * p_tot[q][:, p * 128:(p + 1) * 128] + _dot(uv, bk_r[n], _TN)
            y_ref[q, rows, p * 128:(p + 1) * 128] = y[n][:C] + y[n][C:]
        return carry

    lax.fori_loop(0, tb // C, chunk, 0)

    @pl.when(tblk == pl.num_programs(1) - 1)
    def _():
        for n, (q, p) in enumerate(units):
            s = s_ref[n]
            sout_ref[q, 2 * p] = s[:RW_HEAD, :RW_HEAD]
            sout_ref[q, 2 * p + 1] = s[RW_HEAD:, RW_HEAD:]


def _wkv(r, lw, kk, ka, k2, v, s0, n_seq, tb, C):
    N = r.shape[0]
    T = N // n_seq
    sps = math.gcd(n_seq, WKV_SEQS_PER_STEP)
    kern = functools.partial(_wkv_kernel, tb=tb, C=C, sps=sps)
    in_spec = pl.BlockSpec((sps, tb, RW_W), lambda b, t: (b, t, 0))
    s_spec = pl.BlockSpec((sps, RW_HEADS, RW_HEAD, RW_HEAD), lambda b, t: (b, 0, 0, 0))
    seq = lambda a: a.reshape(n_seq, T, RW_W)
    y, s_fin = pl.pallas_call(
        kern,
        grid=(n_seq // sps, T // tb),
        in_specs=[in_spec] * 6 + [s_spec],
        out_specs=[in_spec, s_spec],
        out_shape=[jax.ShapeDtypeStruct((n_seq, T, RW_W), F32),
                   jax.ShapeDtypeStruct((n_seq, RW_HEADS, RW_HEAD, RW_HEAD), F32)],
        scratch_shapes=[pltpu.VMEM((sps * RW_HEADS // 2, 128, 128), F32)],
        compiler_params=_cparams(("arbitrary", "arbitrary")),
    )(seq(r), seq(lw), seq(kk), seq(ka), seq(k2), seq(v), s0)
    return y.reshape(N, RW_W), s_fin


def _rope_kernel(qk_ref, cos_ref, s1_ref, s2_ref, *rest, stacked):
    if stacked:
        v_ref, q_ref, k_ref, kt_ref, v4_ref = rest[0], *rest[-4:]
    else:
        q_ref, k_ref = rest
    cos, s1, s2 = cos_ref[...], s1_ref[...], s2_ref[...]
    half = ROPE_DIM // 2
    for dst, base, scale in ((q_ref, 0, HEAD_QK ** -0.5), (k_ref, ATT_W, 1.0)):
        for cblk in range(ATT_W // 128):
            x = qk_ref[:, base + cblk * 128: base + (cblk + 1) * 128]
            up = pltpu.roll(x, 128 - half, axis=1)
            dn = pltpu.roll(x, half, axis=1)
            y = x * cos + up * s1 + dn * s2
            dst[:, cblk * 128:(cblk + 1) * 128] = y * scale if scale != 1.0 else y
    if stacked:
        kt_ref[...] = k_ref[...].T
        rows = v_ref.shape[0]
        for h in range(ATT_HEADS):
            v4_ref[pl.ds(h, rows, stride=ATT_HEADS), :] = v_ref[:, h * HEAD_V:(h + 1) * HEAD_V]


def _rope(proj, tabs, tm, stack=None):
    N = proj.shape[0]
    ntab = tabs[0].shape[0] // tm
    t_spec = pl.BlockSpec((tm, 128), lambda i: (i % ntab, 0))
    o_spec = pl.BlockSpec((tm, ATT_W), lambda i: (i, 0))
    in_specs = [pl.BlockSpec((tm, 2 * ATT_W), lambda i: (i, COL_Q // (2 * ATT_W))),
                t_spec, t_spec, t_spec]
    out_specs = [o_spec, o_spec]
    out_shape = [jax.ShapeDtypeStruct((N, ATT_W), F32)] * 2
    operands = [proj, *tabs]
    aliases = {}
    if stack is not None:
        layer, depth, n_seq, kt_buf, v_buf = stack
        T = N // n_seq
        tps = T // tm
        in_specs.append(pl.BlockSpec((tm, ATT_W), lambda i: (i, COL_V // ATT_W)))
        operands.append(proj)
        out_specs += [pl.BlockSpec((None, None, ATT_W, tm), lambda i: (i // tps, layer, 0, i % tps)),
                      pl.BlockSpec((None, None, tm * ATT_HEADS, HEAD_V),
                                   lambda i: (i // tps, layer, i % tps, 0))]
        out_shape += [jax.ShapeDtypeStruct((n_seq, depth, ATT_W, T), F32),
                      jax.ShapeDtypeStruct((n_seq, depth, T * ATT_HEADS, HEAD_V), F32)]
        if kt_buf is not None:
            aliases = {len(operands): 2, len(operands) + 1: 3}
            in_specs += [pl.BlockSpec(memory_space=pl.ANY)] * 2
            operands += [kt_buf, v_buf]
    return pl.pallas_call(
        functools.partial(_rope_kernel, stacked=stack is not None),
        grid=(N // tm,),
        in_specs=in_specs,
        out_specs=out_specs,
        out_shape=out_shape,
        input_output_aliases=aliases,
        compiler_params=_cparams(("arbitrary",)),
    )(*operands)


def _rope_tables(pos):
    half = ROPE_DIM // 2
    inv = ROPE_THETA ** (-jnp.arange(0, ROPE_DIM, 2, dtype=F32) / ROPE_DIM)
    ang = pos.astype(F32)[:, None] * inv[None, :]
    cos, sin = jnp.cos(ang), jnp.sin(ang)
    n = pos.shape[0]
    one = jnp.ones((n, HEAD_QK - ROPE_DIM), F32)
    zero = jnp.zeros((n, HEAD_QK - ROPE_DIM), F32)
    zh = jnp.zeros((n, half), F32)
    c64 = jnp.concatenate([cos, cos, one], axis=1)
    s1 = jnp.concatenate([-sin, zh, zero], axis=1)
    s2 = jnp.concatenate([zh, sin, zero], axis=1)
    return tuple(jnp.concatenate([t, t], axis=1) for t in (c64, s1, s2))


def _lambda(lp_ref, lam_init):
    lp = lp_ref[...]
    d1 = jnp.sum(lp[0:1, :] * lp[1:2, :], axis=-1, keepdims=True)
    d2 = jnp.sum(lp[2:3, :] * lp[3:4, :], axis=-1, keepdims=True)
    return jnp.exp(d1) - jnp.exp(d2) + lam_init


def _flash_kernel(qi_ref, kj_ref, q_ref, k_ref, v_ref, lp_ref, sub_ref, o_ref,
                  qs_ref, m_ref, l_ref, acc_ref, *, tq, hps, lam_init):
    t = pl.program_id(2)
    i = qi_ref[t]
    j = kj_ref[t]
    heads = range(hps)
    cols = lambda hh: slice(hh * 128, (hh + 1) * 128)

    @pl.when(j == 0)
    def _():
        q = q_ref[...] * math.log2(math.e)
        lo = lax.broadcasted_iota(jnp.int32, (tq, 128), 1) < HEAD_QK
        for hh in heads:
            qh = q[:, cols(hh)]
            qs_ref[hh] = jnp.concatenate([jnp.where(lo, qh, 0.0), jnp.where(lo, 0.0, qh)],
                                         axis=0).astype(BF16)
        m_ref[...] = jnp.full(m_ref.shape, NEG_INF, F32)
        l_ref[...] = jnp.zeros(l_ref.shape, F32)
        acc_ref[...] = jnp.zeros(acc_ref.shape, F32)

    def step(diagonal):
        k = k_ref[...].astype(BF16)
        v = v_ref[...].astype(BF16)
        s = [lax.dot_general(k[:, cols(hh)], qs_ref[hh], _NT, preferred_element_type=F32)
             for hh in heads]
        if diagonal:
            key = lax.broadcasted_iota(jnp.int32, s[0].shape, 0)
            qry = lax.broadcasted_iota(jnp.int32, s[0].shape, 1)
            keep = key <= jnp.where(qry >= tq, qry - tq, qry)
            s = [jnp.where(keep, s[hh], NEG_INF) for hh in heads]
        m_old = [m_ref[hh] for hh in heads]
        m_new = [jnp.maximum(m_old[hh], jnp.max(s[hh], axis=0, keepdims=True)) for hh in heads]
        alpha = [jnp.exp2(m_old[hh] - m_new[hh]) for hh in heads]
        p = [jnp.exp2(s[hh] - m_new[hh]) for hh in heads]
        pv = [lax.dot_general(v[:, cols(hh)], p[hh].astype(BF16), _TN, preferred_element_type=F32)
              for hh in heads]
        for hh in heads:
            l_ref[hh] = alpha[hh] * l_ref[hh] + jnp.sum(p[hh], axis=0, keepdims=True)
            acc_ref[hh] = alpha[hh] * acc_ref[hh] + pv[hh]
            m_ref[hh] = m_new[hh]

    @pl.when(j < i)
    def _():
        step(False)

    @pl.when(j == i)
    def _():
        step(True)
        lam = _lambda(lp_ref, lam_init)
        for hh in heads:
            on = acc_ref[hh] / l_ref[hh]
            o_t = on[:, :tq] - lam * on[:, tq:]
            ms = jnp.mean(o_t * o_t, axis=0, keepdims=True)
            o_t = o_t * lax.rsqrt(ms + SUBLN_EPS) * sub_ref[...] * (1.0 - lam_init)
            o_ref[:, cols(hh)] = o_t.T


def _flash(qs, kr, proj, lam_p, subln, n_seq, tq, lam_init):
    N = qs.shape[0]
    nq = (N // n_seq) // tq
    hps = FLASH_HEADS_PER_STEP
    wid = hps * 128
    kern = functools.partial(_flash_kernel, tq=tq, hps=hps, lam_init=lam_init)
    vcol = COL_V // wid
    pairs = [(i, j) for i in range(nq) for j in range(i + 1)]
    qi = jnp.asarray([p[0] for p in pairs], jnp.int32)
    kj = jnp.asarray([p[1] for p in pairs], jnp.int32)
    grid_spec = pltpu.PrefetchScalarGridSpec(
        num_scalar_prefetch=2,
        grid=(n_seq, ATT_HEADS // hps, len(pairs)),
        in_specs=[pl.BlockSpec((tq, wid), lambda b, h, t, qi, kj: (b * nq + qi[t], h)),
                  pl.BlockSpec((tq, wid), lambda b, h, t, qi, kj: (b * nq + kj[t], h)),
                  pl.BlockSpec((tq, wid), lambda b, h, t, qi, kj: (b * nq + kj[t], vcol + h)),
                  pl.BlockSpec((8, HEAD_QK), lambda b, h, t, qi, kj: (0, 0)),
                  pl.BlockSpec((HEAD_V, 1), lambda b, h, t, qi, kj: (0, 0))],
        out_specs=pl.BlockSpec((tq, wid), lambda b, h, t, qi, kj: (b * nq + qi[t], h)),
        scratch_shapes=[pltpu.VMEM((hps, 2 * tq, 128), BF16), pltpu.VMEM((hps, 1, 2 * tq), F32),
                        pltpu.VMEM((hps, 1, 2 * tq), F32), pltpu.VMEM((hps, HEAD_V, 2 * tq), F32)],
    )
    return pl.pallas_call(
        kern, grid_spec=grid_spec,
        out_shape=jax.ShapeDtypeStruct((N, ATT_W), F32),
        compiler_params=_cparams(("arbitrary",) * 3),
    )(qi, kj, qs, kr, proj, lam_p, subln.reshape(HEAD_V, 1))


def _decode_kernel(pt_ref, q_ref, *refs, ts, n_steps, pps, lam_init):
    k_refs, v_refs = refs[:pps], refs[pps:2 * pps]
    kn_ref, vn_ref, lp_ref, sub_ref, o_ref, qrow, m_s, l_s, acc = refs[2 * pps:]
    p = pl.program_id(1)
    nrow = 2 * ATT_HEADS * ts
    rows_per_head = 2 * ts

    @pl.when(p == 0)
    def _():
        row = lax.broadcasted_iota(jnp.int32, (nrow, ATT_W), 0)
        lane = lax.broadcasted_iota(jnp.int32, (nrow, ATT_W), 1)
        qrow[...] = jnp.where(lane // HEAD_QK == row // ts, q_ref[...], 0.0).astype(BF16)
        m_s[...] = jnp.full(m_s.shape, NEG_INF, F32)
        l_s[...] = jnp.zeros(l_s.shape, F32)
        acc[...] = jnp.zeros(acc.shape, F32)

    def attend(k_refs, v_refs, own):
        kt = jnp.concatenate([r[...] for r in k_refs], axis=1)
        s = jnp.dot(qrow[...], kt.astype(BF16), preferred_element_type=F32)
        if own:
            r2 = lax.broadcasted_iota(jnp.int32, s.shape, 0)
            c2 = lax.broadcasted_iota(jnp.int32, s.shape, 1)
            s = jnp.where(c2 <= r2 % ts, s, NEG_INF)
        m_new = jnp.maximum(m_s[...], jnp.max(s, axis=-1, keepdims=True))
        alpha = jnp.exp(m_s[...] - m_new)
        pr = jnp.exp(s - m_new)
        l_s[...] = alpha * l_s[...] + jnp.sum(pr, axis=-1, keepdims=True)
        pr = pr.astype(BF16)
        pv = []
        for h in range(ATT_HEADS):
            vh = jnp.concatenate([r[pl.ds(h, PAGE, stride=ATT_HEADS), :] for r in v_refs], axis=0)
            pv.append(jnp.dot(pr[h * rows_per_head:(h + 1) * rows_per_head], vh.astype(BF16),
                              preferred_element_type=F32))
        acc[...] = alpha * acc[...] + jnp.concatenate(pv, axis=0)
        m_s[...] = m_new

    @pl.when(p < n_steps - 1)
    def _():
        attend(k_refs, v_refs, False)

    @pl.when(p == n_steps - 1)
    def _():
        attend((kn_ref,), (vn_ref,), True)
        lam = _lambda(lp_ref, lam_init)
        on = acc[...] / l_s[...]
        outs = []
        for h in range(ATT_HEADS):
            blk = on[h * rows_per_head:(h + 1) * rows_per_head]
            oh = blk[:ts] - lam * blk[ts:]
            outs.append(_rms(oh, SUBLN_EPS) * sub_ref[...] * (1.0 - lam_init))
        o_ref[...] = jnp.concatenate(outs, axis=-1)


def _decode_attn(q_rows, kn_t, vn, cache_kt, cache_v, page_table, lam_p, subln, layer, ts, lam_init):
    bs, nrow, _ = q_rows.shape
    n_pages = page_table.shape[1]
    pps = math.gcd(n_pages, DECODE_PAGES_PER_STEP)
    n_steps = n_pages // pps + 1
    kern = functools.partial(_decode_kernel, ts=ts, n_steps=n_steps, pps=pps, lam_init=lam_init)

    def page(which):
        return lambda b, p, pt: (pt[b, jnp.minimum(pps * p + which, n_pages - 1)], layer, 0, 0)

    k_specs = [pl.BlockSpec((None, None, ATT_W, PAGE), page(w)) for w in range(pps)]
    v_specs = [pl.BlockSpec((None, None, PAGE * ATT_HEADS, HEAD_V), page(w)) for w in range(pps)]
    grid_spec = pltpu.PrefetchScalarGridSpec(
        num_scalar_prefetch=1,
        grid=(bs, n_steps),
        in_specs=[pl.BlockSpec((None, nrow, ATT_W), lambda b, p, pt: (b, 0, 0))] + k_specs + v_specs + [
            pl.BlockSpec((None, ATT_W, PAGE), lambda b, p, pt: (b, 0, 0)),
            pl.BlockSpec((None, PAGE * ATT_HEADS, HEAD_V), lambda b, p, pt: (b, 0, 0)),
            pl.BlockSpec((8, HEAD_QK), lambda b, p, pt: (0, 0)),
            pl.BlockSpec((1, HEAD_V), lambda b, p, pt: (0, 0))],
        out_specs=pl.BlockSpec((None, ts, ATT_W), lambda b, p, pt: (b, 0, 0)),
        scratch_shapes=[pltpu.VMEM((nrow, ATT_W), BF16), pltpu.VMEM((nrow, 1), F32),
                        pltpu.VMEM((nrow, 1), F32), pltpu.VMEM((nrow, HEAD_V), F32)],
    )
    return pl.pallas_call(
        kern, grid_spec=grid_spec,
        out_shape=jax.ShapeDtypeStruct((bs, ts, ATT_W), F32),
        compiler_params=_cparams(("arbitrary", "arbitrary")),
    )(page_table, q_rows, *([cache_kt] * pps), *([cache_v] * pps), kn_t, vn, lam_p, subln)


def _merge_kernel(x_ref, y_ref, bon_ref, g_ref, oatt_ref, gates_ref, gt_ref, ln_ref,
                  wrw_ref, watt_ref, wo_ref, o_ref):
    y = y_ref[...]
    n = y.shape[0]
    parts = []
    for h in range(RW_HEADS):
        yh = y[:, h * RW_HEAD:(h + 1) * RW_HEAD]
        mu = jnp.mean(yh, axis=-1, keepdims=True)
        d = yh - mu
        var = jnp.mean(d * d, axis=-1, keepdims=True)
        parts.append(d * lax.rsqrt(var + RW_GN_EPS))
    yn = jnp.concatenate(parts, axis=-1) * ln_ref[0:1, :] + ln_ref[1:2, :]
    out_rw = (yn + bon_ref[...]) * g_ref[...]
    y_rw = _dot(out_rw, wrw_ref[...])
    y_att = _dot(oatt_ref[...], watt_ref[...])
    gates = gates_ref[...]
    merged = (jax.nn.sigmoid(gates[:, :D_MODEL]) * y_rw
              + jax.nn.sigmoid(gates[:, D_MODEL:]) * y_att)
    o_ref[...] = x_ref[...] + gt_ref[...] * _dot(merged, wo_ref[...])


def _merge(x, y, bon, g, oatt, proj, gt, ln, wrw, watt, wo, tm):
    N = x.shape[0]
    tpg = (N // gt.arr.shape[1]) // tm
    s512 = pl.BlockSpec((tm, RW_W), lambda i: (i, 0))
    full = lambda a: pl.BlockSpec(a.shape, lambda i: (0,) * a.ndim)
    return pl.pallas_call(
        _merge_kernel,
        grid=(N // tm,),
        in_specs=[pl.BlockSpec((tm, D_MODEL), lambda i: (i, 0)), s512, s512, s512, s512,
                  pl.BlockSpec((tm, 2 * D_MODEL), lambda i: (i, 0)),
                  _mod_spec(gt, tm, tpg), full(ln), full(wrw), full(watt), full(wo)],
        out_specs=pl.BlockSpec((tm, D_MODEL), lambda i: (i, 0)),
        out_shape=jax.ShapeDtypeStruct((N, D_MODEL), F32),
        compiler_params=_cparams(("arbitrary",)),
    )(x, y, bon, g, oatt, proj, gt.arr, ln, wrw, watt, wo)


def _first_argmax(cols):
    best = cols[0]
    idx = jnp.zeros(best.shape, jnp.int32)
    for n, c in enumerate(cols[1:], start=1):
        take = c > best
        best = jnp.where(take, c, best)
        idx = jnp.where(take, n, idx)
    return best, idx


def _router_kernel(x_ref, g_ref, sc_ref, sh_ref, wr_ref, eb_ref, h_ref, gate_ref):
    y = _rms(x_ref[...], NORM_EPS) * g_ref[...]
    h = y * (1.0 + sc_ref[...]) + sh_ref[...]
    h_hi = h.astype(BF16)
    h_ref[...] = h_hi
    h_lo = (h - h_hi.astype(F32)).astype(BF16)
    w = wr_ref[...]
    w_hi = w.astype(BF16)
    w_lo = (w - w_hi.astype(F32)).astype(BF16)
    dot = functools.partial(jnp.dot, preferred_element_type=F32)
    logits = dot(h_hi, w_hi) + (dot(h_lo, w_hi) + dot(h_hi, w_lo))
    s = jax.nn.sigmoid(logits.T[:N_EXPERTS])
    sel = s + eb_ref[...][:N_EXPERTS]
    sc_cols = [sel[e:e + 1, :] for e in range(N_EXPERTS)]
    s_cols = [s[e:e + 1, :] for e in range(N_EXPERTS)]
    grp = []
    for gi in range(N_GROUPS):
        cols = sc_cols[gi * EPG:(gi + 1) * EPG]
        m1, i1 = _first_argmax(cols)
        rest = [jnp.where(i1 == n, -jnp.inf, c) for n, c in enumerate(cols)]
        m2, _ = _first_argmax(rest)
        grp.append(m1 + m2)
    _, g_idx = _first_argmax(grp)
    pick = lambda cols_all, n: sum(jnp.where(g_idx == gi, cols_all[gi * EPG + n], 0.0)
                                   for gi in range(N_GROUPS))
    sel_g = [pick(sc_cols, n) for n in range(EPG)]
    s_g = [pick(s_cols, n) for n in range(EPG)]
    _, loc1 = _first_argmax(sel_g)
    _, loc2 = _first_argmax([jnp.where(loc1 == n, -jnp.inf, c) for n, c in enumerate(sel_g)])
    w_1 = sum(jnp.where(loc1 == n, s_g[n], 0.0) for n in range(EPG))
    w_2 = sum(jnp.where(loc2 == n, s_g[n], 0.0) for n in range(EPG))
    tot = w_1 + w_2
    w_1, w_2 = w_1 / tot, w_2 / tot
    e1 = g_idx * EPG + loc1
    e2 = g_idx * EPG + loc2
    expert = lax.broadcasted_iota(jnp.int32, (logits.shape[1], logits.shape[0]), 0)
    gate_t = jnp.where(expert == e1, w_1, 0.0) + jnp.where(expert == e2, w_2, 0.0)
    gate_ref[...] = gate_t.T


def _router(x, g, sc, sh, wr, eb, tm):
    N, D = x.shape
    tpg = (N // sc.arr.shape[1]) // tm
    return pl.pallas_call(
        _router_kernel,
        grid=(N // tm,),
        in_specs=[pl.BlockSpec((tm, D), lambda i: (i, 0)),
                  pl.BlockSpec((1, D), lambda i: (0, 0)),
                  _mod_spec(sc, tm, tpg), _mod_spec(sh, tm, tpg),
                  pl.BlockSpec((D, 128), lambda i: (0, 0)),
                  pl.BlockSpec((128, 1), lambda i: (0, 0))],
        out_specs=[pl.BlockSpec((tm, D), lambda i: (i, 0)),
                   pl.BlockSpec((tm, 128), lambda i: (i, 0))],
        out_shape=[jax.ShapeDtypeStruct((N, D), BF16), jax.ShapeDtypeStruct((N, 128), F32)],
        compiler_params=_cparams(("arbitrary",)),
    )(x, g, sc.arr, sh.arr, wr, eb)


def _moe_kernel(x_ref, h_ref, gate_ref, gt_ref, w1_ref, w3_ref, w2_ref, o_ref, acc_ref, act_ref):
    grp = pl.program_id(1)
    n_e = w1_ref.shape[0]

    @pl.when(grp == 0)
    def _():
        acc_ref[...] = jnp.zeros(acc_ref.shape, F32)

    h = h_ref[...]
    gate = gate_ref[...]
    lane = lax.broadcasted_iota(jnp.int32, gate.shape, 1)
    for el in range(n_e):
        a1 = jnp.dot(h, w1_ref[el], preferred_element_type=F32)
        a3 = jnp.dot(h, w3_ref[el], preferred_element_type=F32)
        gcol = jnp.sum(jnp.where(lane == grp * n_e + el, gate, 0.0), axis=-1, keepdims=True)
        act_ref[:, el * D_EXPERT:(el + 1) * D_EXPERT] = (
            (a1 * jax.nn.sigmoid(a1)) * a3 * gcol).astype(BF16)
    w2 = w2_ref[...].reshape(n_e * D_EXPERT, D_MODEL)
    acc_ref[...] += jnp.dot(act_ref[...], w2, preferred_element_type=F32)

    @pl.when(grp == pl.num_programs(1) - 1)
    def _():
        o_ref[...] = x_ref[...] + gt_ref[...] * acc_ref[...]


def _moe(x, h, gate, gt, w1, w3, w2, tm):
    N, D = x.shape
    tpg = (N // gt.arr.shape[1]) // tm
    layer = gt.layer
    eps = MOE_EXPERTS_PER_STEP
    return pl.pallas_call(
        _moe_kernel,
        grid=(N // tm, N_EXPERTS // eps),
        in_specs=[pl.BlockSpec((tm, D), lambda i, e: (i, 0)),
                  pl.BlockSpec((tm, D), lambda i, e: (i, 0)),
                  pl.BlockSpec((tm, 128), lambda i, e: (i, 0)),
                  _mod_spec(gt, tm, tpg),
                  pl.BlockSpec((None, eps, D, D_EXPERT), lambda i, e: (layer, e, 0, 0)),
                  pl.BlockSpec((None, eps, D, D_EXPERT), lambda i, e: (layer, e, 0, 0)),
                  pl.BlockSpec((None, eps, D_EXPERT, D), lambda i, e: (layer, e, 0, 0))],
        out_specs=pl.BlockSpec((tm, D), lambda i, e: (i, 0)),
        out_shape=jax.ShapeDtypeStruct((N, D), F32),
        scratch_shapes=[pltpu.VMEM((tm, D), F32), pltpu.VMEM((tm, eps * D_EXPERT), BF16)],
        compiler_params=_cparams(("arbitrary", "arbitrary")),
    )(x, h, gate, gt.arr, w1, w3, w2)


def _final_norm_kernel(x_ref, g_ref, o_ref):
    o_ref[...] = _rms(x_ref[...], NORM_EPS) * g_ref[...]


def _final_norm(x, g, tm):
    N, D = x.shape
    return pl.pallas_call(
        _final_norm_kernel,
        grid=(N // tm,),
        in_specs=[pl.BlockSpec((tm, D), lambda i: (i, 0)), pl.BlockSpec((1, D), lambda i: (0, 0))],
        out_specs=pl.BlockSpec((tm, D), lambda i: (i, 0)),
        out_shape=jax.ShapeDtypeStruct((N, D), F32),
        compiler_params=_cparams(("arbitrary",)),
    )(x, g)


def _group_layer(x, mods, lw, cfg, z0, s0, attend, stack):
    sh1, sc1, gt1, sh2, sc2, gt2 = mods
    tm = cfg["tm"]
    proj = _nm_matmul(x, lw["norm1_g"], sc1, sh1, lw["w_in"], cfg["tm_in"], cfg["tn_in"])
    r, lwd, kk, ka, k2, v, g, bon = _rw_prep(proj, z0, lw["mu"], lw["rw_vecs"], lw["wup"],
                                               cfg["tm_prep"], cfg["seq"])
    seq, n_seq = cfg["seq"], cfg["n_seq"]
    rows = max(seq, 8)
    wkv_in = (r, lwd, kk, ka, k2, v)
    if rows != seq:
        wkv_in = tuple(jnp.pad(a.reshape(n_seq, seq, RW_W), ((0, 0), (0, rows - seq), (0, 0))
                               ).reshape(n_seq * rows, RW_W) for a in wkv_in)
    y, s_fin = _wkv(*wkv_in, s0, n_seq, cfg["tb"], cfg["chunk"])
    if rows != seq:
        y = y.reshape(n_seq, rows, RW_W)[:, :seq].reshape(n_seq * seq, RW_W)
    qs, kr, *stacked = _rope(proj, cfg["rope_tabs"], cfg["tm_prep"], stack)
    oatt = attend(qs, kr, proj)
    x = _merge(x, y, bon, g, oatt, proj, gt1, lw["ln"], lw["w_rw_out"], lw["w_att_out"],
               lw["w_o"], tm)
    h2, gate = _router(x, lw["norm2_g"], sc2, sh2, lw["w_router"], lw["e_bias"], tm)
    x = _moe(x, h2, gate, gt2, lw["moe_w1"], lw["moe_w3"], lw["moe_w2"], cfg["tm_moe"])
    return x, proj, kr, stacked, s_fin


def kernel(x_prompt, x_sample, c_prompt, c_sample, cache_k, cache_v, page_table, state_shift, state_wkv, w_ada, b_ada, norm1_g, norm2_g, w_in, rw_mu, rw_w0, rw_w_up, rw_a0, rw_a_up, rw_g_up, rw_k_k, rw_k_a, rw_r_k, rw_ln_w, rw_ln_b, w_rw_out, att_lq1, att_lk1, att_lq2, att_lk2, att_subln, w_att_out, w_o, w_router, e_bias, moe_w1, moe_w3, moe_w2, normf_g):
    bp, tp, D = x_prompt.shape
    bs, ts, _ = x_sample.shape
    depth = w_in.shape[0]
    n_pages = page_table.shape[1]
    past_len = n_pages * PAGE
    n_p, n_s = bp * tp, bs * ts

    z_w, q_w, k_w, v_w, grw_w, gatt_w = jnp.split(
        w_in, [RW_PROJ, RW_PROJ + ATT_W, RW_PROJ + 2 * ATT_W, RW_PROJ + 3 * ATT_W,
               RW_PROJ + 3 * ATT_W + D], axis=-1)
    w_in_p = jnp.concatenate(
        [grw_w, gatt_w, q_w, k_w, v_w, z_w,
         jnp.zeros((depth, D, RW_PROJ_PAD - RW_PROJ), F32)], axis=-1).astype(BF16)
    mu_p = jnp.pad(rw_mu, ((0, 0), (0, RW_PROJ_PAD - RW_PROJ)))
    wup = jnp.zeros((depth, 3, LORA_PAD, RW_W), F32)
    wup = wup.at[:, 0, 0:DECAY_LORA].set(rw_w_up)
    wup = wup.at[:, 1, DECAY_LORA:DECAY_LORA + AAA_LORA].set(rw_a_up)
    wup = wup.at[:, 2, DECAY_LORA + AAA_LORA:DECAY_LORA + AAA_LORA + GATE_LORA].set(rw_g_up)
    wup = wup.astype(BF16)
    rw_vecs = jnp.stack([rw_w0, rw_a0, rw_k_k, rw_k_a, rw_r_k.reshape(depth, RW_W),
                         jnp.zeros_like(rw_w0), jnp.zeros_like(rw_w0), jnp.zeros_like(rw_w0)], axis=1)
    ln = jnp.stack([rw_ln_w, rw_ln_b], axis=1)
    lam_p = jnp.stack([att_lq1, att_lk1, att_lq2, att_lk2] + [jnp.zeros_like(att_lq1)] * 4, axis=1)
    w_router_p = jnp.pad(w_router, ((0, 0), (0, 128 - N_EXPERTS)))
    e_bias_p = jnp.pad(e_bias, (0, 128 - N_EXPERTS)).reshape(128, 1)
    w_rw_out_b, w_att_out_b, w_o_b = (w.astype(BF16) for w in (w_rw_out, w_att_out, w_o))
    moe_w1_b, moe_w3_b, moe_w2_b = (w.astype(BF16) for w in (moe_w1, moe_w3, moe_w2))

    n_c = bp + bs
    n_cp = -(-n_c // 8) * 8
    c_all = jnp.pad(jnp.concatenate([c_prompt, c_sample], axis=0), ((0, n_cp - n_c), (0, 0)))
    mod = _ada_mod(c_all, w_ada, b_ada)
    mod_p = mod[:, :bp].reshape(depth, bp, 1, 6 * D)
    mod_s = jnp.repeat(mod[:, bp:n_c], ts, axis=1).reshape(depth, 1, n_s, 6 * D)

    tm_p = min(512, tp)
    cfg_p = dict(tm=tm_p, tm_in=min(1024, tp), tn_in=1792, tm_prep=min(512, tp), seq=tp, n_seq=bp,
                 tb=min(256, tp), chunk=WKV_CHUNK, tm_moe=min(1024, tp),
                 rope_tabs=_rope_tables(jnp.arange(tp)))
    cfg_s = dict(tm=n_s, tm_in=n_s, tn_in=1792, tm_prep=n_s, seq=ts, n_seq=bs, tb=8, chunk=8, tm_moe=n_s,
                 rope_tabs=_rope_tables(jnp.tile(past_len + jnp.arange(ts), bs)))

    cache_kt = jnp.transpose(cache_k, (0, 1, 3, 4, 5, 2)).reshape(cache_k.shape[0], depth, ATT_W, PAGE)
    cache_v2 = cache_v.reshape(cache_v.shape[0], depth, PAGE * ATT_HEADS, HEAD_V)
    zp0 = jnp.zeros((bp, 1, RW_PROJ_PAD), F32)
    sp0 = jnp.zeros((bp, RW_HEADS, RW_HEAD, RW_HEAD), F32)

    xp = x_prompt.reshape(n_p, D)
    xs = x_sample.reshape(n_s, D)
    outs = {k: [] for k in ("zp", "sp", "ks", "vs", "zs", "ss")}
    kt_all = v_all = None
    for l in range(depth):
        lam_init = 0.8 - 0.6 * math.exp(-0.3 * l)
        lw = dict(norm1_g=norm1_g[l].reshape(1, D), norm2_g=norm2_g[l].reshape(1, D),
                  w_in=w_in_p, mu=mu_p[l].reshape(1, RW_PROJ_PAD), rw_vecs=rw_vecs[l],
                  wup=wup[l], ln=ln[l], w_rw_out=w_rw_out_b[l], w_att_out=w_att_out_b[l],
                  w_o=w_o_b[l], w_router=w_router_p, e_bias=e_bias_p,
                  moe_w1=moe_w1_b, moe_w3=moe_w3_b, moe_w2=moe_w2_b)
        subln = att_subln[l].reshape(1, HEAD_V)
        mods_p = [_Mod(mod_p, l, sec) for sec in range(6)]
        mods_s = [_Mod(mod_s, l, sec) for sec in range(6)]

        attend_p = lambda qs, kr, proj: _flash(qs, kr, proj, lam_p[l], subln, bp,
                                               min(512, tp), lam_init)
        xp, proj_p, _, (kt_all, v_all), s_fin = _group_layer(
            xp, mods_p, lw, cfg_p, zp0, sp0, attend_p, (l, depth, bp, kt_all, v_all))
        outs["zp"].append(proj_p.reshape(bp, tp, IN_W_PAD)[:, -1, COL_Z:COL_Z + RW_PROJ])
        outs["sp"].append(s_fin)

        def attend_s(qs, kr, proj):
            q_rows = jnp.tile(qs.reshape(bs, ts, ATT_W), (1, 2 * ATT_HEADS, 1))
            kn_t = jnp.pad(jnp.swapaxes(kr.reshape(bs, ts, ATT_W), 1, 2),
                           ((0, 0), (0, 0), (0, PAGE - ts)))
            vn = jnp.pad(proj[:, COL_V:COL_V + ATT_W].reshape(bs, ts * ATT_HEADS, HEAD_V),
                         ((0, 0), (0, (PAGE - ts) * ATT_HEADS), (0, 0)))
            return _decode_attn(q_rows, kn_t, vn, cache_kt, cache_v2, page_table, lam_p[l], subln, l,
                                ts, lam_init).reshape(n_s, ATT_W)

        z0_s = jnp.pad(jnp.repeat(state_shift[l], ts, axis=0), ((0, 0), (0, RW_PROJ_PAD - RW_PROJ)))
        xs, proj_s, kr_s, _, s_fin = _group_layer(xs, mods_s, lw, cfg_s, z0_s, state_wkv[l], attend_s,
                                                  None)
        outs["ks"].append(kr_s.reshape(bs, ts, ATT_HEADS, 2, HEAD_QK))
        outs["vs"].append(proj_s[:, COL_V:COL_V + ATT_W].reshape(bs, ts, ATT_HEADS, HEAD_V))
        outs["zs"].append(proj_s.reshape(bs, ts, IN_W_PAD)[:, -1, COL_Z:COL_Z + RW_PROJ])
        outs["ss"].append(s_fin)

    y_prompt = _final_norm(xp, normf_g.reshape(1, D), tm_p).reshape(bp, tp, D)
    y_sample = _final_norm(xs, normf_g.reshape(1, D), n_s).reshape(bs, ts, D)
    k_prompt = jnp.transpose(kt_all.reshape(bp, depth, ATT_HEADS, 2, HEAD_QK, tp), (0, 1, 5, 2, 3, 4))
    v_prompt = v_all.reshape(bp, depth, tp, ATT_HEADS, HEAD_V)
    return (y_prompt, y_sample, k_prompt, v_prompt,
            jnp.stack(outs["zp"], axis=0), jnp.stack(outs["sp"], axis=0),
            jnp.stack(outs["ks"], axis=1), jnp.stack(outs["vs"], axis=1),
            jnp.stack(outs["zs"], axis=0), jnp.stack(outs["ss"], axis=0))
```

```python
import functools
import math
from typing import NamedTuple

import jax
import jax.numpy as jnp
from jax import lax
from jax.experimental import pallas as pl
from jax.experimental.pallas import tpu as pltpu

F32 = jnp.float32
BF16 = jnp.bfloat16

D_MODEL = 1024
RW_HEAD = 64
RW_HEADS = 8
RW_W = RW_HEADS * RW_HEAD
DECAY_LORA = 32
AAA_LORA = 32
GATE_LORA = 96
RW_PROJ = 3 * RW_W + DECAY_LORA + AAA_LORA + GATE_LORA
RW_PROJ_PAD = 1792
LORA_PAD = RW_PROJ_PAD - 3 * RW_W
RW_GN_EPS = 64e-5
ATT_HEADS = 4
HEAD_QK = 64
HEAD_V = 128
ATT_W = 512
ROPE_DIM = HEAD_QK // 4
ROPE_THETA = 500000.0
SUBLN_EPS = 1e-5
NEG_INF = -1e30
N_EXPERTS = 16
N_GROUPS = 4
EPG = 4
D_EXPERT = 512
NORM_EPS = 1e-6
PAGE = 128
DECODE_PAGES_PER_STEP = 16
WKV_SEQS_PER_STEP = 2
MOE_EXPERTS_PER_STEP = 2
FLASH_HEADS_PER_STEP = 4
WKV_CHUNK = 64

COL_GATE = 0
COL_Q = 2048
COL_V = 3072
COL_Z = 3584
IN_W_PAD = COL_Z + RW_PROJ_PAD

VMEM_LIMIT = 56 * 1024 * 1024

_NN = (((1,), (0,)), ((), ()))
_NT = (((1,), (1,)), ((), ()))
_TN = (((0,), (0,)), ((), ()))


def _dot(a, b, dims=_NN):
    return lax.dot_general(a.astype(BF16), b.astype(BF16), dims, preferred_element_type=F32)


def _dot_hi(a, b, dims=_NN):
    return lax.dot_general(a, b, dims, preferred_element_type=F32,
                           precision=lax.Precision.HIGHEST)


def _cparams(sem):
    return pltpu.CompilerParams(dimension_semantics=sem, vmem_limit_bytes=VMEM_LIMIT)


def _rms(x, eps):
    return x * lax.rsqrt(jnp.mean(x * x, axis=-1, keepdims=True) + eps)


def _ada_kernel(c_ref, w_ref, b_ref, o_ref):
    c = c_ref[...]
    sc = c * jax.nn.sigmoid(c)
    o_ref[...] = _dot(sc, w_ref[...]) + b_ref[...]


def _ada_mod(c_all, w_ada, b_ada):
    L, D, N6 = w_ada.shape
    M = c_all.shape[0]
    tn = 1536
    return pl.pallas_call(
        _ada_kernel,
        grid=(L, N6 // tn),
        in_specs=[pl.BlockSpec((M, D), lambda l, j: (0, 0)),
                  pl.BlockSpec((None, D, tn), lambda l, j: (l, 0, j)),
                  pl.BlockSpec((None, 1, tn), lambda l, j: (l, 0, j))],
        out_specs=pl.BlockSpec((None, M, tn), lambda l, j: (l, 0, j)),
        out_shape=jax.ShapeDtypeStruct((L, M, N6), F32),
        compiler_params=_cparams(("arbitrary", "arbitrary")),
    )(c_all, w_ada, b_ada.reshape(L, 1, N6))


class _Mod(NamedTuple):
    arr: jax.Array
    layer: int
    sec: int


def _mod_spec(m, tm, tiles_per_group):
    rows = m.arr.shape[2]
    return pl.BlockSpec((None, None, rows, D_MODEL),
                        lambda i, *_: (m.layer, i // tiles_per_group, 0, m.sec))


def _nm_matmul_kernel(x_ref, g_ref, sc_ref, sh_ref, w_ref, o_ref, h_ref):
    @pl.when(pl.program_id(1) == 0)
    def _():
        y = _rms(x_ref[...], NORM_EPS) * g_ref[...]
        h_ref[...] = (y * (1.0 + sc_ref[...]) + sh_ref[...]).astype(BF16)

    o_ref[...] = jnp.dot(h_ref[...], w_ref[...], preferred_element_type=F32)


def _nm_matmul(x, g, sc, sh, w, tm, tn):
    N, D = x.shape
    n_out = w.shape[2]
    tpg = (N // sc.arr.shape[1]) // tm
    layer = sc.layer
    return pl.pallas_call(
        _nm_matmul_kernel,
        grid=(N // tm, n_out // tn),
        in_specs=[pl.BlockSpec((tm, D), lambda i, j: (i, 0)),
                  pl.BlockSpec((1, D), lambda i, j: (0, 0)),
                  _mod_spec(sc, tm, tpg), _mod_spec(sh, tm, tpg),
                  pl.BlockSpec((None, D, tn), lambda i, j: (layer, 0, j))],
        out_specs=pl.BlockSpec((tm, tn), lambda i, j: (i, j)),
        out_shape=jax.ShapeDtypeStruct((N, n_out), F32),
        scratch_shapes=[pltpu.VMEM((tm, D), BF16)],
        compiler_params=_cparams(("arbitrary", "arbitrary")),
    )(x, g, sc.arr, sh.arr, w)


def _head_sum(x):
    parts = []
    for h in range(RW_HEADS):
        s = jnp.sum(x[:, h * RW_HEAD:(h + 1) * RW_HEAD], axis=-1, keepdims=True)
        parts.append(jnp.broadcast_to(s, (x.shape[0], RW_HEAD)))
    return jnp.concatenate(parts, axis=-1)


def _rw_prep_kernel(z_ref, z0_ref, mu_ref, vec_ref, wup_ref,
                    r_ref, lw_ref, kk_ref, ka_ref, k2_ref, v_ref, g_ref, bon_ref,
                    carry_ref, *, tm, seq_len):
    i = pl.program_id(0)
    z = z_ref[...]
    rolled = pltpu.roll(z, 1, axis=0)
    row = lax.broadcasted_iota(jnp.int32, z.shape, 0)
    if seq_len >= tm:
        tiles_per_seq = seq_len // tm
        first = jnp.where(i % tiles_per_seq == 0, z0_ref[...], carry_ref[0:1, :])
        z_prev = jnp.where(row == 0, first, rolled)
        carry_ref[0:1, :] = z[tm - 1:tm, :]
    else:
        z_prev = jnp.where(row % seq_len == 0, z0_ref[...], rolled)
    zs = z + (z_prev - z) * mu_ref[...]
    r = zs[:, 0:RW_W]
    k = zs[:, RW_W:2 * RW_W]
    v = zs[:, 2 * RW_W:3 * RW_W]
    tail = zs[:, 3 * RW_W:]
    w0, a0, k_k, k_a, r_k = (vec_ref[j:j + 1, :] for j in range(5))
    dw = _dot(jnp.tanh(tail), wup_ref[0])
    da = _dot(tail, wup_ref[1])
    g = _dot(jax.nn.sigmoid(tail), wup_ref[2])
    t = -(w0 + dw)
    softplus = jnp.maximum(t, 0.0) + jnp.log1p(jnp.exp(-jnp.abs(t)))
    lw = -jnp.exp(-softplus - 0.5)
    a = jax.nn.sigmoid(a0 + da)
    kk = k * k_k
    kk = kk / jnp.maximum(jnp.sqrt(_head_sum(kk * kk)), 1e-12)
    k2 = k * (1.0 + (a - 1.0) * k_a)
    r_ref[...] = r
    lw_ref[...] = lw
    kk_ref[...] = kk
    ka_ref[...] = kk * a
    k2_ref[...] = k2
    v_ref[...] = v
    g_ref[...] = g
    bon_ref[...] = _head_sum(r * k2 * r_k) * v


def _rw_prep(proj, z0, mu, vecs, wup, tm, seq_len):
    N = proj.shape[0]
    kern = functools.partial(_rw_prep_kernel, tm=tm, seq_len=seq_len)
    if seq_len >= tm:
        tps = seq_len // tm
        z0_spec = pl.BlockSpec((None, 1, RW_PROJ_PAD), lambda i: (i // tps, 0, 0))
    else:
        z0_spec = pl.BlockSpec((tm, RW_PROJ_PAD), lambda i: (i, 0))
    o_spec = pl.BlockSpec((tm, RW_W), lambda i: (i, 0))
    return pl.pallas_call(
        kern,
        grid=(N // tm,),
        in_specs=[pl.BlockSpec((tm, RW_PROJ_PAD), lambda i: (i, COL_Z // RW_PROJ_PAD)),
                  z0_spec,
                  pl.BlockSpec((1, RW_PROJ_PAD), lambda i: (0, 0)),
                  pl.BlockSpec((8, RW_W), lambda i: (0, 0)),
                  pl.BlockSpec((3, LORA_PAD, RW_W), lambda i: (0, 0, 0))],
        out_specs=[o_spec] * 8,
        out_shape=[jax.ShapeDtypeStruct((N, RW_W), F32)] * 8,
        scratch_shapes=[pltpu.VMEM((8, RW_PROJ_PAD), F32)],
        compiler_params=_cparams(("arbitrary",)),
    )(proj, z0, mu, vecs, wup)


def _wkv_kernel(r_ref, lw_ref, kk_ref, ka_ref, k2_ref, v_ref, s0_ref, y_ref, sout_ref,
                s_ref, *, tb, C, sps):
    tblk = pl.program_id(1)
    n_pair = RW_HEADS // 2
    zero64 = jnp.zeros((RW_HEAD, RW_HEAD), F32)
    units = [(q, p) for q in range(sps) for p in range(n_pair)]
    nu = range(len(units))

    @pl.when(tblk == 0)
    def _():
        for n, (q, p) in enumerate(units):
            s_ref[n] = jnp.concatenate(
                [jnp.concatenate([s0_ref[q, 2 * p], zero64], axis=1),
                 jnp.concatenate([zero64, s0_ref[q, 2 * p + 1]], axis=1)], axis=0)

    C2 = 2 * C
    ri = lax.broadcasted_iota(jnp.int32, (C, C), 0)
    ci = lax.broadcasted_iota(jnp.int32, (C, C), 1)
    tri = (ri >= ci).astype(F32)
    r2 = lax.broadcasted_iota(jnp.int32, (C2, C2), 0)
    c2 = lax.broadcasted_iota(jnp.int32, (C2, C2), 1)
    low_incl = r2 >= c2
    low_strict = r2 > c2
    eye = (r2 == c2).astype(F32)
    lo_lanes = lax.broadcasted_iota(jnp.int32, (C, 128), 1) < RW_HEAD
    n_sq = int(math.log2(C)) - 1

    def stack(x, p):
        xp = x[:, p * 128:(p + 1) * 128]
        return jnp.concatenate([jnp.where(lo_lanes, xp, 0.0), jnp.where(lo_lanes, 0.0, xp)], axis=0)

    def chunk(c, carry):
        rows = pl.ds(pl.multiple_of(c * C, C), C)
        a_t, r_t, b_t, k_t, b_r, k_r, v_q, p_tot = [], [], [], [], [], [], [], []
        for q in range(sps):
            lw = lw_ref[q, rows, :]
            kk = kk_ref[q, rows, :]
            ka = ka_ref[q, rows, :]
            k2 = k2_ref[q, rows, :]
            cs = _dot_hi(tri, lw)
            tot = cs[C - 1:C, :]
            p_inv = jnp.exp(-cs)
            p_rem = jnp.exp(tot - cs)
            a_t.append(-kk * jnp.exp(cs - lw))
            r_t.append(r_ref[q, rows, :] * jnp.exp(cs))
            b_t.append(ka * p_inv)
            k_t.append(k2 * p_inv)
            b_r.append(ka * p_rem)
            k_r.append(k2 * p_rem)
            v_q.append(v_ref[q, rows, :])
            p_tot.append(jnp.exp(tot))
        a_s = [stack(a_t[q], p).astype(BF16) for q, p in units]
        r_s = [stack(r_t[q], p).astype(BF16) for q, p in units]
        b_s = [stack(b_t[q], p).astype(BF16) for q, p in units]
        k_s = [stack(k_t[q], p).astype(BF16) for q, p in units]
        bk_r = [jnp.concatenate([stack(b_r[q], p), stack(k_r[q], p)], axis=0).astype(BF16)
                for q, p in units]
        v_s = [stack(v_q[q], p).astype(BF16) for q, p in units]
        ar = [jnp.concatenate([a_s[n], r_s[n]], axis=0) for n in nu]
        gb = [_dot(ar[n], b_s[n], _NT) for n in nu]
        gk = [_dot(ar[n], k_s[n], _NT) for n in nu]
        mb = [jnp.where(low_strict, gb[n][:C2], 0.0) for n in nu]
        nb = [jnp.where(low_incl, gb[n][C2:], 0.0) for n in nu]
        mk = [jnp.where(low_strict, gk[n][:C2], 0.0) for n in nu]
        nk = [jnp.where(low_incl, gk[n][C2:], 0.0) for n in nu]
        tinv = [eye + mb[n] for n in nu]
        pw = [_dot(mb[n], mb[n]) for n in nu]
        for lvl in range(n_sq):
            if lvl < n_sq - 1:
                both = [_dot(jnp.concatenate([tinv[n], pw[n]], axis=0), pw[n]) for n in nu]
                tinv = [tinv[n] + both[n][:C2] for n in nu]
                pw = [both[n][C2:] for n in nu]
            else:
                tinv = [tinv[n] + _dot(tinv[n], pw[n]) for n in nu]
        mkv = [_dot(mk[n], v_s[n]) for n in nu]
        x = [_dot(tinv[n], jnp.concatenate([a_s[n], mkv[n].astype(BF16)], axis=1)) for n in nu]
        s_old = [s_ref[n] for n in nu]
        s_bf = [s.astype(BF16) for s in s_old]
        u = [_dot(x[n][:, :128], s_bf[n], _NT) + x[n][:, 128:] for n in nu]
        y = [_dot(r_s[n], s_bf[n], _NT) + _dot(nb[n], u[n]) + _dot(nk[n], v_s[n]) for n in nu]
        for n, (q, p) in enumerate(units):
            uv = jnp.concatenate([u[n].astype(BF16), v_s[n]], axis=0)
            s_ref[n] = s_old[n] * p_tot[q][:, p * 128:(p + 1) * 128] + _dot(uv, bk_r[n], _TN)
            y_ref[q, rows, p * 128:(p + 1) * 128] = y[n][:C] + y[n][C:]
        return carry

    lax.fori_loop(0, tb // C, chunk, 0)

    @pl.when(tblk == pl.num_programs(1) - 1)
    def _():
        for n, (q, p) in enumerate(units):
            s = s_ref[n]
            sout_ref[q, 2 * p] = s[:RW_HEAD, :RW_HEAD]
            sout_ref[q, 2 * p + 1] = s[RW_HEAD:, RW_HEAD:]


def _wkv(r, lw, kk, ka, k2, v, s0, n_seq, tb, C):
    N = r.shape[0]
    T = N // n_seq
    sps = math.gcd(n_seq, WKV_SEQS_PER_STEP)
    kern = functools.partial(_wkv_kernel, tb=tb, C=C, sps=sps)
    in_spec = pl.BlockSpec((sps, tb, RW_W), lambda b, t: (b, t, 0))
    s_spec = pl.BlockSpec((sps, RW_HEADS, RW_HEAD, RW_HEAD), lambda b, t: (b, 0, 0, 0))
    seq = lambda a: a.reshape(n_seq, T, RW_W)
    y, s_fin = pl.pallas_call(
        kern,
        grid=(n_seq // sps, T // tb),
        in_specs=[in_spec] * 6 + [s_spec],
        out_specs=[in_spec, s_spec],
        out_shape=[jax.ShapeDtypeStruct((n_seq, T, RW_W), F32),
                   jax.ShapeDtypeStruct((n_seq, RW_HEADS, RW_HEAD, RW_HEAD), F32)],
        scratch_shapes=[pltpu.VMEM((sps * RW_HEADS // 2, 128, 128), F32)],
        compiler_params=_cparams(("arbitrary", "arbitrary")),
    )(seq(r), seq(lw), seq(kk), seq(ka), seq(k2), seq(v), s0)
    return y.reshape(N, RW_W), s_fin


def _rope_kernel(qk_ref, cos_ref, s1_ref, s2_ref, *rest, stacked):
    if stacked:
        v_ref, q_ref, k_ref, kt_ref, v4_ref = rest[0], *rest[-4:]
    else:
        q_ref, k_ref = rest
    cos, s1, s2 = cos_ref[...], s1_ref[...], s2_ref[...]
    half = ROPE_DIM // 2
    for dst, base, scale in ((q_ref, 0, HEAD_QK ** -0.5), (k_ref, ATT_W, 1.0)):
        for cblk in range(ATT_W // 128):
            x = qk_ref[:, base + cblk * 128: base + (cblk + 1) * 128]
            up = pltpu.roll(x, 128 - half, axis=1)
            dn = pltpu.roll(x, half, axis=1)
            y = x * cos + up * s1 + dn * s2
            dst[:, cblk * 128:(cblk + 1) * 128] = y * scale if scale != 1.0 else y
    if stacked:
        kt_ref[...] = k_ref[...].T
        rows = v_ref.shape[0]
        for h in range(ATT_HEADS):
            v4_ref[pl.ds(h, rows, stride=ATT_HEADS), :] = v_ref[:, h * HEAD_V:(h + 1) * HEAD_V]


def _rope(proj, tabs, tm, stack=None):
    N = proj.shape[0]
    ntab = tabs[0].shape[0] // tm
    t_spec = pl.BlockSpec((tm, 128), lambda i: (i % ntab, 0))
    o_spec = pl.BlockSpec((tm, ATT_W), lambda i: (i, 0))
    in_specs = [pl.BlockSpec((tm, 2 * ATT_W), lambda i: (i, COL_Q // (2 * ATT_W))),
                t_spec, t_spec, t_spec]
    out_specs = [o_spec, o_spec]
    out_shape = [jax.ShapeDtypeStruct((N, ATT_W), F32)] * 2
    operands = [proj, *tabs]
    aliases = {}
    if stack is not None:
        layer, depth, n_seq, kt_buf, v_buf = stack
        T = N // n_seq
        tps = T // tm
        in_specs.append(pl.BlockSpec((tm, ATT_W), lambda i: (i, COL_V // ATT_W)))
        operands.append(proj)
        out_specs += [pl.BlockSpec((None, None, ATT_W, tm), lambda i: (i // tps, layer, 0, i % tps)),
                      pl.BlockSpec((None, None, tm * ATT_HEADS, HEAD_V),
                                   lambda i: (i // tps, layer, i % tps, 0))]
        out_shape += [jax.ShapeDtypeStruct((n_seq, depth, ATT_W, T), F32),
                      jax.ShapeDtypeStruct((n_seq, depth, T * ATT_HEADS, HEAD_V), F32)]
        if kt_buf is not None:
            aliases = {len(operands): 2, len(operands) + 1: 3}
            in_specs += [pl.BlockSpec(memory_space=pl.ANY)] * 2
            operands += [kt_buf, v_buf]
    return pl.pallas_call(
        functools.partial(_rope_kernel, stacked=stack is not None),
        grid=(N // tm,),
        in_specs=in_specs,
        out_specs=out_specs,
        out_shape=out_shape,
        input_output_aliases=aliases,
        compiler_params=_cparams(("arbitrary",)),
    )(*operands)


def _rope_tables(pos):
    half = ROPE_DIM // 2
    inv = ROPE_THETA ** (-jnp.arange(0, ROPE_DIM, 2, dtype=F32) / ROPE_DIM)
    ang = pos.astype(F32)[:, None] * inv[None, :]
    cos, sin = jnp.cos(ang), jnp.sin(ang)
    n = pos.shape[0]
    one = jnp.ones((n, HEAD_QK - ROPE_DIM), F32)
    zero = jnp.zeros((n, HEAD_QK - ROPE_DIM), F32)
    zh = jnp.zeros((n, half), F32)
    c64 = jnp.concatenate([cos, cos, one], axis=1)
    s1 = jnp.concatenate([-sin, zh, zero], axis=1)
    s2 = jnp.concatenate([zh, sin, zero], axis=1)
    return tuple(jnp.concatenate([t, t], axis=1) for t in (c64, s1, s2))


def _lambda(lp_ref, lam_init):
    lp = lp_ref[...]
    d1 = jnp.sum(lp[0:1, :] * lp[1:2, :], axis=-1, keepdims=True)
    d2 = jnp.sum(lp[2:3, :] * lp[3:4, :], axis=-1, keepdims=True)
    return jnp.exp(d1) - jnp.exp(d2) + lam_init


def _flash_kernel(qi_ref, kj_ref, q_ref, k_ref, v_ref, lp_ref, sub_ref, o_ref,
                  qs_ref, m_ref, l_ref, acc_ref, *, tq, hps, lam_init):
    t = pl.program_id(2)
    i = qi_ref[t]
    j = kj_ref[t]
    heads = range(hps)
    cols = lambda hh: slice(hh * 128, (hh + 1) * 128)

    @pl.when(j == 0)
    def _():
        q = q_ref[...] * math.log2(math.e)
        lo = lax.broadcasted_iota(jnp.int32, (tq, 128), 1) < HEAD_QK
        for hh in heads:
            qh = q[:, cols(hh)]
            qs_ref[hh] = jnp.concatenate([jnp.where(lo, qh, 0.0), jnp.where(lo, 0.0, qh)],
                                         axis=0).astype(BF16)
        m_ref[...] = jnp.full(m_ref.shape, NEG_INF, F32)
        l_ref[...] = jnp.zeros(l_ref.shape, F32)
        acc_ref[...] = jnp.zeros(acc_ref.shape, F32)

    def step(diagonal):
        k = k_ref[...].astype(BF16)
        v = v_ref[...].astype(BF16)
        s = [lax.dot_general(k[:, cols(hh)], qs_ref[hh], _NT, preferred_element_type=F32)
             for hh in heads]
        if diagonal:
            key = lax.broadcasted_iota(jnp.int32, s[0].shape, 0)
            qry = lax.broadcasted_iota(jnp.int32, s[0].shape, 1)
            keep = key <= jnp.where(qry >= tq, qry - tq, qry)
            s = [jnp.where(keep, s[hh], NEG_INF) for hh in heads]
        m_old = [m_ref[hh] for hh in heads]
        m_new = [jnp.maximum(m_old[hh], jnp.max(s[hh], axis=0, keepdims=True)) for hh in heads]
        alpha = [jnp.exp2(m_old[hh] - m_new[hh]) for hh in heads]
        p = [jnp.exp2(s[hh] - m_new[hh]) for hh in heads]
        pv = [lax.dot_general(v[:, cols(hh)], p[hh].astype(BF16), _TN, preferred_element_type=F32)
              for hh in heads]
        for hh in heads:
            l_ref[hh] = alpha[hh] * l_ref[hh] + jnp.sum(p[hh], axis=0, keepdims=True)
            acc_ref[hh] = alpha[hh] * acc_ref[hh] + pv[hh]
            m_ref[hh] = m_new[hh]

    @pl.when(j < i)
    def _():
        step(False)

    @pl.when(j == i)
    def _():
        step(True)
        lam = _lambda(lp_ref, lam_init)
        for hh in heads:
            on = acc_ref[hh] / l_ref[hh]
            o_t = on[:, :tq] - lam * on[:, tq:]
            ms = jnp.mean(o_t * o_t, axis=0, keepdims=True)
            o_t = o_t * lax.rsqrt(ms + SUBLN_EPS) * sub_ref[...] * (1.0 - lam_init)
            o_ref[:, cols(hh)] = o_t.T


def _flash(qs, kr, proj, lam_p, subln, n_seq, tq, lam_init):
    N = qs.shape[0]
    nq = (N // n_seq) // tq
    hps = FLASH_HEADS_PER_STEP
    wid = hps * 128
    kern = functools.partial(_flash_kernel, tq=tq, hps=hps, lam_init=lam_init)
    vcol = COL_V // wid
    pairs = [(i, j) for i in range(nq) for j in range(i + 1)]
    qi = jnp.asarray([p[0] for p in pairs], jnp.int32)
    kj = jnp.asarray([p[1] for p in pairs], jnp.int32)
    grid_spec = pltpu.PrefetchScalarGridSpec(
        num_scalar_prefetch=2,
        grid=(n_seq, ATT_HEADS // hps, len(pairs)),
        in_specs=[pl.BlockSpec((tq, wid), lambda b, h, t, qi, kj: (b * nq + qi[t], h)),
                  pl.BlockSpec((tq, wid), lambda b, h, t, qi, kj: (b * nq + kj[t], h)),
                  pl.BlockSpec((tq, wid), lambda b, h, t, qi, kj: (b * nq + kj[t], vcol + h)),
                  pl.BlockSpec((8, HEAD_QK), lambda b, h, t, qi, kj: (0, 0)),
                  pl.BlockSpec((HEAD_V, 1), lambda b, h, t, qi, kj: (0, 0))],
        out_specs=pl.BlockSpec((tq, wid), lambda b, h, t, qi, kj: (b * nq + qi[t], h)),
        scratch_shapes=[pltpu.VMEM((hps, 2 * tq, 128), BF16), pltpu.VMEM((hps, 1, 2 * tq), F32),
                        pltpu.VMEM((hps, 1, 2 * tq), F32), pltpu.VMEM((hps, HEAD_V, 2 * tq), F32)],
    )
    return pl.pallas_call(
        kern, grid_spec=grid_spec,
        out_shape=jax.ShapeDtypeStruct((N, ATT_W), F32),
        compiler_params=_cparams(("arbitrary",) * 3),
    )(qi, kj, qs, kr, proj, lam_p, subln.reshape(HEAD_V, 1))


def _decode_kernel(pt_ref, q_ref, *refs, ts, n_steps, pps, lam_init):
    k_refs, v_refs = refs[:pps], refs[pps:2 * pps]
    kn_ref, vn_ref, lp_ref, sub_ref, o_ref, qrow, m_s, l_s, acc = refs[2 * pps:]
    p = pl.program_id(1)
    nrow = 2 * ATT_HEADS * ts
    rows_per_head = 2 * ts

    @pl.when(p == 0)
    def _():
        row = lax.broadcasted_iota(jnp.int32, (nrow, ATT_W), 0)
        lane = lax.broadcasted_iota(jnp.int32, (nrow, ATT_W), 1)
        qrow[...] = jnp.where(lane // HEAD_QK == row // ts, q_ref[...], 0.0).astype(BF16)
        m_s[...] = jnp.full(m_s.shape, NEG_INF, F32)
        l_s[...] = jnp.zeros(l_s.shape, F32)
        acc[...] = jnp.zeros(acc.shape, F32)

    def attend(k_refs, v_refs, own):
        kt = jnp.concatenate([r[...] for r in k_refs], axis=1)
        s = jnp.dot(qrow[...], kt.astype(BF16), preferred_element_type=F32)
        if own:
            r2 = lax.broadcasted_iota(jnp.int32, s.shape, 0)
            c2 = lax.broadcasted_iota(jnp.int32, s.shape, 1)
            s = jnp.where(c2 <= r2 % ts, s, NEG_INF)
        m_new = jnp.maximum(m_s[...], jnp.max(s, axis=-1, keepdims=True))
        alpha = jnp.exp(m_s[...] - m_new)
        pr = jnp.exp(s - m_new)
        l_s[...] = alpha * l_s[...] + jnp.sum(pr, axis=-1, keepdims=True)
        pr = pr.astype(BF16)
        pv = []
        for h in range(0, ATT_HEADS, 2):
            vh = [jnp.concatenate([r[pl.ds(hh, PAGE, stride=ATT_HEADS), :] for r in v_refs], axis=0)
                  for hh in (h, h + 1)]
            out = jnp.dot(pr[h * rows_per_head:(h + 2) * rows_per_head],
                          jnp.concatenate(vh, axis=1).astype(BF16), preferred_element_type=F32)
            pv += [out[:rows_per_head, :HEAD_V], out[rows_per_head:, HEAD_V:]]
        acc[...] = alpha * acc[...] + jnp.concatenate(pv, axis=0)
        m_s[...] = m_new

    @pl.when(p < n_steps - 1)
    def _():
        attend(k_refs, v_refs, False)

    @pl.when(p == n_steps - 1)
    def _():
        attend((kn_ref,), (vn_ref,), True)
        lam = _lambda(lp_ref, lam_init)
        on = acc[...] / l_s[...]
        outs = []
        for h in range(ATT_HEADS):
            blk = on[h * rows_per_head:(h + 1) * rows_per_head]
            oh = blk[:ts] - lam * blk[ts:]
            outs.append(_rms(oh, SUBLN_EPS) * sub_ref[...] * (1.0 - lam_init))
        o_ref[...] = jnp.concatenate(outs, axis=-1)


def _decode_attn(q_rows, kn_t, vn, cache_kt, cache_v, page_table, lam_p, subln, layer, ts, lam_init):
    bs, nrow, _ = q_rows.shape
    n_pages = page_table.shape[1]
    pps = math.gcd(n_pages, DECODE_PAGES_PER_STEP)
    n_steps = n_pages // pps + 1
    kern = functools.partial(_decode_kernel, ts=ts, n_steps=n_steps, pps=pps, lam_init=lam_init)

    def page(which):
        return lambda b, p, pt: (pt[b, jnp.minimum(pps * p + which, n_pages - 1)], layer, 0, 0)

    k_specs = [pl.BlockSpec((None, None, ATT_W, PAGE), page(w)) for w in range(pps)]
    v_specs = [pl.BlockSpec((None, None, PAGE * ATT_HEADS, HEAD_V), page(w)) for w in range(pps)]
    grid_spec = pltpu.PrefetchScalarGridSpec(
        num_scalar_prefetch=1,
        grid=(bs, n_steps),
        in_specs=[pl.BlockSpec((None, nrow, ATT_W), lambda b, p, pt: (b, 0, 0))] + k_specs + v_specs + [
            pl.BlockSpec((None, ATT_W, PAGE), lambda b, p, pt: (b, 0, 0)),
            pl.BlockSpec((None, PAGE * ATT_HEADS, HEAD_V), lambda b, p, pt: (b, 0, 0)),
            pl.BlockSpec((8, HEAD_QK), lambda b, p, pt: (0, 0)),
            pl.BlockSpec((1, HEAD_V), lambda b, p, pt: (0, 0))],
        out_specs=pl.BlockSpec((None, ts, ATT_W), lambda b, p, pt: (b, 0, 0)),
        scratch_shapes=[pltpu.VMEM((nrow, ATT_W), BF16), pltpu.VMEM((nrow, 1), F32),
                        pltpu.VMEM((nrow, 1), F32), pltpu.VMEM((nrow, HEAD_V), F32)],
    )
    return pl.pallas_call(
        kern, grid_spec=grid_spec,
        out_shape=jax.ShapeDtypeStruct((bs, ts, ATT_W), F32),
        compiler_params=_cparams(("arbitrary", "arbitrary")),
    )(page_table, q_rows, *([cache_kt] * pps), *([cache_v] * pps), kn_t, vn, lam_p, subln)


def _merge_kernel(x_ref, y_ref, bon_ref, g_ref, oatt_ref, gates_ref, gt_ref, ln_ref,
                  wrw_ref, watt_ref, wo_ref, o_ref):
    y = y_ref[...]
    n = y.shape[0]
    parts = []
    for h in range(RW_HEADS):
        yh = y[:, h * RW_HEAD:(h + 1) * RW_HEAD]
        mu = jnp.mean(yh, axis=-1, keepdims=True)
        d = yh - mu
        var = jnp.mean(d * d, axis=-1, keepdims=True)
        parts.append(d * lax.rsqrt(var + RW_GN_EPS))
    yn = jnp.concatenate(parts, axis=-1) * ln_ref[0:1, :] + ln_ref[1:2, :]
    out_rw = (yn + bon_ref[...]) * g_ref[...]
    y_rw = _dot(out_rw, wrw_ref[...])
    y_att = _dot(oatt_ref[...], watt_ref[...])
    gates = gates_ref[...]
    merged = (jax.nn.sigmoid(gates[:, :D_MODEL]) * y_rw
              + jax.nn.sigmoid(gates[:, D_MODEL:]) * y_att)
    o_ref[...] = x_ref[...] + gt_ref[...] * _dot(merged, wo_ref[...])


def _merge(x, y, bon, g, oatt, proj, gt, ln, wrw, watt, wo, tm):
    N = x.shape[0]
    tpg = (N // gt.arr.shape[1]) // tm
    s512 = pl.BlockSpec((tm, RW_W), lambda i: (i, 0))
    full = lambda a: pl.BlockSpec(a.shape, lambda i: (0,) * a.ndim)
    return pl.pallas_call(
        _merge_kernel,
        grid=(N // tm,),
        in_specs=[pl.BlockSpec((tm, D_MODEL), lambda i: (i, 0)), s512, s512, s512, s512,
                  pl.BlockSpec((tm, 2 * D_MODEL), lambda i: (i, 0)),
                  _mod_spec(gt, tm, tpg), full(ln), full(wrw), full(watt), full(wo)],
        out_specs=pl.BlockSpec((tm, D_MODEL), lambda i: (i, 0)),
        out_shape=jax.ShapeDtypeStruct((N, D_MODEL), F32),
        compiler_params=_cparams(("arbitrary",)),
    )(x, y, bon, g, oatt, proj, gt.arr, ln, wrw, watt, wo)


def _first_argmax(cols):
    best = cols[0]
    idx = jnp.zeros(best.shape, jnp.int32)
    for n, c in enumerate(cols[1:], start=1):
        take = c > best
        best = jnp.where(take, c, best)
        idx = jnp.where(take, n, idx)
    return best, idx


def _router_kernel(x_ref, g_ref, sc_ref, sh_ref, wr_ref, eb_ref, h_ref, gate_ref):
    y = _rms(x_ref[...], NORM_EPS) * g_ref[...]
    h = y * (1.0 + sc_ref[...]) + sh_ref[...]
    h_hi = h.astype(BF16)
    h_ref[...] = h_hi
    h_lo = (h - h_hi.astype(F32)).astype(BF16)
    w = wr_ref[...]
    w_hi = w.astype(BF16)
    w_lo = (w - w_hi.astype(F32)).astype(BF16)
    dot = functools.partial(jnp.dot, preferred_element_type=F32)
    logits = dot(h_hi, w_hi) + (dot(h_lo, w_hi) + dot(h_hi, w_lo))
    s = jax.nn.sigmoid(logits.T[:N_EXPERTS])
    sel = s + eb_ref[...][:N_EXPERTS]
    sc_cols = [sel[e:e + 1, :] for e in range(N_EXPERTS)]
    s_cols = [s[e:e + 1, :] for e in range(N_EXPERTS)]
    grp = []
    for gi in range(N_GROUPS):
        cols = sc_cols[gi * EPG:(gi + 1) * EPG]
        m1, i1 = _first_argmax(cols)
        rest = [jnp.where(i1 == n, -jnp.inf, c) for n, c in enumerate(cols)]
        m2, _ = _first_argmax(rest)
        grp.append(m1 + m2)
    _, g_idx = _first_argmax(grp)
    pick = lambda cols_all, n: sum(jnp.where(g_idx == gi, cols_all[gi * EPG + n], 0.0)
                                   for gi in range(N_GROUPS))
    sel_g = [pick(sc_cols, n) for n in range(EPG)]
    s_g = [pick(s_cols, n) for n in range(EPG)]
    _, loc1 = _first_argmax(sel_g)
    _, loc2 = _first_argmax([jnp.where(loc1 == n, -jnp.inf, c) for n, c in enumerate(sel_g)])
    w_1 = sum(jnp.where(loc1 == n, s_g[n], 0.0) for n in range(EPG))
    w_2 = sum(jnp.where(loc2 == n, s_g[n], 0.0) for n in range(EPG))
    tot = w_1 + w_2
    w_1, w_2 = w_1 / tot, w_2 / tot
    e1 = g_idx * EPG + loc1
    e2 = g_idx * EPG + loc2
    expert = lax.broadcasted_iota(jnp.int32, (logits.shape[1], logits.shape[0]), 0)
    gate_t = jnp.where(expert == e1, w_1, 0.0) + jnp.where(expert == e2, w_2, 0.0)
    gate_ref[...] = gate_t.T


def _router(x, g, sc, sh, wr, eb, tm):
    N, D = x.shape
    tpg = (N // sc.arr.shape[1]) // tm
    return pl.pallas_call(
        _router_kernel,
        grid=(N // tm,),
        in_specs=[pl.BlockSpec((tm, D), lambda i: (i, 0)),
                  pl.BlockSpec((1, D), lambda i: (0, 0)),
                  _mod_spec(sc, tm, tpg), _mod_spec(sh, tm, tpg),
                  pl.BlockSpec((D, 128), lambda i: (0, 0)),
                  pl.BlockSpec((128, 1), lambda i: (0, 0))],
        out_specs=[pl.BlockSpec((tm, D), lambda i: (i, 0)),
                   pl.BlockSpec((tm, 128), lambda i: (i, 0))],
        out_shape=[jax.ShapeDtypeStruct((N, D), BF16), jax.ShapeDtypeStruct((N, 128), F32)],
        compiler_params=_cparams(("arbitrary",)),
    )(x, g, sc.arr, sh.arr, wr, eb)


def _moe_kernel(x_ref, h_ref, gate_ref, gt_ref, w1_ref, w3_ref, w2_ref, o_ref, acc_ref, act_ref):
    grp = pl.program_id(1)
    n_e = w1_ref.shape[0]

    @pl.when(grp == 0)
    def _():
        acc_ref[...] = jnp.zeros(acc_ref.shape, F32)

    h = h_ref[...]
    gate = gate_ref[...]
    lane = lax.broadcasted_iota(jnp.int32, gate.shape, 1)
    for el in range(n_e):
        a1 = jnp.dot(h, w1_ref[el], preferred_element_type=F32)
        a3 = jnp.dot(h, w3_ref[el], preferred_element_type=F32)
        gcol = jnp.sum(jnp.where(lane == grp * n_e + el, gate, 0.0), axis=-1, keepdims=True)
        act_ref[:, el * D_EXPERT:(el + 1) * D_EXPERT] = (
            (a1 * jax.nn.sigmoid(a1)) * a3 * gcol).astype(BF16)
    w2 = w2_ref[...].reshape(n_e * D_EXPERT, D_MODEL)
    acc_ref[...] += jnp.dot(act_ref[...], w2, preferred_element_type=F32)

    @pl.when(grp == pl.num_programs(1) - 1)
    def _():
        o_ref[...] = x_ref[...] + gt_ref[...] * acc_ref[...]


def _moe(x, h, gate, gt, w1, w3, w2, tm):
    N, D = x.shape
    tpg = (N // gt.arr.shape[1]) // tm
    layer = gt.layer
    eps = MOE_EXPERTS_PER_STEP
    return pl.pallas_call(
        _moe_kernel,
        grid=(N // tm, N_EXPERTS // eps),
        in_specs=[pl.BlockSpec((tm, D), lambda i, e: (i, 0)),
                  pl.BlockSpec((tm, D), lambda i, e: (i, 0)),
                  pl.BlockSpec((tm, 128), lambda i, e: (i, 0)),
                  _mod_spec(gt, tm, tpg),
                  pl.BlockSpec((None, eps, D, D_EXPERT), lambda i, e: (layer, e, 0, 0)),
                  pl.BlockSpec((None, eps, D, D_EXPERT), lambda i, e: (layer, e, 0, 0)),
                  pl.BlockSpec((None, eps, D_EXPERT, D), lambda i, e: (layer, e, 0, 0))],
        out_specs=pl.BlockSpec((tm, D), lambda i, e: (i, 0)),
        out_shape=jax.ShapeDtypeStruct((N, D), F32),
        scratch_shapes=[pltpu.VMEM((tm, D), F32), pltpu.VMEM((tm, eps * D_EXPERT), BF16)],
        compiler_params=_cparams(("arbitrary", "arbitrary")),
    )(x, h, gate, gt.arr, w1, w3, w2)


def _final_norm_kernel(x_ref, g_ref, o_ref):
    o_ref[...] = _rms(x_ref[...], NORM_EPS) * g_ref[...]


def _final_norm(x, g, tm):
    N, D = x.shape
    return pl.pallas_call(
        _final_norm_kernel,
        grid=(N // tm,),
        in_specs=[pl.BlockSpec((tm, D), lambda i: (i, 0)), pl.BlockSpec((1, D), lambda i: (0, 0))],
        out_specs=pl.BlockSpec((tm, D), lambda i: (i, 0)),
        out_shape=jax.ShapeDtypeStruct((N, D), F32),
        compiler_params=_cparams(("arbitrary",)),
    )(x, g)


def _group_layer(x, mods, lw, cfg, z0, s0, attend, stack):
    sh1, sc1, gt1, sh2, sc2, gt2 = mods
    tm = cfg["tm"]
    proj = _nm_matmul(x, lw["norm1_g"], sc1, sh1, lw["w_in"], cfg["tm_in"], cfg["tn_in"])
    r, lwd, kk, ka, k2, v, g, bon = _rw_prep(proj, z0, lw["mu"], lw["rw_vecs"], lw["wup"],
                                               cfg["tm_prep"], cfg["seq"])
    seq, n_seq = cfg["seq"], cfg["n_seq"]
    rows = max(seq, 8)
    wkv_in = (r, lwd, kk, ka, k2, v)
    if rows != seq:
        wkv_in = tuple(jnp.pad(a.reshape(n_seq, seq, RW_W), ((0, 0), (0, rows - seq), (0, 0))
                               ).reshape(n_seq * rows, RW_W) for a in wkv_in)
    y, s_fin = _wkv(*wkv_in, s0, n_seq, cfg["tb"], cfg["chunk"])
    if rows != seq:
        y = y.reshape(n_seq, rows, RW_W)[:, :seq].reshape(n_seq * seq, RW_W)
    qs, kr, *stacked = _rope(proj, cfg["rope_tabs"], cfg["tm_prep"], stack)
    oatt = attend(qs, kr, proj)
    x = _merge(x, y, bon, g, oatt, proj, gt1, lw["ln"], lw["w_rw_out"], lw["w_att_out"],
               lw["w_o"], tm)
    h2, gate = _router(x, lw["norm2_g"], sc2, sh2, lw["w_router"], lw["e_bias"], tm)
    x = _moe(x, h2, gate, gt2, lw["moe_w1"], lw["moe_w3"], lw["moe_w2"], cfg["tm_moe"])
    return x, proj, kr, stacked, s_fin


def kernel(x_prompt, x_sample, c_prompt, c_sample, cache_k, cache_v, page_table, state_shift, state_wkv, w_ada, b_ada, norm1_g, norm2_g, w_in, rw_mu, rw_w0, rw_w_up, rw_a0, rw_a_up, rw_g_up, rw_k_k, rw_k_a, rw_r_k, rw_ln_w, rw_ln_b, w_rw_out, att_lq1, att_lk1, att_lq2, att_lk2, att_subln, w_att_out, w_o, w_router, e_bias, moe_w1, moe_w3, moe_w2, normf_g):
    bp, tp, D = x_prompt.shape
    bs, ts, _ = x_sample.shape
    depth = w_in.shape[0]
    n_pages = page_table.shape[1]
    past_len = n_pages * PAGE
    n_p, n_s = bp * tp, bs * ts

    z_w, q_w, k_w, v_w, grw_w, gatt_w = jnp.split(
        w_in, [RW_PROJ, RW_PROJ + ATT_W, RW_PROJ + 2 * ATT_W, RW_PROJ + 3 * ATT_W,
               RW_PROJ + 3 * ATT_W + D], axis=-1)
    w_in_p = jnp.concatenate(
        [grw_w, gatt_w, q_w, k_w, v_w, z_w,
         jnp.zeros((depth, D, RW_PROJ_PAD - RW_PROJ), F32)], axis=-1).astype(BF16)
    mu_p = jnp.pad(rw_mu, ((0, 0), (0, RW_PROJ_PAD - RW_PROJ)))
    wup = jnp.zeros((depth, 3, LORA_PAD, RW_W), F32)
    wup = wup.at[:, 0, 0:DECAY_LORA].set(rw_w_up)
    wup = wup.at[:, 1, DECAY_LORA:DECAY_LORA + AAA_LORA].set(rw_a_up)
    wup = wup.at[:, 2, DECAY_LORA + AAA_LORA:DECAY_LORA + AAA_LORA + GATE_LORA].set(rw_g_up)
    wup = wup.astype(BF16)
    rw_vecs = jnp.stack([rw_w0, rw_a0, rw_k_k, rw_k_a, rw_r_k.reshape(depth, RW_W),
                         jnp.zeros_like(rw_w0), jnp.zeros_like(rw_w0), jnp.zeros_like(rw_w0)], axis=1)
    ln = jnp.stack([rw_ln_w, rw_ln_b], axis=1)
    lam_p = jnp.stack([att_lq1, att_lk1, att_lq2, att_lk2] + [jnp.zeros_like(att_lq1)] * 4, axis=1)
    w_router_p = jnp.pad(w_router, ((0, 0), (0, 128 - N_EXPERTS)))
    e_bias_p = jnp.pad(e_bias, (0, 128 - N_EXPERTS)).reshape(128, 1)
    w_rw_out_b, w_att_out_b, w_o_b = (w.astype(BF16) for w in (w_rw_out, w_att_out, w_o))
    moe_w1_b, moe_w3_b, moe_w2_b = (w.astype(BF16) for w in (moe_w1, moe_w3, moe_w2))

    n_c = bp + bs
    n_cp = -(-n_c // 8) * 8
    c_all = jnp.pad(jnp.concatenate([c_prompt, c_sample], axis=0), ((0, n_cp - n_c), (0, 0)))
    mod = _ada_mod(c_all, w_ada, b_ada)
    mod_p = mod[:, :bp].reshape(depth, bp, 1, 6 * D)
    mod_s = jnp.repeat(mod[:, bp:n_c], ts, axis=1).reshape(depth, 1, n_s, 6 * D)

    tm_p = min(512, tp)
    cfg_p = dict(tm=tm_p, tm_in=min(1024, tp), tn_in=1792, tm_prep=min(512, tp), seq=tp, n_seq=bp,
                 tb=min(256, tp), chunk=WKV_CHUNK, tm_moe=min(1024, tp),
                 rope_tabs=_rope_tables(jnp.arange(tp)))
    cfg_s = dict(tm=n_s, tm_in=n_s, tn_in=1792, tm_prep=n_s, seq=ts, n_seq=bs, tb=8, chunk=8, tm_moe=n_s,
                 rope_tabs=_rope_tables(jnp.tile(past_len + jnp.arange(ts), bs)))

    cache_kt = jnp.transpose(cache_k, (0, 1, 3, 4, 5, 2)).reshape(cache_k.shape[0], depth, ATT_W, PAGE)
    cache_v2 = cache_v.reshape(cache_v.shape[0], depth, PAGE * ATT_HEADS, HEAD_V)
    zp0 = jnp.zeros((bp, 1, RW_PROJ_PAD), F32)
    sp0 = jnp.zeros((bp, RW_HEADS, RW_HEAD, RW_HEAD), F32)

    xp = x_prompt.reshape(n_p, D)
    xs = x_sample.reshape(n_s, D)
    outs = {k: [] for k in ("zp", "sp", "ks", "vs", "zs", "ss")}
    kt_all = v_all = None
    for l in range(depth):
        lam_init = 0.8 - 0.6 * math.exp(-0.3 * l)
        lw = dict(norm1_g=norm1_g[l].reshape(1, D), norm2_g=norm2_g[l].reshape(1, D),
                  w_in=w_in_p, mu=mu_p[l].reshape(1, RW_PROJ_PAD), rw_vecs=rw_vecs[l],
                  wup=wup[l], ln=ln[l], w_rw_out=w_rw_out_b[l], w_att_out=w_att_out_b[l],
                  w_o=w_o_b[l], w_router=w_router_p, e_bias=e_bias_p,
                  moe_w1=moe_w1_b, moe_w3=moe_w3_b, moe_w2=moe_w2_b)
        subln = att_subln[l].reshape(1, HEAD_V)
        mods_p = [_Mod(mod_p, l, sec) for sec in range(6)]
        mods_s = [_Mod(mod_s, l, sec) for sec in range(6)]

        attend_p = lambda qs, kr, proj: _flash(qs, kr, proj, lam_p[l], subln, bp,
                                               min(512, tp), lam_init)
        xp, proj_p, _, (kt_all, v_all), s_fin = _group_layer(
            xp, mods_p, lw, cfg_p, zp0, sp0, attend_p, (l, depth, bp, kt_all, v_all))
        outs["zp"].append(proj_p.reshape(bp, tp, IN_W_PAD)[:, -1, COL_Z:COL_Z + RW_PROJ])
        outs["sp"].append(s_fin)

        def attend_s(qs, kr, proj):
            q_rows = jnp.tile(qs.reshape(bs, ts, ATT_W), (1, 2 * ATT_HEADS, 1))
            kn_t = jnp.pad(jnp.swapaxes(kr.reshape(bs, ts, ATT_W), 1, 2),
                           ((0, 0), (0, 0), (0, PAGE - ts)))
            vn = jnp.pad(proj[:, COL_V:COL_V + ATT_W].reshape(bs, ts * ATT_HEADS, HEAD_V),
                         ((0, 0), (0, (PAGE - ts) * ATT_HEADS), (0, 0)))
            return _decode_attn(q_rows, kn_t, vn, cache_kt, cache_v2, page_table, lam_p[l], subln, l,
                                ts, lam_init).reshape(n_s, ATT_W)

        z0_s = jnp.pad(jnp.repeat(state_shift[l], ts, axis=0), ((0, 0), (0, RW_PROJ_PAD - RW_PROJ)))
        xs, proj_s, kr_s, _, s_fin = _group_layer(xs, mods_s, lw, cfg_s, z0_s, state_wkv[l], attend_s,
                                                  None)
        outs["ks"].append(kr_s.reshape(bs, ts, ATT_HEADS, 2, HEAD_QK))
        outs["vs"].append(proj_s[:, COL_V:COL_V + ATT_W].reshape(bs, ts, ATT_HEADS, HEAD_V))
        outs["zs"].append(proj_s.reshape(bs, ts, IN_W_PAD)[:, -1, COL_Z:COL_Z + RW_PROJ])
        outs["ss"].append(s_fin)

    y_prompt = _final_norm(xp, normf_g.reshape(1, D), tm_p).reshape(bp, tp, D)
    y_sample = _final_norm(xs, normf_g.reshape(1, D), n_s).reshape(bs, ts, D)
    k_prompt = jnp.transpose(kt_all.reshape(bp, depth, ATT_HEADS, 2, HEAD_QK, tp), (0, 1, 5, 2, 3, 4))
    v_prompt = v_all.reshape(bp, depth, tp, ATT_HEADS, HEAD_V)
    return (y_prompt, y_sample, k_prompt, v_prompt,
            jnp.stack(outs["zp"], axis=0), jnp.stack(outs["sp"], axis=0),
            jnp.stack(outs["ks"], axis=1), jnp.stack(outs["vs"], axis=1),
            jnp.stack(outs["zs"], axis=0), jnp.stack(outs["ss"], axis=0))
```

```python
import functools
import math
from typing import NamedTuple

import jax
import jax.numpy as jnp
from jax import lax
from jax.experimental import pallas as pl
from jax.experimental.pallas import tpu as pltpu

F32 = jnp.float32
BF16 = jnp.bfloat16

D_MODEL = 1024
RW_HEAD = 64
RW_HEADS = 8
RW_W = RW_HEADS * RW_HEAD
DECAY_LORA = 32
AAA_LORA = 32
GATE_LORA = 96
RW_PROJ = 3 * RW_W + DECAY_LORA + AAA_LORA + GATE_LORA
RW_PROJ_PAD = 1792
LORA_PAD = RW_PROJ_PAD - 3 * RW_W
RW_GN_EPS = 64e-5
ATT_HEADS = 4
HEAD_QK = 64
HEAD_V = 128
ATT_W = 512
ROPE_DIM = HEAD_QK // 4
ROPE_THETA = 500000.0
SUBLN_EPS = 1e-5
NEG_INF = -1e30
N_EXPERTS = 16
N_GROUPS = 4
EPG = 4
D_EXPERT = 512
NORM_EPS = 1e-6
PAGE = 128
DECODE_PAGES_PER_STEP = 8
WKV_SEQS_PER_STEP = 2
MOE_EXPERTS_PER_STEP = 2
FLASH_HEADS_PER_STEP = 4
WKV_CHUNK = 64

COL_GATE = 0
COL_Q = 2048
COL_V = 3072
COL_Z = 3584
IN_W_PAD = COL_Z + RW_PROJ_PAD

VMEM_LIMIT = 56 * 1024 * 1024

_NN = (((1,), (0,)), ((), ()))
_NT = (((1,), (1,)), ((), ()))
_TN = (((0,), (0,)), ((), ()))


def _dot(a, b, dims=_NN):
    return lax.dot_general(a.astype(BF16), b.astype(BF16), dims, preferred_element_type=F32)


def _dot_hi(a, b, dims=_NN):
    return lax.dot_general(a, b, dims, preferred_element_type=F32,
                           precision=lax.Precision.HIGHEST)


def _cparams(sem):
    return pltpu.CompilerParams(dimension_semantics=sem, vmem_limit_bytes=VMEM_LIMIT)


def _rms(x, eps):
    return x * lax.rsqrt(jnp.mean(x * x, axis=-1, keepdims=True) + eps)


def _ada_kernel(c_ref, w_ref, b_ref, o_ref):
    c = c_ref[...]
    sc = c * jax.nn.sigmoid(c)
    o_ref[...] = _dot(sc, w_ref[...]) + b_ref[...]


def _ada_mod(c_all, w_ada, b_ada):
    L, D, N6 = w_ada.shape
    M = c_all.shape[0]
    tn = 1536
    return pl.pallas_call(
        _ada_kernel,
        grid=(L, N6 // tn),
        in_specs=[pl.BlockSpec((M, D), lambda l, j: (0, 0)),
                  pl.BlockSpec((None, D, tn), lambda l, j: (l, 0, j)),
                  pl.BlockSpec((None, 1, tn), lambda l, j: (l, 0, j))],
        out_specs=pl.BlockSpec((None, M, tn), lambda l, j: (l, 0, j)),
        out_shape=jax.ShapeDtypeStruct((L, M, N6), F32),
        compiler_params=_cparams(("arbitrary", "arbitrary")),
    )(c_all, w_ada, b_ada.reshape(L, 1, N6))


class _Mod(NamedTuple):
    arr: jax.Array
    layer: int
    sec: int


def _mod_spec(m, tm, tiles_per_group):
    rows = m.arr.shape[2]
    return pl.BlockSpec((None, None, rows, D_MODEL),
                        lambda i, *_: (m.layer, i // tiles_per_group, 0, m.sec))


def _nm_matmul_kernel(x_ref, g_ref, sc_ref, sh_ref, w_ref, o_ref, h_ref):
    @pl.when(pl.program_id(1) == 0)
    def _():
        y = _rms(x_ref[...], NORM_EPS) * g_ref[...]
        h_ref[...] = (y * (1.0 + sc_ref[...]) + sh_ref[...]).astype(BF16)

    o_ref[...] = jnp.dot(h_ref[...], w_ref[...], preferred_element_type=F32)


def _nm_matmul(x, g, sc, sh, w, tm, tn):
    N, D = x.shape
    n_out = w.shape[2]
    tpg = (N // sc.arr.shape[1]) // tm
    layer = sc.layer
    return pl.pallas_call(
        _nm_matmul_kernel,
        grid=(N // tm, n_out // tn),
        in_specs=[pl.BlockSpec((tm, D), lambda i, j: (i, 0)),
                  pl.BlockSpec((1, D), lambda i, j: (0, 0)),
                  _mod_spec(sc, tm, tpg), _mod_spec(sh, tm, tpg),
                  pl.BlockSpec((None, D, tn), lambda i, j: (layer, 0, j))],
        out_specs=pl.BlockSpec((tm, tn), lambda i, j: (i, j)),
        out_shape=jax.ShapeDtypeStruct((N, n_out), F32),
        scratch_shapes=[pltpu.VMEM((tm, D), BF16)],
        compiler_params=_cparams(("arbitrary", "arbitrary")),
    )(x, g, sc.arr, sh.arr, w)


def _head_sum(x):
    parts = []
    for h in range(RW_HEADS):
        s = jnp.sum(x[:, h * RW_HEAD:(h + 1) * RW_HEAD], axis=-1, keepdims=True)
        parts.append(jnp.broadcast_to(s, (x.shape[0], RW_HEAD)))
    return jnp.concatenate(parts, axis=-1)


def _rw_prep_kernel(z_ref, z0_ref, mu_ref, vec_ref, wup_ref,
                    r_ref, lw_ref, kk_ref, ka_ref, k2_ref, v_ref, g_ref, bon_ref,
                    carry_ref, *, tm, seq_len):
    i = pl.program_id(0)
    z = z_ref[...]
    rolled = pltpu.roll(z, 1, axis=0)
    row = lax.broadcasted_iota(jnp.int32, z.shape, 0)
    if seq_len >= tm:
        tiles_per_seq = seq_len // tm
        first = jnp.where(i % tiles_per_seq == 0, z0_ref[...], carry_ref[0:1, :])
        z_prev = jnp.where(row == 0, first, rolled)
        carry_ref[0:1, :] = z[tm - 1:tm, :]
    else:
        z_prev = jnp.where(row % seq_len == 0, z0_ref[...], rolled)
    zs = z + (z_prev - z) * mu_ref[...]
    r = zs[:, 0:RW_W]
    k = zs[:, RW_W:2 * RW_W]
    v = zs[:, 2 * RW_W:3 * RW_W]
    tail = zs[:, 3 * RW_W:]
    w0, a0, k_k, k_a, r_k = (vec_ref[j:j + 1, :] for j in range(5))
    dw = _dot(jnp.tanh(tail), wup_ref[0])
    da = _dot(tail, wup_ref[1])
    g = _dot(jax.nn.sigmoid(tail), wup_ref[2])
    t = -(w0 + dw)
    softplus = jnp.maximum(t, 0.0) + jnp.log1p(jnp.exp(-jnp.abs(t)))
    lw = -jnp.exp(-softplus - 0.5)
    a = jax.nn.sigmoid(a0 + da)
    kk = k * k_k
    kk = kk / jnp.maximum(jnp.sqrt(_head_sum(kk * kk)), 1e-12)
    k2 = k * (1.0 + (a - 1.0) * k_a)
    r_ref[...] = r
    lw_ref[...] = lw
    kk_ref[...] = kk
    ka_ref[...] = kk * a
    k2_ref[...] = k2
    v_ref[...] = v
    g_ref[...] = g
    bon_ref[...] = _head_sum(r * k2 * r_k) * v


def _rw_prep(proj, z0, mu, vecs, wup, tm, seq_len):
    N = proj.shape[0]
    kern = functools.partial(_rw_prep_kernel, tm=tm, seq_len=seq_len)
    if seq_len >= tm:
        tps = seq_len // tm
        z0_spec = pl.BlockSpec((None, 1, RW_PROJ_PAD), lambda i: (i // tps, 0, 0))
    else:
        z0_spec = pl.BlockSpec((tm, RW_PROJ_PAD), lambda i: (i, 0))
    o_spec = pl.BlockSpec((tm, RW_W), lambda i: (i, 0))
    return pl.pallas_call(
        kern,
        grid=(N // tm,),
        in_specs=[pl.BlockSpec((tm, RW_PROJ_PAD), lambda i: (i, COL_Z // RW_PROJ_PAD)),
                  z0_spec,
                  pl.BlockSpec((1, RW_PROJ_PAD), lambda i: (0, 0)),
                  pl.BlockSpec((8, RW_W), lambda i: (0, 0)),
                  pl.BlockSpec((3, LORA_PAD, RW_W), lambda i: (0, 0, 0))],
        out_specs=[o_spec] * 8,
        out_shape=[jax.ShapeDtypeStruct((N, RW_W), F32)] * 8,
        scratch_shapes=[pltpu.VMEM((8, RW_PROJ_PAD), F32)],
        compiler_params=_cparams(("arbitrary",)),
    )(proj, z0, mu, vecs, wup)


def _wkv_kernel(r_ref, lw_ref, kk_ref, ka_ref, k2_ref, v_ref, s0_ref, y_ref, sout_ref,
                s_ref, *, tb, C, sps):
    tblk = pl.program_id(1)
    n_pair = RW_HEADS // 2
    zero64 = jnp.zeros((RW_HEAD, RW_HEAD), F32)
    units = [(q, p) for q in range(sps) for p in range(n_pair)]
    nu = range(len(units))

    @pl.when(tblk == 0)
    def _():
        for n, (q, p) in enumerate(units):
            s_ref[n] = jnp.concatenate(
                [jnp.concatenate([s0_ref[q, 2 * p], zero64], axis=1),
                 jnp.concatenate([zero64, s0_ref[q, 2 * p + 1]], axis=1)], axis=0)

    C2 = 2 * C
    ri = lax.broadcasted_iota(jnp.int32, (C, C), 0)
    ci = lax.broadcasted_iota(jnp.int32, (C, C), 1)
    tri = (ri >= ci).astype(F32)
    r2 = lax.broadcasted_iota(jnp.int32, (C2, C2), 0)
    c2 = lax.broadcasted_iota(jnp.int32, (C2, C2), 1)
    low_incl = r2 >= c2
    low_strict = r2 > c2
    eye = (r2 == c2).astype(F32)
    lo_lanes = lax.broadcasted_iota(jnp.int32, (C, 128), 1) < RW_HEAD
    n_sq = int(math.log2(C)) - 1

    def stack(x, p):
        xp = x[:, p * 128:(p + 1) * 128]
        return jnp.concatenate([jnp.where(lo_lanes, xp, 0.0), jnp.where(lo_lanes, 0.0, xp)], axis=0)

    def chunk(c, carry):
        rows = pl.ds(pl.multiple_of(c * C, C), C)
        a_t, r_t, b_t, k_t, b_r, k_r, v_q, p_tot = [], [], [], [], [], [], [], []
        for q in range(sps):
            lw = lw_ref[q, rows, :]
            kk = kk_ref[q, rows, :]
            ka = ka_ref[q, rows, :]
            k2 = k2_ref[q, rows, :]
            cs = _dot_hi(tri, lw)
            tot = cs[C - 1:C, :]
            p_inv = jnp.exp(-cs)
            p_rem = jnp.exp(tot - cs)
            a_t.append(-kk * jnp.exp(cs - lw))
            r_t.append(r_ref[q, rows, :] * jnp.exp(cs))
            b_t.append(ka * p_inv)
            k_t.append(k2 * p_inv)
            b_r.append(ka * p_rem)
            k_r.append(k2 * p_rem)
            v_q.append(v_ref[q, rows, :])
            p_tot.append(jnp.exp(tot))
        a_s = [stack(a_t[q], p).astype(BF16) for q, p in units]
        r_s = [stack(r_t[q], p).astype(BF16) for q, p in units]
        b_s = [stack(b_t[q], p).astype(BF16) for q, p in units]
        k_s = [stack(k_t[q], p).astype(BF16) for q, p in units]
        bk_r = [jnp.concatenate([stack(b_r[q], p), stack(k_r[q], p)], axis=0).astype(BF16)
                for q, p in units]
        v_s = [stack(v_q[q], p).astype(BF16) for q, p in units]
        ar = [jnp.concatenate([a_s[n], r_s[n]], axis=0) for n in nu]
        gb = [_dot(ar[n], b_s[n], _NT) for n in nu]
        gk = [_dot(ar[n], k_s[n], _NT) for n in nu]
        mb = [jnp.where(low_strict, gb[n][:C2], 0.0) for n in nu]
        nb = [jnp.where(low_incl, gb[n][C2:], 0.0) for n in nu]
        mk = [jnp.where(low_strict, gk[n][:C2], 0.0) for n in nu]
        nk = [jnp.where(low_incl, gk[n][C2:], 0.0) for n in nu]
        tinv = [eye + mb[n] for n in nu]
        pw = [_dot(mb[n], mb[n]) for n in nu]
        for lvl in range(n_sq):
            if lvl < n_sq - 1:
                both = [_dot(jnp.concatenate([tinv[n], pw[n]], axis=0), pw[n]) for n in nu]
                tinv = [tinv[n] + both[n][:C2] for n in nu]
                pw = [both[n][C2:] for n in nu]
            else:
                tinv = [tinv[n] + _dot(tinv[n], pw[n]) for n in nu]
        mkv = [_dot(mk[n], v_s[n]) for n in nu]
        x = [_dot(tinv[n], jnp.concatenate([a_s[n], mkv[n].astype(BF16)], axis=1)) for n in nu]
        s_old = [s_ref[n] for n in nu]
        s_bf = [s.astype(BF16) for s in s_old]
        u = [_dot(x[n][:, :128], s_bf[n], _NT) + x[n][:, 128:] for n in nu]
        y = [_dot(r_s[n], s_bf[n], _NT) + _dot(nb[n], u[n]) + _dot(nk[n], v_s[n]) for n in nu]
        for n, (q, p) in enumerate(units):
            uv = jnp.concatenate([u[n].astype(BF16), v_s[n]], axis=0)
            s_ref[n] = s_old[n] * p_tot[q][:, p * 128:(p + 1) * 128] + _dot(uv, bk_r[n], _TN)
            y_ref[q, rows, p * 128:(p + 1) * 128] = y[n][:C] + y[n][C:]
        return carry

    lax.fori_loop(0, tb // C, chunk, 0, unroll=True)

    @pl.when(tblk == pl.num_programs(1) - 1)
    def _():
        for n, (q, p) in enumerate(units):
            s = s_ref[n]
            sout_ref[q, 2 * p] = s[:RW_HEAD, :RW_HEAD]
            sout_ref[q, 2 * p + 1] = s[RW_HEAD:, RW_HEAD:]


def _wkv(r, lw, kk, ka, k2, v, s0, n_seq, tb, C):
    N = r.shape[0]
    T = N // n_seq
    sps = math.gcd(n_seq, WKV_SEQS_PER_STEP)
    kern = functools.partial(_wkv_kernel, tb=tb, C=C, sps=sps)
    in_spec = pl.BlockSpec((sps, tb, RW_W), lambda b, t: (b, t, 0))
    s_spec = pl.BlockSpec((sps, RW_HEADS, RW_HEAD, RW_HEAD), lambda b, t: (b, 0, 0, 0))
    seq = lambda a: a.reshape(n_seq, T, RW_W)
    y, s_fin = pl.pallas_call(
        kern,
        grid=(n_seq // sps, T // tb),
        in_specs=[in_spec] * 6 + [s_spec],
        out_specs=[in_spec, s_spec],
        out_shape=[jax.ShapeDtypeStruct((n_seq, T, RW_W), F32),
                   jax.ShapeDtypeStruct((n_seq, RW_HEADS, RW_HEAD, RW_HEAD), F32)],
        scratch_shapes=[pltpu.VMEM((sps * RW_HEADS // 2, 128, 128), F32)],
        compiler_params=_cparams(("arbitrary", "arbitrary")),
    )(seq(r), seq(lw), seq(kk), seq(ka), seq(k2), seq(v), s0)
    return y.reshape(N, RW_W), s_fin


def _rope_kernel(qk_ref, cos_ref, s1_ref, s2_ref, *rest, stacked):
    if stacked:
        v_ref, q_ref, k_ref, kt_ref, v4_ref = rest[0], *rest[-4:]
    else:
        q_ref, k_ref = rest
    cos, s1, s2 = cos_ref[...], s1_ref[...], s2_ref[...]
    half = ROPE_DIM // 2
    for dst, base, scale in ((q_ref, 0, HEAD_QK ** -0.5), (k_ref, ATT_W, 1.0)):
        for cblk in range(ATT_W // 128):
            x = qk_ref[:, base + cblk * 128: base + (cblk + 1) * 128]
            up = pltpu.roll(x, 128 - half, axis=1)
            dn = pltpu.roll(x, half, axis=1)
            y = x * cos + up * s1 + dn * s2
            dst[:, cblk * 128:(cblk + 1) * 128] = y * scale if scale != 1.0 else y
    if stacked:
        kt_ref[...] = k_ref[...].T
        rows = v_ref.shape[0]
        for h in range(ATT_HEADS):
            v4_ref[pl.ds(h, rows, stride=ATT_HEADS), :] = v_ref[:, h * HEAD_V:(h + 1) * HEAD_V]


def _rope(proj, tabs, tm, stack=None):
    N = proj.shape[0]
    ntab = tabs[0].shape[0] // tm
    t_spec = pl.BlockSpec((tm, 128), lambda i: (i % ntab, 0))
    o_spec = pl.BlockSpec((tm, ATT_W), lambda i: (i, 0))
    in_specs = [pl.BlockSpec((tm, 2 * ATT_W), lambda i: (i, COL_Q // (2 * ATT_W))),
                t_spec, t_spec, t_spec]
    out_specs = [o_spec, o_spec]
    out_shape = [jax.ShapeDtypeStruct((N, ATT_W), F32)] * 2
    operands = [proj, *tabs]
    aliases = {}
    if stack is not None:
        layer, depth, n_seq, kt_buf, v_buf = stack
        T = N // n_seq
        tps = T // tm
        in_specs.append(pl.BlockSpec((tm, ATT_W), lambda i: (i, COL_V // ATT_W)))
        operands.append(proj)
        out_specs += [pl.BlockSpec((None, None, ATT_W, tm), lambda i: (i // tps, layer, 0, i % tps)),
                      pl.BlockSpec((None, None, tm * ATT_HEADS, HEAD_V),
                                   lambda i: (i // tps, layer, i % tps, 0))]
        out_shape += [jax.ShapeDtypeStruct((n_seq, depth, ATT_W, T), F32),
                      jax.ShapeDtypeStruct((n_seq, depth, T * ATT_HEADS, HEAD_V), F32)]
        if kt_buf is not None:
            aliases = {len(operands): 2, len(operands) + 1: 3}
            in_specs += [pl.BlockSpec(memory_space=pl.ANY)] * 2
            operands += [kt_buf, v_buf]
    return pl.pallas_call(
        functools.partial(_rope_kernel, stacked=stack is not None),
        grid=(N // tm,),
        in_specs=in_specs,
        out_specs=out_specs,
        out_shape=out_shape,
        input_output_aliases=aliases,
        compiler_params=_cparams(("arbitrary",)),
    )(*operands)


def _rope_tables(pos):
    half = ROPE_DIM // 2
    inv = ROPE_THETA ** (-jnp.arange(0, ROPE_DIM, 2, dtype=F32) / ROPE_DIM)
    ang = pos.astype(F32)[:, None] * inv[None, :]
    cos, sin = jnp.cos(ang), jnp.sin(ang)
    n = pos.shape[0]
    one = jnp.ones((n, HEAD_QK - ROPE_DIM), F32)
    zero = jnp.zeros((n, HEAD_QK - ROPE_DIM), F32)
    zh = jnp.zeros((n, half), F32)
    c64 = jnp.concatenate([cos, cos, one], axis=1)
    s1 = jnp.concatenate([-sin, zh, zero], axis=1)
    s2 = jnp.concatenate([zh, sin, zero], axis=1)
    return tuple(jnp.concatenate([t, t], axis=1) for t in (c64, s1, s2))


def _lambda(lp_ref, lam_init):
    lp = lp_ref[...]
    d1 = jnp.sum(lp[0:1, :] * lp[1:2, :], axis=-1, keepdims=True)
    d2 = jnp.sum(lp[2:3, :] * lp[3:4, :], axis=-1, keepdims=True)
    return jnp.exp(d1) - jnp.exp(d2) + lam_init


def _flash_kernel(qi_ref, kj_ref, q_ref, k_ref, v_ref, lp_ref, sub_ref, o_ref,
                  qs_ref, m_ref, l_ref, acc_ref, *, tq, hps, lam_init):
    t = pl.program_id(2)
    i = qi_ref[t]
    j = kj_ref[t]
    heads = range(hps)
    cols = lambda hh: slice(hh * 128, (hh + 1) * 128)

    @pl.when(j == 0)
    def _():
        q = q_ref[...] * math.log2(math.e)
        lo = lax.broadcasted_iota(jnp.int32, (tq, 128), 1) < HEAD_QK
        for hh in heads:
            qh = q[:, cols(hh)]
            qs_ref[hh] = jnp.concatenate([jnp.where(lo, qh, 0.0), jnp.where(lo, 0.0, qh)],
                                         axis=0).astype(BF16)
        m_ref[...] = jnp.full(m_ref.shape, NEG_INF, F32)
        l_ref[...] = jnp.zeros(l_ref.shape, F32)
        acc_ref[...] = jnp.zeros(acc_ref.shape, F32)

    def step(diagonal):
        k = k_ref[...].astype(BF16)
        v = v_ref[...].astype(BF16)
        s = [lax.dot_general(k[:, cols(hh)], qs_ref[hh], _NT, preferred_element_type=F32)
             for hh in heads]
        if diagonal:
            key = lax.broadcasted_iota(jnp.int32, s[0].shape, 0)
            qry = lax.broadcasted_iota(jnp.int32, s[0].shape, 1)
            keep = key <= jnp.where(qry >= tq, qry - tq, qry)
            s = [jnp.where(keep, s[hh], NEG_INF) for hh in heads]
        m_old = [m_ref[hh] for hh in heads]
        m_new = [jnp.maximum(m_old[hh], jnp.max(s[hh], axis=0, keepdims=True)) for hh in heads]
        alpha = [jnp.exp2(m_old[hh] - m_new[hh]) for hh in heads]
        p = [jnp.exp2(s[hh] - m_new[hh]) for hh in heads]
        pv = [lax.dot_general(v[:, cols(hh)], p[hh].astype(BF16), _TN, preferred_element_type=F32)
              for hh in heads]
        for hh in heads:
            l_ref[hh] = alpha[hh] * l_ref[hh] + jnp.sum(p[hh], axis=0, keepdims=True)
            acc_ref[hh] = alpha[hh] * acc_ref[hh] + pv[hh]
            m_ref[hh] = m_new[hh]

    @pl.when(j < i)
    def _():
        step(False)

    @pl.when(j == i)
    def _():
        step(True)
        lam = _lambda(lp_ref, lam_init)
        for hh in heads:
            on = acc_ref[hh] / l_ref[hh]
            o_t = on[:, :tq] - lam * on[:, tq:]
            ms = jnp.mean(o_t * o_t, axis=0, keepdims=True)
            o_t = o_t * lax.rsqrt(ms + SUBLN_EPS) * sub_ref[...] * (1.0 - lam_init)
            o_ref[:, cols(hh)] = o_t.T


def _flash(qs, kr, proj, lam_p, subln, n_seq, tq, lam_init):
    N = qs.shape[0]
    nq = (N // n_seq) // tq
    hps = FLASH_HEADS_PER_STEP
    wid = hps * 128
    kern = functools.partial(_flash_kernel, tq=tq, hps=hps, lam_init=lam_init)
    vcol = COL_V // wid
    pairs = [(i, j) for i in range(nq) for j in range(i + 1)]
    qi = jnp.asarray([p[0] for p in pairs], jnp.int32)
    kj = jnp.asarray([p[1] for p in pairs], jnp.int32)
    grid_spec = pltpu.PrefetchScalarGridSpec(
        num_scalar_prefetch=2,
        grid=(n_seq, ATT_HEADS // hps, len(pairs)),
        in_specs=[pl.BlockSpec((tq, wid), lambda b, h, t, qi, kj: (b * nq + qi[t], h)),
                  pl.BlockSpec((tq, wid), lambda b, h, t, qi, kj: (b * nq + kj[t], h)),
                  pl.BlockSpec((tq, wid), lambda b, h, t, qi, kj: (b * nq + kj[t], vcol + h)),
                  pl.BlockSpec((8, HEAD_QK), lambda b, h, t, qi, kj: (0, 0)),
                  pl.BlockSpec((HEAD_V, 1), lambda b, h, t, qi, kj: (0, 0))],
        out_specs=pl.BlockSpec((tq, wid), lambda b, h, t, qi, kj: (b * nq + qi[t], h)),
        scratch_shapes=[pltpu.VMEM((hps, 2 * tq, 128), BF16), pltpu.VMEM((hps, 1, 2 * tq), F32),
                        pltpu.VMEM((hps, 1, 2 * tq), F32), pltpu.VMEM((hps, HEAD_V, 2 * tq), F32)],
    )
    return pl.pallas_call(
        kern, grid_spec=grid_spec,
        out_shape=jax.ShapeDtypeStruct((N, ATT_W), F32),
        compiler_params=_cparams(("arbitrary",) * 3),
    )(qi, kj, qs, kr, proj, lam_p, subln.reshape(HEAD_V, 1))


def _decode_kernel(pt_ref, q_ref, *refs, ts, n_steps, pps, lam_init):
    k_refs, v_refs = refs[:pps], refs[pps:2 * pps]
    kn_ref, vn_ref, lp_ref, sub_ref, o_ref, qrow, m_s, l_s, acc = refs[2 * pps:]
    p = pl.program_id(1)
    nrow = 2 * ATT_HEADS * ts
    rows_per_head = 2 * ts

    @pl.when(p == 0)
    def _():
        row = lax.broadcasted_iota(jnp.int32, (nrow, ATT_W), 0)
        lane = lax.broadcasted_iota(jnp.int32, (nrow, ATT_W), 1)
        qrow[...] = jnp.where(lane // HEAD_QK == row // ts, q_ref[...], 0.0).astype(BF16)
        m_s[...] = jnp.full(m_s.shape, NEG_INF, F32)
        l_s[...] = jnp.zeros(l_s.shape, F32)
        acc[...] = jnp.zeros(acc.shape, F32)

    def attend(k_refs, v_refs, own):
        kt = jnp.concatenate([r[...] for r in k_refs], axis=1)
        s = jnp.dot(qrow[...], kt.astype(BF16), preferred_element_type=F32)
        if own:
            r2 = lax.broadcasted_iota(jnp.int32, s.shape, 0)
            c2 = lax.broadcasted_iota(jnp.int32, s.shape, 1)
            s = jnp.where(c2 <= r2 % ts, s, NEG_INF)
        m_new = jnp.maximum(m_s[...], jnp.max(s, axis=-1, keepdims=True))
        alpha = jnp.exp(m_s[...] - m_new)
        pr = jnp.exp(s - m_new)
        l_s[...] = alpha * l_s[...] + jnp.sum(pr, axis=-1, keepdims=True)
        pr = pr.astype(BF16)
        pv = []
        for h in range(ATT_HEADS):
            vh = jnp.concatenate([r[pl.ds(h, PAGE, stride=ATT_HEADS), :] for r in v_refs], axis=0)
            pv.append(jnp.dot(pr[h * rows_per_head:(h + 1) * rows_per_head], vh.astype(BF16),
                              preferred_element_type=F32))
        acc[...] = alpha * acc[...] + jnp.concatenate(pv, axis=0)
        m_s[...] = m_new

    @pl.when(p < n_steps - 1)
    def _():
        attend(k_refs, v_refs, False)

    @pl.when(p == n_steps - 1)
    def _():
        attend((kn_ref,), (vn_ref,), True)
        lam = _lambda(lp_ref, lam_init)
        on = acc[...] / l_s[...]
        outs = []
        for h in range(ATT_HEADS):
            blk = on[h * rows_per_head:(h + 1) * rows_per_head]
            oh = blk[:ts] - lam * blk[ts:]
            outs.append(_rms(oh, SUBLN_EPS) * sub_ref[...] * (1.0 - lam_init))
        o_ref[...] = jnp.concatenate(outs, axis=-1)


def _decode_attn(q_rows, kn_t, vn, cache_kt, cache_v, page_table, lam_p, subln, layer, ts, lam_init):
    bs, nrow, _ = q_rows.shape
    n_pages = page_table.shape[1]
    pps = math.gcd(n_pages, DECODE_PAGES_PER_STEP)
    n_steps = n_pages // pps + 1
    kern = functools.partial(_decode_kernel, ts=ts, n_steps=n_steps, pps=pps, lam_init=lam_init)

    def page(which):
        return lambda b, p, pt: (pt[b, jnp.minimum(pps * p + which, n_pages - 1)], layer, 0, 0)

    k_specs = [pl.BlockSpec((None, None, ATT_W, PAGE), page(w)) for w in range(pps)]
    v_specs = [pl.BlockSpec((None, None, PAGE * ATT_HEADS, HEAD_V), page(w)) for w in range(pps)]
    grid_spec = pltpu.PrefetchScalarGridSpec(
        num_scalar_prefetch=1,
        grid=(bs, n_steps),
        in_specs=[pl.BlockSpec((None, nrow, ATT_W), lambda b, p, pt: (b, 0, 0))] + k_specs + v_specs + [
            pl.BlockSpec((None, ATT_W, PAGE), lambda b, p, pt: (b, 0, 0)),
            pl.BlockSpec((None, PAGE * ATT_HEADS, HEAD_V), lambda b, p, pt: (b, 0, 0)),
            pl.BlockSpec((8, HEAD_QK), lambda b, p, pt: (0, 0)),
            pl.BlockSpec((1, HEAD_V), lambda b, p, pt: (0, 0))],
        out_specs=pl.BlockSpec((None, ts, ATT_W), lambda b, p, pt: (b, 0, 0)),
        scratch_shapes=[pltpu.VMEM((nrow, ATT_W), BF16), pltpu.VMEM((nrow, 1), F32),
                        pltpu.VMEM((nrow, 1), F32), pltpu.VMEM((nrow, HEAD_V), F32)],
    )
    return pl.pallas_call(
        kern, grid_spec=grid_spec,
        out_shape=jax.ShapeDtypeStruct((bs, ts, ATT_W), F32),
        compiler_params=_cparams(("arbitrary", "arbitrary")),
    )(page_table, q_rows, *([cache_kt] * pps), *([cache_v] * pps), kn_t, vn, lam_p, subln)


def _merge_kernel(x_ref, y_ref, bon_ref, g_ref, oatt_ref, gates_ref, gt_ref, ln_ref,
                  wrw_ref, watt_ref, wo_ref, o_ref):
    y = y_ref[...]
    n = y.shape[0]
    parts = []
    for h in range(RW_HEADS):
        yh = y[:, h * RW_HEAD:(h + 1) * RW_HEAD]
        mu = jnp.mean(yh, axis=-1, keepdims=True)
        d = yh - mu
        var = jnp.mean(d * d, axis=-1, keepdims=True)
        parts.append(d * lax.rsqrt(var + RW_GN_EPS))
    yn = jnp.concatenate(parts, axis=-1) * ln_ref[0:1, :] + ln_ref[1:2, :]
    out_rw = (yn + bon_ref[...]) * g_ref[...]
    y_rw = _dot(out_rw, wrw_ref[...])
    y_att = _dot(oatt_ref[...], watt_ref[...])
    gates = gates_ref[...]
    merged = (jax.nn.sigmoid(gates[:, :D_MODEL]) * y_rw
              + jax.nn.sigmoid(gates[:, D_MODEL:]) * y_att)
    o_ref[...] = x_ref[...] + gt_ref[...] * _dot(merged, wo_ref[...])


def _merge(x, y, bon, g, oatt, proj, gt, ln, wrw, watt, wo, tm):
    N = x.shape[0]
    tpg = (N // gt.arr.shape[1]) // tm
    s512 = pl.BlockSpec((tm, RW_W), lambda i: (i, 0))
    full = lambda a: pl.BlockSpec(a.shape, lambda i: (0,) * a.ndim)
    return pl.pallas_call(
        _merge_kernel,
        grid=(N // tm,),
        in_specs=[pl.BlockSpec((tm, D_MODEL), lambda i: (i, 0)), s512, s512, s512, s512,
                  pl.BlockSpec((tm, 2 * D_MODEL), lambda i: (i, 0)),
                  _mod_spec(gt, tm, tpg), full(ln), full(wrw), full(watt), full(wo)],
        out_specs=pl.BlockSpec((tm, D_MODEL), lambda i: (i, 0)),
        out_shape=jax.ShapeDtypeStruct((N, D_MODEL), F32),
        compiler_params=_cparams(("arbitrary",)),
    )(x, y, bon, g, oatt, proj, gt.arr, ln, wrw, watt, wo)


def _first_argmax(cols):
    best = cols[0]
    idx = jnp.zeros(best.shape, jnp.int32)
    for n, c in enumerate(cols[1:], start=1):
        take = c > best
        best = jnp.where(take, c, best)
        idx = jnp.where(take, n, idx)
    return best, idx


def _router_kernel(x_ref, g_ref, sc_ref, sh_ref, wr_ref, eb_ref, h_ref, gate_ref):
    y = _rms(x_ref[...], NORM_EPS) * g_ref[...]
    h = y * (1.0 + sc_ref[...]) + sh_ref[...]
    h_hi = h.astype(BF16)
    h_ref[...] = h_hi
    h_lo = (h - h_hi.astype(F32)).astype(BF16)
    w = wr_ref[...]
    w_hi = w.astype(BF16)
    w_lo = (w - w_hi.astype(F32)).astype(BF16)
    dot = functools.partial(jnp.dot, preferred_element_type=F32)
    logits = dot(h_hi, w_hi) + (dot(h_lo, w_hi) + dot(h_hi, w_lo))
    s = jax.nn.sigmoid(logits.T[:N_EXPERTS])
    sel = s + eb_ref[...][:N_EXPERTS]
    sc_cols = [sel[e:e + 1, :] for e in range(N_EXPERTS)]
    s_cols = [s[e:e + 1, :] for e in range(N_EXPERTS)]
    grp = []
    for gi in range(N_GROUPS):
        cols = sc_cols[gi * EPG:(gi + 1) * EPG]
        m1, i1 = _first_argmax(cols)
        rest = [jnp.where(i1 == n, -jnp.inf, c) for n, c in enumerate(cols)]
        m2, _ = _first_argmax(rest)
        grp.append(m1 + m2)
    _, g_idx = _first_argmax(grp)
    pick = lambda cols_all, n: sum(jnp.where(g_idx == gi, cols_all[gi * EPG + n], 0.0)
                                   for gi in range(N_GROUPS))
    sel_g = [pick(sc_cols, n) for n in range(EPG)]
    s_g = [pick(s_cols, n) for n in range(EPG)]
    _, loc1 = _first_argmax(sel_g)
    _, loc2 = _first_argmax([jnp.where(loc1 == n, -jnp.inf, c) for n, c in enumerate(sel_g)])
    w_1 = sum(jnp.where(loc1 == n, s_g[n], 0.0) for n in range(EPG))
    w_2 = sum(jnp.where(loc2 == n, s_g[n], 0.0) for n in range(EPG))
    tot = w_1 + w_2
    w_1, w_2 = w_1 / tot, w_2 / tot
    e1 = g_idx * EPG + loc1
    e2 = g_idx * EPG + loc2
    expert = lax.broadcasted_iota(jnp.int32, (logits.shape[1], logits.shape[0]), 0)
    gate_t = jnp.where(expert == e1, w_1, 0.0) + jnp.where(expert == e2, w_2, 0.0)
    gate_ref[...] = gate_t.T


def _router(x, g, sc, sh, wr, eb, tm):
    N, D = x.shape
    tpg = (N // sc.arr.shape[1]) // tm
    return pl.pallas_call(
        _router_kernel,
        grid=(N // tm,),
        in_specs=[pl.BlockSpec((tm, D), lambda i: (i, 0)),
                  pl.BlockSpec((1, D), lambda i: (0, 0)),
                  _mod_spec(sc, tm, tpg), _mod_spec(sh, tm, tpg),
                  pl.BlockSpec((D, 128), lambda i: (0, 0)),
                  pl.BlockSpec((128, 1), lambda i: (0, 0))],
        out_specs=[pl.BlockSpec((tm, D), lambda i: (i, 0)),
                   pl.BlockSpec((tm, 128), lambda i: (i, 0))],
        out_shape=[jax.ShapeDtypeStruct((N, D), BF16), jax.ShapeDtypeStruct((N, 128), F32)],
        compiler_params=_cparams(("arbitrary",)),
    )(x, g, sc.arr, sh.arr, wr, eb)


def _moe_kernel(x_ref, h_ref, gate_ref, gt_ref, w1_ref, w3_ref, w2_ref, o_ref, acc_ref, act_ref):
    grp = pl.program_id(1)
    n_e = w1_ref.shape[0]

    @pl.when(grp == 0)
    def _():
        acc_ref[...] = jnp.zeros(acc_ref.shape, F32)

    h = h_ref[...]
    gate = gate_ref[...]
    lane = lax.broadcasted_iota(jnp.int32, gate.shape, 1)
    for el in range(n_e):
        a1 = jnp.dot(h, w1_ref[el], preferred_element_type=F32)
        a3 = jnp.dot(h, w3_ref[el], preferred_element_type=F32)
        gcol = jnp.sum(jnp.where(lane == grp * n_e + el, gate, 0.0), axis=-1, keepdims=True)
        act_ref[:, el * D_EXPERT:(el + 1) * D_EXPERT] = (
            (a1 * jax.nn.sigmoid(a1)) * a3 * gcol).astype(BF16)
    w2 = w2_ref[...].reshape(n_e * D_EXPERT, D_MODEL)
    acc_ref[...] += jnp.dot(act_ref[...], w2, preferred_element_type=F32)

    @pl.when(grp == pl.num_programs(1) - 1)
    def _():
        o_ref[...] = x_ref[...] + gt_ref[...] * acc_ref[...]


def _moe(x, h, gate, gt, w1, w3, w2, tm):
    N, D = x.shape
    tpg = (N // gt.arr.shape[1]) // tm
    layer = gt.layer
    eps = MOE_EXPERTS_PER_STEP
    return pl.pallas_call(
        _moe_kernel,
        grid=(N // tm, N_EXPERTS // eps),
        in_specs=[pl.BlockSpec((tm, D), lambda i, e: (i, 0)),
                  pl.BlockSpec((tm, D), lambda i, e: (i, 0)),
                  pl.BlockSpec((tm, 128), lambda i, e: (i, 0)),
                  _mod_spec(gt, tm, tpg),
                  pl.BlockSpec((None, eps, D, D_EXPERT), lambda i, e: (layer, e, 0, 0)),
                  pl.BlockSpec((None, eps, D, D_EXPERT), lambda i, e: (layer, e, 0, 0)),
                  pl.BlockSpec((None, eps, D_EXPERT, D), lambda i, e: (layer, e, 0, 0))],
        out_specs=pl.BlockSpec((tm, D), lambda i, e: (i, 0)),
        out_shape=jax.ShapeDtypeStruct((N, D), F32),
        scratch_shapes=[pltpu.VMEM((tm, D), F32), pltpu.VMEM((tm, eps * D_EXPERT), BF16)],
        compiler_params=_cparams(("arbitrary", "arbitrary")),
    )(x, h, gate, gt.arr, w1, w3, w2)


def _final_norm_kernel(x_ref, g_ref, o_ref):
    o_ref[...] = _rms(x_ref[...], NORM_EPS) * g_ref[...]


def _final_norm(x, g, tm):
    N, D = x.shape
    return pl.pallas_call(
        _final_norm_kernel,
        grid=(N // tm,),
        in_specs=[pl.BlockSpec((tm, D), lambda i: (i, 0)), pl.BlockSpec((1, D), lambda i: (0, 0))],
        out_specs=pl.BlockSpec((tm, D), lambda i: (i, 0)),
        out_shape=jax.ShapeDtypeStruct((N, D), F32),
        compiler_params=_cparams(("arbitrary",)),
    )(x, g)


def _group_layer(x, mods, lw, cfg, z0, s0, attend, stack):
    sh1, sc1, gt1, sh2, sc2, gt2 = mods
    tm = cfg["tm"]
    proj = _nm_matmul(x, lw["norm1_g"], sc1, sh1, lw["w_in"], cfg["tm_in"], cfg["tn_in"])
    r, lwd, kk, ka, k2, v, g, bon = _rw_prep(proj, z0, lw["mu"], lw["rw_vecs"], lw["wup"],
                                               cfg["tm_prep"], cfg["seq"])
    seq, n_seq = cfg["seq"], cfg["n_seq"]
    rows = max(seq, 8)
    wkv_in = (r, lwd, kk, ka, k2, v)
    if rows != seq:
        wkv_in = tuple(jnp.pad(a.reshape(n_seq, seq, RW_W), ((0, 0), (0, rows - seq), (0, 0))
                               ).reshape(n_seq * rows, RW_W) for a in wkv_in)
    y, s_fin = _wkv(*wkv_in, s0, n_seq, cfg["tb"], cfg["chunk"])
    if rows != seq:
        y = y.reshape(n_seq, rows, RW_W)[:, :seq].reshape(n_seq * seq, RW_W)
    qs, kr, *stacked = _rope(proj, cfg["rope_tabs"], cfg["tm_prep"], stack)
    oatt = attend(qs, kr, proj)
    x = _merge(x, y, bon, g, oatt, proj, gt1, lw["ln"], lw["w_rw_out"], lw["w_att_out"],
               lw["w_o"], tm)
    h2, gate = _router(x, lw["norm2_g"], sc2, sh2, lw["w_router"], lw["e_bias"], tm)
    x = _moe(x, h2, gate, gt2, lw["moe_w1"], lw["moe_w3"], lw["moe_w2"], cfg["tm_moe"])
    return x, proj, kr, stacked, s_fin


def kernel(x_prompt, x_sample, c_prompt, c_sample, cache_k, cache_v, page_table, state_shift, state_wkv, w_ada, b_ada, norm1_g, norm2_g, w_in, rw_mu, rw_w0, rw_w_up, rw_a0, rw_a_up, rw_g_up, rw_k_k, rw_k_a, rw_r_k, rw_ln_w, rw_ln_b, w_rw_out, att_lq1, att_lk1, att_lq2, att_lk2, att_subln, w_att_out, w_o, w_router, e_bias, moe_w1, moe_w3, moe_w2, normf_g):
    bp, tp, D = x_prompt.shape
    bs, ts, _ = x_sample.shape
    depth = w_in.shape[0]
    n_pages = page_table.shape[1]
    past_len = n_pages * PAGE
    n_p, n_s = bp * tp, bs * ts

    z_w, q_w, k_w, v_w, grw_w, gatt_w = jnp.split(
        w_in, [RW_PROJ, RW_PROJ + ATT_W, RW_PROJ + 2 * ATT_W, RW_PROJ + 3 * ATT_W,
               RW_PROJ + 3 * ATT_W + D], axis=-1)
    w_in_p = jnp.concatenate(
        [grw_w, gatt_w, q_w, k_w, v_w, z_w,
         jnp.zeros((depth, D, RW_PROJ_PAD - RW_PROJ), F32)], axis=-1).astype(BF16)
    mu_p = jnp.pad(rw_mu, ((0, 0), (0, RW_PROJ_PAD - RW_PROJ)))
    wup = jnp.zeros((depth, 3, LORA_PAD, RW_W), F32)
    wup = wup.at[:, 0, 0:DECAY_LORA].set(rw_w_up)
    wup = wup.at[:, 1, DECAY_LORA:DECAY_LORA + AAA_LORA].set(rw_a_up)
    wup = wup.at[:, 2, DECAY_LORA + AAA_LORA:DECAY_LORA + AAA_LORA + GATE_LORA].set(rw_g_up)
    wup = wup.astype(BF16)
    rw_vecs = jnp.stack([rw_w0, rw_a0, rw_k_k, rw_k_a, rw_r_k.reshape(depth, RW_W),
                         jnp.zeros_like(rw_w0), jnp.zeros_like(rw_w0), jnp.zeros_like(rw_w0)], axis=1)
    ln = jnp.stack([rw_ln_w, rw_ln_b], axis=1)
    lam_p = jnp.stack([att_lq1, att_lk1, att_lq2, att_lk2] + [jnp.zeros_like(att_lq1)] * 4, axis=1)
    w_router_p = jnp.pad(w_router, ((0, 0), (0, 128 - N_EXPERTS)))
    e_bias_p = jnp.pad(e_bias, (0, 128 - N_EXPERTS)).reshape(128, 1)
    w_rw_out_b, w_att_out_b, w_o_b = (w.astype(BF16) for w in (w_rw_out, w_att_out, w_o))
    moe_w1_b, moe_w3_b, moe_w2_b = (w.astype(BF16) for w in (moe_w1, moe_w3, moe_w2))

    n_c = bp + bs
    n_cp = -(-n_c // 8) * 8
    c_all = jnp.pad(jnp.concatenate([c_prompt, c_sample], axis=0), ((0, n_cp - n_c), (0, 0)))
    mod = _ada_mod(c_all, w_ada, b_ada)
    mod_p = mod[:, :bp].reshape(depth, bp, 1, 6 * D)
    mod_s = jnp.repeat(mod[:, bp:n_c], ts, axis=1).reshape(depth, 1, n_s, 6 * D)

    tm_p = min(512, tp)
    cfg_p = dict(tm=tm_p, tm_in=min(1024, tp), tn_in=1792, tm_prep=min(512, tp), seq=tp, n_seq=bp,
                 tb=min(256, tp), chunk=WKV_CHUNK, tm_moe=min(1024, tp),
                 rope_tabs=_rope_tables(jnp.arange(tp)))
    cfg_s = dict(tm=n_s, tm_in=n_s, tn_in=1792, tm_prep=n_s, seq=ts, n_seq=bs, tb=8, chunk=8, tm_moe=n_s,
                 rope_tabs=_rope_tables(jnp.tile(past_len + jnp.arange(ts), bs)))

    cache_kt = jnp.transpose(cache_k, (0, 1, 3, 4, 5, 2)).reshape(cache_k.shape[0], depth, ATT_W, PAGE)
    cache_v2 = cache_v.reshape(cache_v.shape[0], depth, PAGE * ATT_HEADS, HEAD_V)
    zp0 = jnp.zeros((bp, 1, RW_PROJ_PAD), F32)
    sp0 = jnp.zeros((bp, RW_HEADS, RW_HEAD, RW_HEAD), F32)

    xp = x_prompt.reshape(n_p, D)
    xs = x_sample.reshape(n_s, D)
    outs = {k: [] for k in ("zp", "sp", "ks", "vs", "zs", "ss")}
    kt_all = v_all = None
    for l in range(depth):
        lam_init = 0.8 - 0.6 * math.exp(-0.3 * l)
        lw = dict(norm1_g=norm1_g[l].reshape(1, D), norm2_g=norm2_g[l].reshape(1, D),
                  w_in=w_in_p, mu=mu_p[l].reshape(1, RW_PROJ_PAD), rw_vecs=rw_vecs[l],
                  wup=wup[l], ln=ln[l], w_rw_out=w_rw_out_b[l], w_att_out=w_att_out_b[l],
                  w_o=w_o_b[l], w_router=w_router_p, e_bias=e_bias_p,
                  moe_w1=moe_w1_b, moe_w3=moe_w3_b, moe_w2=moe_w2_b)
        subln = att_subln[l].reshape(1, HEAD_V)
        mods_p = [_Mod(mod_p, l, sec) for sec in range(6)]
        mods_s = [_Mod(mod_s, l, sec) for sec in range(6)]

        attend_p = lambda qs, kr, proj: _flash(qs, kr, proj, lam_p[l], subln, bp,
                                               min(512, tp), lam_init)
        xp, proj_p, _, (kt_all, v_all), s_fin = _group_layer(
            xp, mods_p, lw, cfg_p, zp0, sp0, attend_p, (l, depth, bp, kt_all, v_all))
        outs["zp"].append(proj_p.reshape(bp, tp, IN_W_PAD)[:, -1, COL_Z:COL_Z + RW_PROJ])
        outs["sp"].append(s_fin)

        def attend_s(qs, kr, proj):
            q_rows = jnp.tile(qs.reshape(bs, ts, ATT_W), (1, 2 * ATT_HEADS, 1))
            kn_t = jnp.pad(jnp.swapaxes(kr.reshape(bs, ts, ATT_W), 1, 2),
                           ((0, 0), (0, 0), (0, PAGE - ts)))
            vn = jnp.pad(proj[:, COL_V:COL_V + ATT_W].reshape(bs, ts * ATT_HEADS, HEAD_V),
                         ((0, 0), (0, (PAGE - ts) * ATT_HEADS), (0, 0)))
            return _decode_attn(q_rows, kn_t, vn, cache_kt, cache_v2, page_table, lam_p[l], subln, l,
                                ts, lam_init).reshape(n_s, ATT_W)

        z0_s = jnp.pad(jnp.repeat(state_shift[l], ts, axis=0), ((0, 0), (0, RW_PROJ_PAD - RW_PROJ)))
        xs, proj_s, kr_s, _, s_fin = _group_layer(xs, mods_s, lw, cfg_s, z0_s, state_wkv[l], attend_s,
                                                  None)
        outs["ks"].append(kr_s.reshape(bs, ts, ATT_HEADS, 2, HEAD_QK))
        outs["vs"].append(proj_s[:, COL_V:COL_V + ATT_W].reshape(bs, ts, ATT_HEADS, HEAD_V))
        outs["zs"].append(proj_s.reshape(bs, ts, IN_W_PAD)[:, -1, COL_Z:COL_Z + RW_PROJ])
        outs["ss"].append(s_fin)

    y_prompt = _final_norm(xp, normf_g.reshape(1, D), tm_p).reshape(bp, tp, D)
    y_sample = _final_norm(xs, normf_g.reshape(1, D), n_s).reshape(bs, ts, D)
    k_prompt = jnp.transpose(kt_all.reshape(bp, depth, ATT_HEADS, 2, HEAD_QK, tp), (0, 1, 5, 2, 3, 4))
    v_prompt = v_all.reshape(bp, depth, tp, ATT_HEADS, HEAD_V)
    return (y_prompt, y_sample, k_prompt, v_prompt,
            jnp.stack(outs["zp"], axis=0), jnp.stack(outs["sp"], axis=0),
            jnp.stack(outs["ks"], axis=1), jnp.stack(outs["vs"], axis=1),
            jnp.stack(outs["zs"], axis=0), jnp.stack(outs["ss"], axis=0))
```
